```python
import math
import jax, jax.numpy as jnp
from jax import lax
import numpy as np

D_MODEL = 1024
BATCH = 2
SEQ = 8192
DEPTH = 2

HEAD_DIM = 64
RET_HEADS = 4
FOX_HEADS = 4
RWKV_HEADS = 4
RET_W = RET_HEADS * HEAD_DIM
FOX_W = FOX_HEADS * HEAD_DIM
RWKV_W = RWKV_HEADS * HEAD_DIM
RET_CHUNK = 128
FOX_BLOCK = 128
ROPE_BASE = 10000.0
DECAY_LORA = 64
AAA_LORA = 64
GATE_LORA = 128
N_BRANCH = 3
D_FF = -(-(8 * D_MODEL) // (3 * 256)) * 256
NORM_EPS = 1e-6
RET_GN_EPS = 1e-5
RWKV_GN_EPS = 64e-5

RET_COLS = 4 * RET_W
FOX_COLS = 3 * FOX_W + FOX_HEADS
RWKV_COLS = 3 * RWKV_W + DECAY_LORA + AAA_LORA + GATE_LORA
GATE_COLS = N_BRANCH * D_MODEL
IN_COLS = RET_COLS + FOX_COLS + RWKV_COLS + GATE_COLS

kernel_name = "hybrid_retention_fox_rwkv7_gated_block"


def _rmsnorm(x, g, eps=NORM_EPS):
    xf = x.astype(jnp.float32)
    y = xf * lax.rsqrt(jnp.mean(xf * xf, axis=-1, keepdims=True) + eps)
    return (y * g.astype(jnp.float32)).astype(x.dtype)


def _head_layernorm(y, eps):
    mu = jnp.mean(y, axis=-1, keepdims=True)
    var = jnp.mean(jnp.square(y - mu), axis=-1, keepdims=True)
    return (y - mu) * lax.rsqrt(var + eps)


def _heads(t, n_heads):
    B, S, _ = t.shape
    return t.reshape(B, S, n_heads, HEAD_DIM).transpose(0, 2, 1, 3)


def _rotary(t, pos):
    half = HEAD_DIM // 2
    inv_freq = ROPE_BASE ** (-jnp.arange(half, dtype=jnp.float32) / half)
    ang = pos[:, None] * inv_freq[None, :]
    cos, sin = jnp.cos(ang), jnp.sin(ang)
    t1, t2 = t[..., :half], t[..., half:]
    return jnp.concatenate([t1 * cos - t2 * sin, t1 * sin + t2 * cos], axis=-1)


def _retention(q, k, v):
    B, H, S, Dh = q.shape
    C = RET_CHUNK
    N = S // C
    log_gamma = jnp.log1p(-jnp.exp2(-5.0 - jnp.arange(H, dtype=jnp.float32)))
    q = q.reshape(B, H, N, C, Dh)
    k = (k * Dh ** -0.5).reshape(B, H, N, C, Dh)
    v = v.reshape(B, H, N, C, Dh)
    idx = jnp.arange(C, dtype=jnp.float32)
    dist = idx[:, None] - idx[None, :]
    lg = log_gamma[:, None, None]
    dmat = jnp.where(dist >= 0, jnp.exp(lg * jnp.maximum(dist, 0.0)), 0.0)
    scores = jnp.einsum('bhnid,bhnjd->bhnij', q, k) * dmat[None, :, None]
    inner = jnp.einsum('bhnij,bhnjd->bhnid', scores, v)
    k_dec = jnp.exp(log_gamma[:, None] * (C - 1 - idx)[None, :])
    upd = jnp.einsum('bhnjd,bhnje->bhnde', k * k_dec[None, :, None, :, None], v)
    chunk_dec = jnp.exp(log_gamma * C)[None, :, None, None]

    def step(R, u):
        return R * chunk_dec + u, R

    _, r_prev = lax.scan(step, jnp.zeros((B, H, Dh, Dh), jnp.float32), jnp.moveaxis(upd, 2, 0))
    r_prev = jnp.moveaxis(r_prev, 0, 2)
    q_dec = jnp.exp(log_gamma[:, None] * (idx + 1.0)[None, :])
    cross = jnp.einsum('bhnid,bhnde->bhnie', q * q_dec[None, :, None, :, None], r_prev)
    return (inner + cross).reshape(B, H, S, Dh)


def _retention_branch(cols, gn_gain):
    B, S, _ = cols.shape
    q, k, v, g = jnp.split(cols, 4, axis=-1)
    pos = jnp.arange(S, dtype=jnp.float32)
    q = _rotary(_heads(q, RET_HEADS), pos)
    k = _rotary(_heads(k, RET_HEADS), pos)
    y = _retention(q, k, _heads(v, RET_HEADS))
    y = _head_layernorm(y, RET_GN_EPS).transpose(0, 2, 1, 3).reshape(B, S, RET_W)
    return jax.nn.silu(g) * (y * gn_gain)


def _forgetting_attention(q, k, v, log_f):
    B, H, S, Dh = q.shape
    c = jnp.cumsum(log_f, axis=-1)
    scale = Dh ** -0.5
    kpos = jnp.arange(S)
    n_blocks = S // FOX_BLOCK

    def block(i):
        t0 = i * FOX_BLOCK
        qb = lax.dynamic_slice_in_dim(q, t0, FOX_BLOCK, axis=2)
        cb = lax.dynamic_slice_in_dim(c, t0, FOX_BLOCK, axis=2)
        logits = jnp.einsum('bhqd,bhkd->bhqk', qb, k) * scale + cb[..., None] - c[:, :, None, :]
        qpos = t0 + jnp.arange(FOX_BLOCK)
        logits = jnp.where(kpos[None, :] <= qpos[:, None], logits, -jnp.inf)
        p = jax.nn.softmax(logits, axis=-1)
        return jnp.einsum('bhqk,bhkd->bhqd', p, v)

    out = lax.map(block, jnp.arange(n_blocks))
    return jnp.moveaxis(out, 0, 2).reshape(B, H, S, Dh)


def _fox_branch(cols, f_bias, q_gain, k_gain):
    B, S, _ = cols.shape
    q, k, v, f = jnp.split(cols, [FOX_W, 2 * FOX_W, 3 * FOX_W], axis=-1)
    q = _rmsnorm(_heads(q, FOX_HEADS), q_gain)
    k = _rmsnorm(_heads(k, FOX_HEADS), k_gain)
    log_f = jax.nn.log_sigmoid(f + f_bias).transpose(0, 2, 1)
    y = _forgetting_attention(q, k, _heads(v, FOX_HEADS), log_f)
    return y.transpose(0, 2, 1, 3).reshape(B, S, FOX_W)


def _rwkv7_scan(r, w, k, v, a, b):
    B, S, H, Dh = r.shape

    def step(state, inp):
        r_t, w_t, k_t, v_t, a_t, b_t = inp
        sa = jnp.einsum('bhvk,bhk->bhv', state, a_t)
        state = (state * w_t[:, :, None, :] + sa[..., None] * b_t[:, :, None, :]
                 + v_t[..., None] * k_t[:, :, None, :])
        return state, jnp.einsum('bhvk,bhk->bhv', state, r_t)

    xs = tuple(jnp.moveaxis(t, 1, 0) for t in (r, w, k, v, a, b))
    _, y = lax.scan(step, jnp.zeros((B, H, Dh, Dh), jnp.float32), xs)
    return jnp.moveaxis(y, 0, 1)


def _rwkv7_branch(cols, mu, w0, w2, a0, a2, g2, k_k, k_a, r_k, gn_gain):
    B, S, _ = cols.shape
    prev = jnp.pad(cols[:, :-1], ((0, 0), (1, 0), (0, 0)))
    cols = cols + (prev - cols) * mu
    r, k, v, xw, xa, xg = jnp.split(
        cols, [RWKV_W, 2 * RWKV_W, 3 * RWKV_W, 3 * RWKV_W + DECAY_LORA,
               3 * RWKV_W + DECAY_LORA + AAA_LORA], axis=-1)
    w_pre = -jax.nn.softplus(-(w0 + jnp.tanh(xw) @ w2)) - 0.5
    decay = jnp.exp(-jnp.exp(w_pre))
    a = jax.nn.sigmoid(a0 + xa @ a2)
    g = jax.nn.sigmoid(xg) @ g2
    hd = lambda t: t.reshape(B, S, RWKV_HEADS, HEAD_DIM)
    kk = hd(k * k_k)
    kk = kk / jnp.maximum(jnp.sqrt(jnp.sum(kk * kk, axis=-1, keepdims=True)), 1e-12)
    k = k * (1.0 + (a - 1.0) * k_a)
    r_h, k_h, v_h, a_h = hd(r), hd(k), hd(v), hd(a)
    y = _rwkv7_scan(r_h, hd(decay), k_h, v_h, -kk, kk * a_h)
    y = _head_layernorm(y, RWKV_GN_EPS).reshape(B, S, RWKV_W) * gn_gain
    r_k_h = r_k.reshape(RWKV_HEADS, HEAD_DIM)
    bonus = jnp.sum(r_h * k_h * r_k_h, axis=-1, keepdims=True) * v_h
    return (y + bonus.reshape(B, S, RWKV_W)) * g


def setup_inputs(seed: int = 0) -> dict:
    key = jax.random.key(seed)
    ks = jax.random.split(key, 32)
    f32 = jnp.float32
    nrm = lambda k, shape, s: (jax.random.normal(k, shape, f32) * s)
    L, D = DEPTH, D_MODEL
    return {
        "x": nrm(ks[0], (BATCH, SEQ, D), 1.0),
        "mix_norm": 1.0 + nrm(ks[1], (L, D), 0.02),
        "w_in": nrm(ks[2], (L, D, IN_COLS), D ** -0.5),
        "ret_gn": 1.0 + nrm(ks[3], (L, RET_W), 0.02),
        "fox_q_norm": 1.0 + nrm(ks[4], (L, HEAD_DIM), 0.02),
        "fox_k_norm": 1.0 + nrm(ks[5], (L, HEAD_DIM), 0.02),
        "fox_f_bias": 2.0 + nrm(ks[6], (L, FOX_HEADS), 0.5),
        "rwkv_mu": jax.random.uniform(ks[7], (L, RWKV_COLS), f32),
        "rwkv_w0": nrm(ks[8], (L, RWKV_W), 0.5),
        "rwkv_w2": nrm(ks[9], (L, DECAY_LORA, RWKV_W), 0.1),
        "rwkv_a0": nrm(ks[10], (L, RWKV_W), 0.1),
        "rwkv_a2": nrm(ks[11], (L, AAA_LORA, RWKV_W), 0.1),
        "rwkv_g2": nrm(ks[12], (L, GATE_LORA, RWKV_W), GATE_LORA ** -0.5),
        "rwkv_k_k": 0.85 + nrm(ks[13], (L, RWKV_W), 0.02),
        "rwkv_k_a": 1.0 + nrm(ks[14], (L, RWKV_W), 0.02),
        "rwkv_r_k": nrm(ks[15], (L, RWKV_W), 0.1),
        "rwkv_gn": 1.0 + nrm(ks[16], (L, RWKV_W), 0.02),
        "p_ret": nrm(ks[17], (L, RET_W, D), RET_W ** -0.5),
        "p_fox": nrm(ks[18], (L, FOX_W, D), FOX_W ** -0.5),
        "p_rwkv": nrm(ks[19], (L, RWKV_W, D), RWKV_W ** -0.5),
        "w_out": nrm(ks[20], (L, D, D), D ** -0.5),
        "ffn_norm": 1.0 + nrm(ks[21], (L, D), 0.02),
        "w_gate_up": nrm(ks[22], (L, D, 2 * D_FF), D ** -0.5),
        "w_down": nrm(ks[23], (L, D_FF, D), D_FF ** -0.5),
    }


def reference(x, mix_norm, w_in, ret_gn, fox_q_norm, fox_k_norm, fox_f_bias,
              rwkv_mu, rwkv_w0, rwkv_w2, rwkv_a0, rwkv_a2, rwkv_g2, rwkv_k_k, rwkv_k_a,
              rwkv_r_k, rwkv_gn, p_ret, p_fox, p_rwkv, w_out, ffn_norm, w_gate_up, w_down):
    B, S, D = x.shape
    splits = [RET_COLS, RET_COLS + FOX_COLS, RET_COLS + FOX_COLS + RWKV_COLS]
    for l in range(DEPTH):
        h = _rmsnorm(x, mix_norm[l])
        proj = (h @ w_in[l]).astype(jnp.float32)
        ret_c, fox_c, rwkv_c, gate_c = jnp.split(proj, splits, axis=-1)
        y_ret = _retention_branch(ret_c, ret_gn[l])
        y_fox = _fox_branch(fox_c, fox_f_bias[l], fox_q_norm[l], fox_k_norm[l])
        y_rwkv = _rwkv7_branch(rwkv_c, rwkv_mu[l], rwkv_w0[l], rwkv_w2[l], rwkv_a0[l],
                               rwkv_a2[l], rwkv_g2[l], rwkv_k_k[l], rwkv_k_a[l],
                               rwkv_r_k[l], rwkv_gn[l])
        gates = jax.nn.sigmoid(gate_c).astype(x.dtype).reshape(B, S, N_BRANCH, D)
        merged = (gates[:, :, 0] * (y_ret.astype(x.dtype) @ p_ret[l])
                  + gates[:, :, 1] * (y_fox.astype(x.dtype) @ p_fox[l])
                  + gates[:, :, 2] * (y_rwkv.astype(x.dtype) @ p_rwkv[l]))
        x = x + merged @ w_out[l]
        h = _rmsnorm(x, ffn_norm[l])
        gu = h @ w_gate_up[l]
        g, u = jnp.split(gu, 2, axis=-1)
        x = x + (jax.nn.silu(g) * u) @ w_down[l]
    return x
```

```python
import functools
import math

import jax
import jax.numpy as jnp
import numpy as np
from jax import lax
from jax.experimental import pallas as pl
from jax.experimental.pallas import tpu as pltpu

F32 = jnp.float32
BF16 = jnp.bfloat16

HEAD_DIM = 64
N_HEADS = 4
MIX_W = N_HEADS * HEAD_DIM
RET_CHUNK = 128
ROPE_BASE = 10000.0
DECAY_LORA = 64
AAA_LORA = 64
GATE_LORA = 128
NORM_EPS = 1e-6
RET_GN_EPS = 1e-5
RWKV_GN_EPS = 64e-5
RWKV_CHUNK = 64
RWKV_SUB = 16
LANES = 128
VMEM_LIMIT = 52 * 1024 * 1024


def _cparams(sem):
    return pltpu.CompilerParams(dimension_semantics=sem, vmem_limit_bytes=VMEM_LIMIT)


def _resident(shape):
    nd = len(shape)
    return pl.BlockSpec(shape, lambda *_: (0,) * nd, pipeline_mode=pl.Buffered(1))


def _mm(a, b):
    return jnp.dot(a.astype(BF16), b.astype(BF16), preferred_element_type=F32)


def _mm_nt(a, b):
    return lax.dot_general(a.astype(BF16), b.astype(BF16), (((1,), (1,)), ((), ())),
                           preferred_element_type=F32)


def _mm_tn(a, b):
    return lax.dot_general(a.astype(BF16), b.astype(BF16), (((0,), (0,)), ((), ())),
                           preferred_element_type=F32)


def _mask_mm_f32(mask_bf16, x):
    h1 = x.astype(BF16)
    r1 = x - h1.astype(F32)
    h2 = r1.astype(BF16)
    h3 = (r1 - h2.astype(F32)).astype(BF16)
    dot = lambda h: jnp.dot(mask_bf16, h, preferred_element_type=F32)
    return dot(h1) + dot(h2) + dot(h3)


def _sigmoid(x):
    return 1.0 / (1.0 + jnp.exp(-x))


def _log_sigmoid(x):
    return jnp.minimum(x, 0.0) - jnp.log1p(jnp.exp(-jnp.abs(x)))


def _iota2(shape, dim):
    return lax.broadcasted_iota(jnp.int32, shape, dim)


RET_COLS = 4 * MIX_W
FOXQKV_COLS = 3 * MIX_W
RWKV_COLS = 3 * MIX_W + DECAY_LORA + AAA_LORA + GATE_LORA
IN_OFF = np.cumsum([0, RET_COLS, FOXQKV_COLS, RWKV_COLS])


def _in_proj_kernel(x_ref, g_ref, w_ref, ret_ref, fox_ref, rwkv_ref, gate_ref, f_ref, *, d_model):
    x = x_ref[...]
    h = x * lax.rsqrt(jnp.mean(x * x, axis=-1, keepdims=True) + NORM_EPS) * g_ref[...]
    hb = h.astype(BF16)
    o0, o1, o2, o3 = (int(v) for v in IN_OFF)
    o4 = o3 + 3 * d_model
    dot = lambda lo, hi: jnp.dot(hb, w_ref[:, lo:hi], preferred_element_type=F32)
    ret_ref[...] = dot(o0, o1)
    fox_ref[...] = dot(o1, o2)
    rwkv_ref[...] = dot(o2, o3)
    gate_ref[...] = dot(o3, o4)
    f_ref[...] = dot(o4, o4 + LANES)


def _in_proj(x, gain, w_cat, tm=256):
    B, S, D = x.shape
    ncol = w_cat.shape[1]
    row = lambda w: pl.BlockSpec((None, tm, w), lambda b, i: (b, i, 0))
    widths = (RET_COLS, FOXQKV_COLS, RWKV_COLS, 3 * D, LANES)
    return pl.pallas_call(
        functools.partial(_in_proj_kernel, d_model=D),
        out_shape=tuple(jax.ShapeDtypeStruct((B, S, w), F32) for w in widths),
        grid=(B, S // tm),
        in_specs=[row(D), _resident((1, D)), _resident((D, ncol))],
        out_specs=tuple(row(w) for w in widths),
        compiler_params=_cparams(("parallel", "parallel")),
        name="in_proj",
    )(x, gain, w_cat)


def _retention_kernel(c_ref, cos_ref, sin_ref, gn_ref, o_ref, state_ref):
    C = RET_CHUNK

    @pl.when(pl.program_id(1) == 0)
    def _():
        state_ref[...] = jnp.zeros_like(state_ref)

    blk = c_ref[...]
    cos = cos_ref[...]
    sin = sin_ref[...]
    first_half = (_iota2((C, MIX_W), 1) & (HEAD_DIM - 1)) < HEAD_DIM // 2

    def rope(t):
        partner = jnp.where(first_half, pltpu.roll(t, MIX_W - HEAD_DIM // 2, 1),
                            pltpu.roll(t, HEAD_DIM // 2, 1))
        return t * cos + partner * sin

    q = rope(blk[:, 0:MIX_W])
    k = rope(blk[:, MIX_W:2 * MIX_W]) * (HEAD_DIM ** -0.5)
    v = blk[:, 2 * MIX_W:3 * MIX_W]
    g = blk[:, 3 * MIX_W:4 * MIX_W]
    gate = g * _sigmoid(g)
    gn = gn_ref[...]

    ii = _iota2((C, C), 0)
    jj = _iota2((C, C), 1)
    dist = jnp.maximum(ii - jj, 0).astype(F32)
    causal = ii >= jj
    row = _iota2((C, HEAD_DIM), 0).astype(F32)

    for h in range(N_HEADS):
        sl = slice(h * HEAD_DIM, (h + 1) * HEAD_DIM)
        log_gamma = math.log1p(-2.0 ** (-5.0 - h))
        qh, kh, vh = q[:, sl], k[:, sl], v[:, sl]
        dmat = jnp.where(causal, jnp.exp(log_gamma * dist), 0.0)
        scores = _mm_nt(qh, kh) * dmat
        inner = _mm(scores, vh)
        r_prev = state_ref[h]
        cross = _mm(qh * jnp.exp(log_gamma * (row + 1.0)), r_prev)
        k_dec = jnp.exp(log_gamma * (C - 1.0 - row))
        state_ref[h] = r_prev * math.exp(log_gamma * C) + _mm_tn(kh * k_dec, vh)
        y = inner + cross
        mu = jnp.mean(y, axis=-1, keepdims=True)
        yc = y - mu
        var = jnp.mean(yc * yc, axis=-1, keepdims=True)
        yn = yc * lax.rsqrt(var + RET_GN_EPS)
        o_ref[:, sl] = (gate[:, sl] * (yn * gn[:, sl])).astype(o_ref.dtype)


def _retention(ret_c, cos, sin, gn):
    B, S, W = ret_c.shape
    C = RET_CHUNK
    return pl.pallas_call(
        _retention_kernel,
        out_shape=jax.ShapeDtypeStruct((B, S, MIX_W), BF16),
        grid=(B, S // C),
        in_specs=[pl.BlockSpec((None, C, W), lambda b, i: (b, i, 0)),
                  pl.BlockSpec((C, MIX_W), lambda b, i: (i, 0)),
                  pl.BlockSpec((C, MIX_W), lambda b, i: (i, 0)),
                  _resident((1, MIX_W))],
        out_specs=pl.BlockSpec((None, C, MIX_W), lambda b, i: (b, i, 0)),
        scratch_shapes=[pltpu.VMEM((N_HEADS, HEAD_DIM, HEAD_DIM), F32)],
        compiler_params=_cparams(("parallel", "arbitrary")),
        name="retention",
    )(ret_c, cos, sin, gn)


def _rope_tables(S):
    half = HEAD_DIM // 2
    inv_freq = ROPE_BASE ** (-jnp.arange(half, dtype=F32) / half)
    ang = jnp.arange(S, dtype=F32)[:, None] * inv_freq[None, :]
    cos, sin = jnp.cos(ang), jnp.sin(ang)
    cos_h = jnp.concatenate([cos, cos], axis=-1)
    sin_h = jnp.concatenate([-sin, sin], axis=-1)
    return jnp.tile(cos_h, (1, N_HEADS)), jnp.tile(sin_h, (1, N_HEADS))


def _fox_prep_kernel(c_ref, qg_ref, kg_ref, q_ref, k_ref, v_ref):
    blk = c_ref[...]

    def rms(t, gain):
        return t * lax.rsqrt(jnp.mean(t * t, axis=-1, keepdims=True) + NORM_EPS) * gain

    for h in range(N_HEADS):
        lo = h * HEAD_DIM
        qh = rms(blk[:, lo:lo + HEAD_DIM], qg_ref[...]) * (HEAD_DIM ** -0.5)
        kh = rms(blk[:, MIX_W + lo:MIX_W + lo + HEAD_DIM], kg_ref[...])
        q_ref[h] = qh.astype(BF16)
        k_ref[h] = kh.astype(BF16)
        v_ref[h] = blk[:, 2 * MIX_W + lo:2 * MIX_W + lo + HEAD_DIM].astype(BF16)


def _fox_prep(fox_c, q_gain, k_gain, ts=512):
    B, S, W = fox_c.shape
    head_spec = pl.BlockSpec((None, N_HEADS, ts, HEAD_DIM), lambda b, i: (b, 0, i, 0))
    shp = jax.ShapeDtypeStruct((B, N_HEADS, S, HEAD_DIM), BF16)
    return pl.pallas_call(
        _fox_prep_kernel,
        out_shape=(shp, shp, shp),
        grid=(B, S // ts),
        in_specs=[pl.BlockSpec((None, ts, W), lambda b, i: (b, i, 0)),
                  _resident((1, HEAD_DIM)), _resident((1, HEAD_DIM))],
        out_specs=(head_spec, head_spec, head_spec),
        compiler_params=_cparams(("parallel", "parallel")),
        name="fox_prep",
    )(fox_c, q_gain, k_gain)


def _fox_cumsum_kernel(f_ref, bias_ref, c_ref, carry_ref, *, ts):
    @pl.when(pl.program_id(1) == 0)
    def _():
        carry_ref[...] = jnp.zeros_like(carry_ref)

    log_f = _log_sigmoid(f_ref[...] + bias_ref[...])
    tril = (_iota2((ts, ts), 0) >= _iota2((ts, ts), 1)).astype(BF16)
    c = _mask_mm_f32(tril, log_f) + carry_ref[0:1, :]
    c_ref[...] = c
    carry_ref[0:1, :] = c[ts - 1:ts, :]


def _fox_cumsum(f_c, bias, ts=512):
    B, S, W = f_c.shape
    return pl.pallas_call(
        functools.partial(_fox_cumsum_kernel, ts=ts),
        out_shape=jax.ShapeDtypeStruct((B, S, W), F32),
        grid=(B, S // ts),
        in_specs=[pl.BlockSpec((None, ts, W), lambda b, i: (b, i, 0)), _resident((1, W))],
        out_specs=pl.BlockSpec((None, ts, W), lambda b, i: (b, i, 0)),
        scratch_shapes=[pltpu.VMEM((8, W), F32)],
        compiler_params=_cparams(("parallel", "arbitrary")),
        name="fox_cumsum",
    )(f_c, bias)


def _fox_attn_kernel(q_ref, k_ref, v_ref, cq_ref, ck_ref, o_ref, *, tq):
    qi = pl.program_id(2)
    q = q_ref[...]
    cq = cq_ref[...]

    def step(j, carry, diagonal):
        m, l, acc = carry
        start = pl.multiple_of(j * tq, tq)
        ks = k_ref[pl.ds(start, tq), :]
        vs = v_ref[pl.ds(start, tq), :]
        ck = ck_ref[pl.ds(j, 1), :]
        s = _mm_nt(q, ks) + (cq - ck)
        if diagonal:
            s = jnp.where(_iota2((tq, tq), 0) >= _iota2((tq, tq), 1), s, -jnp.inf)
        m_new = jnp.maximum(m, jnp.max(s, axis=-1, keepdims=True))
        alpha = jnp.exp(m - m_new)
        p = jnp.exp(s - m_new)
        l = alpha * l + jnp.sum(p, axis=-1, keepdims=True)
        acc = alpha * acc + jnp.dot(p.astype(BF16), vs, preferred_element_type=F32)
        return m_new, l, acc

    init = (jnp.full((tq, 1), -jnp.inf, F32), jnp.zeros((tq, 1), F32),
            jnp.zeros((tq, HEAD_DIM), F32))
    carry = lax.fori_loop(0, qi, lambda j, c: step(j, c, False), init)
    _, l, acc = step(qi, carry, True)
    o_ref[...] = (acc / l).astype(o_ref.dtype)


def _fox_attn(q, k, v, c_col, c_row, tq=512):
    B, H, S, Dh = q.shape
    nq = S // tq
    return pl.pallas_call(
        functools.partial(_fox_attn_kernel, tq=tq),
        out_shape=jax.ShapeDtypeStruct((B, H, S, Dh), BF16),
        grid=(B, H, nq),
        in_specs=[pl.BlockSpec((None, None, tq, Dh), lambda b, h, i: (b, h, i, 0)),
                  pl.BlockSpec((None, None, S, Dh), lambda b, h, i: (b, h, 0, 0)),
                  pl.BlockSpec((None, None, S, Dh), lambda b, h, i: (b, h, 0, 0)),
                  pl.BlockSpec((None, None, tq, 1), lambda b, h, i: (b, h, i, 0)),
                  pl.BlockSpec((None, None, nq, tq), lambda b, h, i: (b, h, 0, 0))],
        out_specs=pl.BlockSpec((None, None, tq, Dh), lambda b, h, i: (b, h, i, 0)),
        compiler_params=_cparams(("parallel", "parallel", "arbitrary")),
        name="fox_attn",
    )(q, k, v, c_col, c_row)


def _unit_lower_inverse(a_strict):
    C = a_strict.shape[0]
    ii = _iota2((C, C), 0)
    jj = _iota2((C, C), 1)
    shift = int(math.log2(RWKV_SUB))
    same_blk = (ii >> shift) == (jj >> shift)
    eye = (ii == jj).astype(F32)
    d = jnp.where(same_blk, a_strict, 0.0)
    low = a_strict - d
    x = eye + d
    dp = d
    for _ in range(shift - 1):
        dp = _mm(dp, dp)
        x = x + _mm(x, dp)
    n = _mm(x, low)
    y = eye + n
    npow = n
    for _ in range(int(math.log2(C // RWKV_SUB)) - 1):
        npow = _mm(npow, npow)
        y = y + _mm(y, npow)
    return _mm(y, x)


def _rwkv_kernel(c_ref, mu_ref, w0_ref, w2_ref, a0_ref, a2_ref, g2_ref, kk_ref, ka_ref, rk_ref,
                 gn_ref, o_ref, state_ref, carry_ref, y_ref, *, tb):
    C = RWKV_CHUNK
    W = MIX_W

    @pl.when(pl.program_id(1) == 0)
    def _():
        state_ref[...] = jnp.zeros_like(state_ref)
        carry_ref[...] = jnp.zeros_like(carry_ref)

    cur = c_ref[...]
    prev = jnp.where(_iota2(cur.shape, 0) == 0, carry_ref[0:1, :], pltpu.roll(cur, 1, 0))
    carry_ref[0:1, :] = cur[tb - 1:tb, :]
    mixed = cur + (prev - cur) * mu_ref[...]
    r = mixed[:, 0:W]
    k = mixed[:, W:2 * W]
    v = mixed[:, 2 * W:3 * W]
    xw = mixed[:, 3 * W:3 * W + DECAY_LORA]
    xa = mixed[:, 3 * W + DECAY_LORA:3 * W + DECAY_LORA + AAA_LORA]
    xg = mixed[:, 3 * W + DECAY_LORA + AAA_LORA:]

    log_w = -jnp.exp(_log_sigmoid(w0_ref[...] + _mm(jnp.tanh(xw), w2_ref[...])) - 0.5)
    a = _sigmoid(a0_ref[...] + _mm(xa, a2_ref[...]))
    g = _mm(_sigmoid(xg), g2_ref[...])
    kk = k * kk_ref[...]
    k2 = k * (1.0 + (a - 1.0) * ka_ref[...])

    heads = [slice(h * HEAD_DIM, (h + 1) * HEAD_DIM) for h in range(N_HEADS)]
    kk = jnp.concatenate(
        [kk[:, sl] / jnp.maximum(jnp.sqrt(jnp.sum(kk[:, sl] * kk[:, sl], axis=-1, keepdims=True)),
                                 1e-12) for sl in heads], axis=-1)
    a_vec = -kk
    b_vec = kk * a

    ii = _iota2((C, C), 0)
    jj = _iota2((C, C), 1)
    strict = ii > jj
    incl = ii >= jj
    tril = incl.astype(BF16)

    for c in range(tb // C):
        rows = slice(c * C, (c + 1) * C)
        lw = log_w[rows]
        cum = _mask_mm_f32(tril, lw)
        e_pos = jnp.exp(cum)
        e_neg = jnp.exp(-cum)
        p_end = e_pos[C - 1:C, :]
        at = a_vec[rows] * jnp.exp(cum - lw)
        rt = r[rows] * e_pos
        bt = b_vec[rows] * e_neg
        kt = k2[rows] * e_neg
        bp = bt * p_end
        kp = kt * p_end
        vc = v[rows]
        for h, sl in enumerate(heads):
            s0 = state_ref[h]
            a_ab = jnp.where(strict, _mm_nt(at[:, sl], bt[:, sl]), 0.0)
            a_ak = jnp.where(strict, _mm_nt(at[:, sl], kt[:, sl]), 0.0)
            a_rb = jnp.where(incl, _mm_nt(rt[:, sl], bt[:, sl]), 0.0)
            a_rk = jnp.where(incl, _mm_nt(rt[:, sl], kt[:, sl]), 0.0)
            t_inv = _unit_lower_inverse(a_ab)
            vh = vc[:, sl]
            u = _mm(t_inv, _mm_nt(at[:, sl], s0) + _mm(a_ak, vh))
            y_ref[rows, sl] = _mm_nt(rt[:, sl], s0) + _mm(a_rb, u) + _mm(a_rk, vh)
            state_ref[h] = s0 * p_end[:, sl] + _mm_tn(u, bp[:, sl]) + _mm_tn(vh, kp[:, sl])

    y = y_ref[...]
    gn = gn_ref[...]
    rk = r * k2 * rk_ref[...]
    for sl in heads:
        yh = y[:, sl]
        mean = jnp.mean(yh, axis=-1, keepdims=True)
        yc = yh - mean
        var = jnp.mean(yc * yc, axis=-1, keepdims=True)
        yn = yc * lax.rsqrt(var + RWKV_GN_EPS) * gn[:, sl]
        bonus = jnp.sum(rk[:, sl], axis=-1, keepdims=True) * v[:, sl]
        o_ref[:, sl] = ((yn + bonus) * g[:, sl]).astype(o_ref.dtype)


def _rwkv(rwkv_c, mu, w0, w2, a0, a2, g2, k_k, k_a, r_k, gn, tb=128):
    B, S, W = rwkv_c.shape
    vec = lambda t: _resident((1, t.shape[-1]))
    mat = lambda t: _resident(t.shape)
    return pl.pallas_call(
        functools.partial(_rwkv_kernel, tb=tb),
        out_shape=jax.ShapeDtypeStruct((B, S, MIX_W), BF16),
        grid=(B, S // tb),
        in_specs=[pl.BlockSpec((None, tb, W), lambda b, i: (b, i, 0)),
                  vec(mu), vec(w0), mat(w2), vec(a0), mat(a2), mat(g2), vec(k_k), vec(k_a),
                  vec(r_k), vec(gn)],
        out_specs=pl.BlockSpec((None, tb, MIX_W), lambda b, i: (b, i, 0)),
        scratch_shapes=[pltpu.VMEM((N_HEADS, HEAD_DIM, HEAD_DIM), F32),
                        pltpu.VMEM((8, W), F32),
                        pltpu.VMEM((tb, MIX_W), F32)],
        compiler_params=_cparams(("parallel", "arbitrary")),
        name="rwkv7",
    )(rwkv_c, mu, w0, w2, a0, a2, g2, k_k, k_a, r_k, gn)


def _merge_kernel(x_ref, gate_ref, yret_ref, yfox_ref, yrwkv_ref, pret_ref, pfox_ref, prwkv_ref,
                  wout_ref, o_ref, *, d_model):
    D = d_model
    dot = lambda a, b: jnp.dot(a, b, preferred_element_type=F32)
    gates = gate_ref[...]
    fox = dot(yfox_ref[0], pfox_ref[0:HEAD_DIM, :])
    for h in range(1, N_HEADS):
        fox = fox + dot(yfox_ref[h], pfox_ref[h * HEAD_DIM:(h + 1) * HEAD_DIM, :])
    merged = (_sigmoid(gates[:, 0:D]) * dot(yret_ref[...], pret_ref[...])
              + _sigmoid(gates[:, D:2 * D]) * fox
              + _sigmoid(gates[:, 2 * D:3 * D]) * dot(yrwkv_ref[...], prwkv_ref[...]))
    o_ref[...] = x_ref[...] + dot(merged.astype(BF16), wout_ref[...])


def _merge(x, gates, y_ret, y_fox, y_rwkv, p_ret, p_fox, p_rwkv, w_out, tm=512):
    B, S, D = x.shape
    row = lambda w: pl.BlockSpec((None, tm, w), lambda b, i: (b, i, 0))
    return pl.pallas_call(
        functools.partial(_merge_kernel, d_model=D),
        out_shape=jax.ShapeDtypeStruct((B, S, D), F32),
        grid=(B, S // tm),
        in_specs=[row(D), row(3 * D), row(MIX_W),
                  pl.BlockSpec((None, N_HEADS, tm, HEAD_DIM), lambda b, i: (b, 0, i, 0)),
                  row(MIX_W), _resident(p_ret.shape), _resident(p_fox.shape),
                  _resident(p_rwkv.shape), _resident(w_out.shape)],
        out_specs=row(D),
        compiler_params=_cparams(("parallel", "parallel")),
        name="merge_out",
    )(x, gates, y_ret, y_fox, y_rwkv, p_ret, p_fox, p_rwkv, w_out)


def _ffn_kernel(x_ref, g_ref, wgu_ref, wd_ref, o_ref, *, d_ff, ck):
    x = x_ref[...]
    h = (x * lax.rsqrt(jnp.mean(x * x, axis=-1, keepdims=True) + NORM_EPS) * g_ref[...]).astype(BF16)
    acc = x
    for c in range(d_ff // ck):
        lo = c * ck
        gate = jnp.dot(h, wgu_ref[:, lo:lo + ck], preferred_element_type=F32)
        up = jnp.dot(h, wgu_ref[:, d_ff + lo:d_ff + lo + ck], preferred_element_type=F32)
        act = (gate * _sigmoid(gate) * up).astype(BF16)
        acc = acc + jnp.dot(act, wd_ref[lo:lo + ck, :], preferred_element_type=F32)
    o_ref[...] = acc


def _ffn(x, gain, w_gate_up, w_down, tm=512, ck=256):
    B, S, D = x.shape
    d_ff = w_down.shape[0]
    row = pl.BlockSpec((None, tm, D), lambda b, i: (b, i, 0))
    return pl.pallas_call(
        functools.partial(_ffn_kernel, d_ff=d_ff, ck=ck),
        out_shape=jax.ShapeDtypeStruct((B, S, D), F32),
        grid=(B, S // tm),
        in_specs=[row, _resident((1, D)), _resident(w_gate_up.shape), _resident(w_down.shape)],
        out_specs=row,
        compiler_params=_cparams(("parallel", "parallel")),
        name="ffn",
    )(x, gain, w_gate_up, w_down)


def kernel(x, mix_norm, w_in, ret_gn, fox_q_norm, fox_k_norm, fox_f_bias, rwkv_mu, rwkv_w0, rwkv_w2,
           rwkv_a0, rwkv_a2, rwkv_g2, rwkv_k_k, rwkv_k_a, rwkv_r_k, rwkv_gn, p_ret, p_fox, p_rwkv,
           w_out, ffn_norm, w_gate_up, w_down):
    B, S, D = x.shape
    depth = w_in.shape[0]
    cos, sin = _rope_tables(S)
    o0, o1, o2, o3 = (int(v) for v in IN_OFF)
    f_lo = o1 + FOXQKV_COLS
    f_hi = f_lo + N_HEADS
    row = lambda t: t.reshape(1, -1)
    tq = 512
    for l in range(depth):
        w = w_in[l]
        w_cat = jnp.concatenate(
            [w[:, o0:f_lo], w[:, f_hi:], jnp.pad(w[:, f_lo:f_hi], ((0, 0), (0, LANES - N_HEADS)))],
            axis=1).astype(BF16)
        ret_c, fox_c, rwkv_c, gate_c, f_c = _in_proj(x, row(mix_norm[l]), w_cat)

        y_ret = _retention(ret_c, cos, sin, row(ret_gn[l]))

        fq, fk, fv = _fox_prep(fox_c, row(fox_q_norm[l]), row(fox_k_norm[l]))
        f_bias = jnp.pad(fox_f_bias[l], (0, LANES - N_HEADS)).reshape(1, LANES)
        c = _fox_cumsum(f_c, f_bias)[:, :, :N_HEADS]
        c = c.transpose(0, 2, 1)
        y_fox = _fox_attn(fq, fk, fv, c.reshape(B, N_HEADS, S, 1),
                          c.reshape(B, N_HEADS, S // tq, tq), tq=tq)

        y_rwkv = _rwkv(rwkv_c, row(rwkv_mu[l]), row(rwkv_w0[l]), rwkv_w2[l].astype(BF16),
                       row(rwkv_a0[l]), rwkv_a2[l].astype(BF16), rwkv_g2[l].astype(BF16),
                       row(rwkv_k_k[l]), row(rwkv_k_a[l]), row(rwkv_r_k[l]), row(rwkv_gn[l]))

        x = _merge(x, gate_c, y_ret, y_fox, y_rwkv, p_ret[l].astype(BF16), p_fox[l].astype(BF16),
                   p_rwkv[l].astype(BF16), w_out[l].astype(BF16))
        x = _ffn(x, row(ffn_norm[l]), w_gate_up[l].astype(BF16), w_down[l].astype(BF16))
    return x
```

```python
import functools
import math

import jax
import jax.numpy as jnp
import numpy as np
from jax import lax
from jax.experimental import pallas as pl
from jax.experimental.pallas import tpu as pltpu

F32 = jnp.float32
BF16 = jnp.bfloat16

HEAD_DIM = 64
N_HEADS = 4
MIX_W = N_HEADS * HEAD_DIM
RET_CHUNK = 128
ROPE_BASE = 10000.0
DECAY_LORA = 64
AAA_LORA = 64
GATE_LORA = 128
NORM_EPS = 1e-6
RET_GN_EPS = 1e-5
RWKV_GN_EPS = 64e-5
RWKV_CHUNK = 64
RWKV_SUB = 16
LANES = 128
VMEM_LIMIT = 52 * 1024 * 1024


def _cparams(sem):
    return pltpu.CompilerParams(dimension_semantics=sem, vmem_limit_bytes=VMEM_LIMIT)


def _resident(shape):
    nd = len(shape)
    return pl.BlockSpec(shape, lambda *_: (0,) * nd, pipeline_mode=pl.Buffered(1))


def _mm(a, b):
    return jnp.dot(a.astype(BF16), b.astype(BF16), preferred_element_type=F32)


def _mm_nt(a, b):
    return lax.dot_general(a.astype(BF16), b.astype(BF16), (((1,), (1,)), ((), ())),
                           preferred_element_type=F32)


def _mm_tn(a, b):
    return lax.dot_general(a.astype(BF16), b.astype(BF16), (((0,), (0,)), ((), ())),
                           preferred_element_type=F32)


def _mask_mm_f32(mask_bf16, x):
    h1 = x.astype(BF16)
    r1 = x - h1.astype(F32)
    h2 = r1.astype(BF16)
    h3 = (r1 - h2.astype(F32)).astype(BF16)
    dot = lambda h: jnp.dot(mask_bf16, h, preferred_element_type=F32)
    return dot(h1) + dot(h2) + dot(h3)


def _sigmoid(x):
    return 1.0 / (1.0 + jnp.exp(-x))


def _log_sigmoid(x):
    return jnp.minimum(x, 0.0) - jnp.log1p(jnp.exp(-jnp.abs(x)))


def _iota2(shape, dim):
    return lax.broadcasted_iota(jnp.int32, shape, dim)


RET_COLS = 4 * MIX_W
FOXQKV_COLS = 3 * MIX_W
RWKV_COLS = 3 * MIX_W + DECAY_LORA + AAA_LORA + GATE_LORA
IN_OFF = np.cumsum([0, RET_COLS, FOXQKV_COLS, RWKV_COLS])


def _in_proj_kernel(x_ref, g_ref, w_ref, ret_ref, fox_ref, rwkv_ref, gate_ref, f_ref, *, d_model):
    x = x_ref[...]
    h = x * lax.rsqrt(jnp.mean(x * x, axis=-1, keepdims=True) + NORM_EPS) * g_ref[...]
    hb = h.astype(BF16)
    o0, o1, o2, o3 = (int(v) for v in IN_OFF)
    o4 = o3 + 3 * d_model
    dot = lambda lo, hi: jnp.dot(hb, w_ref[:, lo:hi], preferred_element_type=F32)
    ret_ref[...] = dot(o0, o1)
    fox_ref[...] = dot(o1, o2)
    rwkv_ref[...] = dot(o2, o3)
    gate_ref[...] = dot(o3, o4)
    f_ref[...] = dot(o4, o4 + LANES)


def _in_proj(x, gain, w_cat, tm=256):
    B, S, D = x.shape
    ncol = w_cat.shape[1]
    row = lambda w: pl.BlockSpec((None, tm, w), lambda b, i: (b, i, 0))
    widths = (RET_COLS, FOXQKV_COLS, RWKV_COLS, 3 * D, LANES)
    return pl.pallas_call(
        functools.partial(_in_proj_kernel, d_model=D),
        out_shape=tuple(jax.ShapeDtypeStruct((B, S, w), F32) for w in widths),
        grid=(B, S // tm),
        in_specs=[row(D), _resident((1, D)), _resident((D, ncol))],
        out_specs=tuple(row(w) for w in widths),
        compiler_params=_cparams(("parallel", "parallel")),
        name="in_proj",
    )(x, gain, w_cat)


def _retention_kernel(c_ref, cos_ref, sin_ref, gn_ref, o_ref, state_ref):
    C = RET_CHUNK

    @pl.when(pl.program_id(1) == 0)
    def _():
        state_ref[...] = jnp.zeros_like(state_ref)

    blk = c_ref[...]
    cos = cos_ref[...]
    sin = sin_ref[...]
    first_half = (_iota2((C, MIX_W), 1) & (HEAD_DIM - 1)) < HEAD_DIM // 2

    def rope(t):
        partner = jnp.where(first_half, pltpu.roll(t, MIX_W - HEAD_DIM // 2, 1),
                            pltpu.roll(t, HEAD_DIM // 2, 1))
        return t * cos + partner * sin

    q = rope(blk[:, 0:MIX_W])
    k = rope(blk[:, MIX_W:2 * MIX_W]) * (HEAD_DIM ** -0.5)
    v = blk[:, 2 * MIX_W:3 * MIX_W]
    g = blk[:, 3 * MIX_W:4 * MIX_W]
    gate = g * _sigmoid(g)
    gn = gn_ref[...]

    ii = _iota2((C, C), 0)
    jj = _iota2((C, C), 1)
    dist = jnp.maximum(ii - jj, 0).astype(F32)
    causal = ii >= jj
    row = _iota2((C, HEAD_DIM), 0).astype(F32)

    for h in range(N_HEADS):
        sl = slice(h * HEAD_DIM, (h + 1) * HEAD_DIM)
        log_gamma = math.log1p(-2.0 ** (-5.0 - h))
        qh, kh, vh = q[:, sl], k[:, sl], v[:, sl]
        dmat = jnp.where(causal, jnp.exp(log_gamma * dist), 0.0)
        scores = _mm_nt(qh, kh) * dmat
        inner = _mm(scores, vh)
        r_prev = state_ref[h]
        cross = _mm(qh * jnp.exp(log_gamma * (row + 1.0)), r_prev)
        k_dec = jnp.exp(log_gamma * (C - 1.0 - row))
        state_ref[h] = r_prev * math.exp(log_gamma * C) + _mm_tn(kh * k_dec, vh)
        y = inner + cross
        mu = jnp.mean(y, axis=-1, keepdims=True)
        yc = y - mu
        var = jnp.mean(yc * yc, axis=-1, keepdims=True)
        yn = yc * lax.rsqrt(var + RET_GN_EPS)
        o_ref[:, sl] = (gate[:, sl] * (yn * gn[:, sl])).astype(o_ref.dtype)


def _retention(ret_c, cos, sin, gn):
    B, S, W = ret_c.shape
    C = RET_CHUNK
    return pl.pallas_call(
        _retention_kernel,
        out_shape=jax.ShapeDtypeStruct((B, S, MIX_W), BF16),
        grid=(B, S // C),
        in_specs=[pl.BlockSpec((None, C, W), lambda b, i: (b, i, 0)),
                  pl.BlockSpec((C, MIX_W), lambda b, i: (i, 0)),
                  pl.BlockSpec((C, MIX_W), lambda b, i: (i, 0)),
                  _resident((1, MIX_W))],
        out_specs=pl.BlockSpec((None, C, MIX_W), lambda b, i: (b, i, 0)),
        scratch_shapes=[pltpu.VMEM((N_HEADS, HEAD_DIM, HEAD_DIM), F32)],
        compiler_params=_cparams(("parallel", "arbitrary")),
        name="retention",
    )(ret_c, cos, sin, gn)


def _rope_tables(S):
    half = HEAD_DIM // 2
    inv_freq = ROPE_BASE ** (-jnp.arange(half, dtype=F32) / half)
    ang = jnp.arange(S, dtype=F32)[:, None] * inv_freq[None, :]
    cos, sin = jnp.cos(ang), jnp.sin(ang)
    cos_h = jnp.concatenate([cos, cos], axis=-1)
    sin_h = jnp.concatenate([-sin, sin], axis=-1)
    return jnp.tile(cos_h, (1, N_HEADS)), jnp.tile(sin_h, (1, N_HEADS))


LOG2E = math.log2(math.e)
FOX_AUG = HEAD_DIM


def _fox_prep_kernel(c_ref, f_ref, bias_ref, qg_ref, kg_ref, q_ref, k_ref, v_ref, carry_ref, *, ts):
    @pl.when(pl.program_id(1) == 0)
    def _():
        carry_ref[...] = jnp.zeros_like(carry_ref)

    log_f = _log_sigmoid(f_ref[...] + bias_ref[...])
    tril = (_iota2((ts, ts), 0) >= _iota2((ts, ts), 1)).astype(BF16)
    c = _mask_mm_f32(tril, log_f) + carry_ref[0:1, :]
    carry_ref[0:1, :] = c[ts - 1:ts, :]
    c = c * LOG2E

    blk = c_ref[...]
    lane = _iota2((ts, LANES), 1)
    zeros = jnp.zeros((ts, LANES - HEAD_DIM), F32)
    pad = lambda t: jnp.concatenate([t, zeros], axis=1)

    def rms(t, gain):
        return t * lax.rsqrt(jnp.mean(t * t, axis=-1, keepdims=True) + NORM_EPS) * gain

    def at_lanes(lo, vals, rest):
        out = rest
        for i, val in reversed(list(enumerate(vals))):
            out = jnp.where(lane == lo + i, val, out)
        return out

    for h in range(N_HEADS):
        lo = h * HEAD_DIM
        ch = jnp.broadcast_to(c[:, h:h + 1], (ts, LANES))
        c1 = ch.astype(BF16).astype(F32)
        r1 = ch - c1
        c2 = r1.astype(BF16).astype(F32)
        c3 = (r1 - c2).astype(BF16).astype(F32)
        qh = rms(blk[:, lo:lo + HEAD_DIM], qg_ref[...]) * (LOG2E * HEAD_DIM ** -0.5)
        kh = rms(blk[:, MIX_W + lo:MIX_W + lo + HEAD_DIM], kg_ref[...])
        vh = blk[:, 2 * MIX_W + lo:2 * MIX_W + lo + HEAD_DIM]
        one = jnp.ones((ts, LANES), F32)
        q_aug = at_lanes(FOX_AUG, [c1, c2, c3, one, one, one], 0.0)
        k_aug = at_lanes(FOX_AUG, [one, one, one, -c1, -c2, -c3], 0.0)
        v_aug = at_lanes(FOX_AUG, [one], 0.0)
        q_ref[h] = jnp.where(lane < HEAD_DIM, pad(qh), q_aug).astype(BF16)
        k_ref[h] = jnp.where(lane < HEAD_DIM, pad(kh), k_aug).astype(BF16)
        v_ref[h] = jnp.where(lane < HEAD_DIM, pad(vh), v_aug).astype(BF16)


def _fox_prep(fox_c, f_c, f_bias, q_gain, k_gain, ts=512):
    B, S, W = fox_c.shape
    head_spec = pl.BlockSpec((None, N_HEADS, ts, LANES), lambda b, i: (b, 0, i, 0))
    shp = jax.ShapeDtypeStruct((B, N_HEADS, S, LANES), BF16)
    return pl.pallas_call(
        functools.partial(_fox_prep_kernel, ts=ts),
        out_shape=(shp, shp, shp),
        grid=(B, S // ts),
        in_specs=[pl.BlockSpec((None, ts, W), lambda b, i: (b, i, 0)),
                  pl.BlockSpec((None, ts, LANES), lambda b, i: (b, i, 0)),
                  _resident((1, LANES)), _resident((1, HEAD_DIM)), _resident((1, HEAD_DIM))],
        out_specs=(head_spec, head_spec, head_spec),
        scratch_shapes=[pltpu.VMEM((8, LANES), F32)],
        compiler_params=_cparams(("parallel", "arbitrary")),
        name="fox_prep",
    )(fox_c, f_c, f_bias, q_gain, k_gain)


def _fox_attn_kernel(q_ref, k_ref, v_ref, o_ref, s_ref, p_ref, m_ref, acc_ref, *, tq, rs):
    qi = pl.program_id(2)
    m_ref[...] = jnp.full(m_ref.shape, -jnp.inf, F32)
    acc_ref[...] = jnp.zeros_like(acc_ref)

    def block(j, diagonal):
        start = pl.multiple_of(j * tq, tq)
        s_ref[...] = _mm_nt(q_ref[...], k_ref[pl.ds(start, tq), :])
        for r in range(0, tq, rs):
            rows = slice(r, r + rs)
            s = s_ref[rows, :]
            if diagonal:
                s = jnp.where(_iota2((rs, tq), 1) <= _iota2((rs, tq), 0) + r, s, -jnp.inf)
            m_old = m_ref[rows, :]
            m_new = jnp.maximum(m_old, jnp.max(s, axis=-1, keepdims=True))
            p_ref[rows, :] = jnp.exp2(s - m_new).astype(BF16)
            m_ref[rows, :] = m_new
            acc_ref[rows, :] = acc_ref[rows, :] * jnp.exp2(m_old - m_new)
        acc_ref[...] += jnp.dot(p_ref[...], v_ref[pl.ds(start, tq), :],
                                preferred_element_type=F32)

    def body(j, carry):
        block(j, False)
        return carry

    lax.fori_loop(0, qi, body, 0)
    block(qi, True)
    acc = acc_ref[...]
    o_ref[...] = (acc[:, 0:HEAD_DIM] / acc[:, FOX_AUG:FOX_AUG + 1]).astype(o_ref.dtype)


def _fox_attn(q, k, v, tq=512, rs=32):
    B, H, S, W = q.shape
    return pl.pallas_call(
        functools.partial(_fox_attn_kernel, tq=tq, rs=rs),
        out_shape=jax.ShapeDtypeStruct((B, H, S, HEAD_DIM), BF16),
        grid=(B, H, S // tq),
        in_specs=[pl.BlockSpec((None, None, tq, W), lambda b, h, i: (b, h, i, 0)),
                  pl.BlockSpec((None, None, S, W), lambda b, h, i: (b, h, 0, 0)),
                  pl.BlockSpec((None, None, S, W), lambda b, h, i: (b, h, 0, 0))],
        out_specs=pl.BlockSpec((None, None, tq, HEAD_DIM), lambda b, h, i: (b, h, i, 0)),
        scratch_shapes=[pltpu.VMEM((tq, tq), F32), pltpu.VMEM((tq, tq), BF16),
                        pltpu.VMEM((tq, 1), F32), pltpu.VMEM((tq, W), F32)],
        compiler_params=_cparams(("parallel", "parallel", "arbitrary")),
        name="fox_attn",
    )(q, k, v)


def _head_blockdiag(x):
    C, W = x.shape
    n = W // HEAD_DIM
    tiled = jnp.concatenate([x] * n, axis=0)
    shift = int(math.log2(C))
    same = (_iota2((n * C, W), 0) >> shift) == (_iota2((n * C, W), 1) >> int(math.log2(HEAD_DIM)))
    return jnp.where(same, tiled, 0.0).astype(BF16)


def _bmm(a, bd):
    return jnp.dot(a.astype(BF16), bd, preferred_element_type=F32)


def _bmm_nt(a, bd):
    return lax.dot_general(a.astype(BF16), bd, (((1,), (1,)), ((), ())), preferred_element_type=F32)


def _unit_lower_inverse(a_list):
    C, W = a_list[0].shape
    ii = _iota2((C, W), 0)
    jj = _iota2((C, W), 1) & (C - 1)
    shift = int(math.log2(RWKV_SUB))
    same_blk = (ii >> shift) == (jj >> shift)
    eye = (ii == jj).astype(F32)
    d = [jnp.where(same_blk, a, 0.0) for a in a_list]
    low = [a - di for a, di in zip(a_list, d)]
    x = [eye + di for di in d]
    dp = d
    bd = [_head_blockdiag(t) for t in dp]
    for _ in range(shift - 1):
        dp = [_bmm(t, b) for t, b in zip(dp, bd)]
        bd = [_head_blockdiag(t) for t in dp]
        x = [xi + _bmm(xi, b) for xi, b in zip(x, bd)]
    n = [_bmm(xi, _head_blockdiag(lo)) for xi, lo in zip(x, low)]
    y = [eye + ni for ni in n]
    npow = n
    for _ in range(int(math.log2(C // RWKV_SUB)) - 1):
        bd = [_head_blockdiag(t) for t in npow]
        npow = [_bmm(t, b) for t, b in zip(npow, bd)]
        y = [yi + _bmm(yi, _head_blockdiag(t)) for yi, t in zip(y, npow)]
    return [_bmm(yi, _head_blockdiag(xi)) for yi, xi in zip(y, x)]


def _rwkv_kernel(c_ref, mu_ref, w0_ref, w2_ref, a0_ref, a2_ref, g2_ref, kk_ref, ka_ref, rk_ref,
                 gn_ref, o_ref, state_ref, carry_ref, y_ref, *, tb):
    C = RWKV_CHUNK
    W = MIX_W

    @pl.when(pl.program_id(1) == 0)
    def _():
        state_ref[...] = jnp.zeros_like(state_ref)
        carry_ref[...] = jnp.zeros_like(carry_ref)

    cur = c_ref[...]
    prev = jnp.where(_iota2(cur.shape, 0) == 0, carry_ref[0:1, :], pltpu.roll(cur, 1, 0))
    carry_ref[0:1, :] = cur[tb - 1:tb, :]
    mixed = cur + (prev - cur) * mu_ref[...]
    r = mixed[:, 0:W]
    k = mixed[:, W:2 * W]
    v = mixed[:, 2 * W:3 * W]
    xw = mixed[:, 3 * W:3 * W + DECAY_LORA]
    xa = mixed[:, 3 * W + DECAY_LORA:3 * W + DECAY_LORA + AAA_LORA]
    xg = mixed[:, 3 * W + DECAY_LORA + AAA_LORA:]

    log_w = -jnp.exp(_log_sigmoid(w0_ref[...] + _mm(jnp.tanh(xw), w2_ref[...])) - 0.5)
    a = _sigmoid(a0_ref[...] + _mm(xa, a2_ref[...]))
    g = _mm(_sigmoid(xg), g2_ref[...])
    kk = k * kk_ref[...]
    k2 = k * (1.0 + (a - 1.0) * ka_ref[...])

    heads = [slice(h * HEAD_DIM, (h + 1) * HEAD_DIM) for h in range(N_HEADS)]
    kk = jnp.concatenate(
        [kk[:, sl] / jnp.maximum(jnp.sqrt(jnp.sum(kk[:, sl] * kk[:, sl], axis=-1, keepdims=True)),
                                 1e-12) for sl in heads], axis=-1)
    a_vec = -kk
    b_vec = kk * a

    ii = _iota2((C, W), 0)
    jj = _iota2((C, W), 1) & (C - 1)
    strict = ii > jj
    incl = ii >= jj
    tril = (_iota2((C, C), 0) >= _iota2((C, C), 1)).astype(BF16)
    lane_head = _iota2((HEAD_DIM, W), 1) >> int(math.log2(HEAD_DIM))

    n_chunks = tb // C
    pre = []
    for c in range(n_chunks):
        rows = slice(c * C, (c + 1) * C)
        lw = log_w[rows]
        cum = _mask_mm_f32(tril, lw)
        e_pos = jnp.exp(cum)
        e_neg = jnp.exp(-cum)
        p_end = e_pos[C - 1:C, :]
        at = a_vec[rows] * jnp.exp(cum - lw)
        rt = r[rows] * e_pos
        bt = b_vec[rows] * e_neg
        kt = k2[rows] * e_neg
        ar = jnp.concatenate([at, rt], axis=0)
        gb = _bmm_nt(ar, _head_blockdiag(bt))
        gk = _bmm_nt(ar, _head_blockdiag(kt))
        vc = v[rows]
        pre.append(dict(
            ar=ar, p_end=p_end, vc=vc, bd_v=_head_blockdiag(vc),
            a_ab=jnp.where(strict, gb[0:C], 0.0), a_ak=jnp.where(strict, gk[0:C], 0.0),
            a_r=jnp.concatenate([jnp.where(incl, gb[C:], 0.0), jnp.where(incl, gk[C:], 0.0)], axis=1),
            bk_end=jnp.concatenate([bt * p_end, kt * p_end], axis=0)))
    t_inv = _unit_lower_inverse([p["a_ab"] for p in pre])
    akv = [_bmm(p["a_ak"], p["bd_v"]) for p in pre]

    s_all = state_ref[...]
    for c, p in enumerate(pre):
        ars = _bmm_nt(p["ar"], _head_blockdiag(s_all))
        u = _bmm(t_inv[c], _head_blockdiag(ars[0:C] + akv[c]))
        uv = jnp.concatenate([u, p["vc"]], axis=0)
        y_ref[c * C:(c + 1) * C, :] = ars[C:] + jnp.dot(
            p["a_r"].astype(BF16), jnp.concatenate([_head_blockdiag(u), p["bd_v"]], axis=0),
            preferred_element_type=F32)
        full = _mm_tn(uv, p["bk_end"])
        upd = full[0:HEAD_DIM]
        for h in range(1, N_HEADS):
            upd = jnp.where(lane_head == h, full[h * HEAD_DIM:(h + 1) * HEAD_DIM], upd)
        s_all = s_all * p["p_end"] + upd
    state_ref[...] = s_all

    y = y_ref[...]
    gn = gn_ref[...]
    rk = r * k2 * rk_ref[...]
    for sl in heads:
        yh = y[:, sl]
        mean = jnp.mean(yh, axis=-1, keepdims=True)
        yc = yh - mean
        var = jnp.mean(yc * yc, axis=-1, keepdims=True)
        yn = yc * lax.rsqrt(var + RWKV_GN_EPS) * gn[:, sl]
        bonus = jnp.sum(rk[:, sl], axis=-1, keepdims=True) * v[:, sl]
        o_ref[:, sl] = ((yn + bonus) * g[:, sl]).astype(o_ref.dtype)


def _rwkv(rwkv_c, mu, w0, w2, a0, a2, g2, k_k, k_a, r_k, gn, tb=256):
    B, S, W = rwkv_c.shape
    vec = lambda t: _resident((1, t.shape[-1]))
    mat = lambda t: _resident(t.shape)
    return pl.pallas_call(
        functools.partial(_rwkv_kernel, tb=tb),
        out_shape=jax.ShapeDtypeStruct((B, S, MIX_W), BF16),
        grid=(B, S // tb),
        in_specs=[pl.BlockSpec((None, tb, W), lambda b, i: (b, i, 0)),
                  vec(mu), vec(w0), mat(w2), vec(a0), mat(a2), mat(g2), vec(k_k), vec(k_a),
                  vec(r_k), vec(gn)],
        out_specs=pl.BlockSpec((None, tb, MIX_W), lambda b, i: (b, i, 0)),
        scratch_shapes=[pltpu.VMEM((HEAD_DIM, MIX_W), F32),
                        pltpu.VMEM((8, W), F32),
                        pltpu.VMEM((tb, MIX_W), F32)],
        compiler_params=_cparams(("parallel", "arbitrary")),
        name="rwkv7",
    )(rwkv_c, mu, w0, w2, a0, a2, g2, k_k, k_a, r_k, gn)


def _merge_kernel(x_ref, gate_ref, yret_ref, yfox_ref, yrwkv_ref, pret_ref, pfox_ref, prwkv_ref,
                  wout_ref, o_ref, *, d_model):
    D = d_model
    dot = lambda a, b: jnp.dot(a, b, preferred_element_type=F32)
    gates = gate_ref[...]
    fox = dot(yfox_ref[0], pfox_ref[0:HEAD_DIM, :])
    for h in range(1, N_HEADS):
        fox = fox + dot(yfox_ref[h], pfox_ref[h * HEAD_DIM:(h + 1) * HEAD_DIM, :])
    merged = (_sigmoid(gates[:, 0:D]) * dot(yret_ref[...], pret_ref[...])
              + _sigmoid(gates[:, D:2 * D]) * fox
              + _sigmoid(gates[:, 2 * D:3 * D]) * dot(yrwkv_ref[...], prwkv_ref[...]))
    o_ref[...] = x_ref[...] + dot(merged.astype(BF16), wout_ref[...])


def _merge(x, gates, y_ret, y_fox, y_rwkv, p_ret, p_fox, p_rwkv, w_out, tm=512):
    B, S, D = x.shape
    row = lambda w: pl.BlockSpec((None, tm, w), lambda b, i: (b, i, 0))
    return pl.pallas_call(
        functools.partial(_merge_kernel, d_model=D),
        out_shape=jax.ShapeDtypeStruct((B, S, D), F32),
        grid=(B, S // tm),
        in_specs=[row(D), row(3 * D), row(MIX_W),
                  pl.BlockSpec((None, N_HEADS, tm, HEAD_DIM), lambda b, i: (b, 0, i, 0)),
                  row(MIX_W), _resident(p_ret.shape), _resident(p_fox.shape),
                  _resident(p_rwkv.shape), _resident(w_out.shape)],
        out_specs=row(D),
        compiler_params=_cparams(("parallel", "parallel")),
        name="merge_out",
    )(x, gates, y_ret, y_fox, y_rwkv, p_ret, p_fox, p_rwkv, w_out)


def _ffn_kernel(x_ref, g_ref, wgu_ref, wd_ref, o_ref, *, d_ff, ck):
    x = x_ref[...]
    h = (x * lax.rsqrt(jnp.mean(x * x, axis=-1, keepdims=True) + NORM_EPS) * g_ref[...]).astype(BF16)
    acc = x
    for c in range(d_ff // ck):
        lo = c * ck
        gate = jnp.dot(h, wgu_ref[:, lo:lo + ck], preferred_element_type=F32)
        up = jnp.dot(h, wgu_ref[:, d_ff + lo:d_ff + lo + ck], preferred_element_type=F32)
        act = (gate * _sigmoid(gate) * up).astype(BF16)
        acc = acc + jnp.dot(act, wd_ref[lo:lo + ck, :], preferred_element_type=F32)
    o_ref[...] = acc


def _ffn(x, gain, w_gate_up, w_down, tm=512, ck=256):
    B, S, D = x.shape
    d_ff = w_down.shape[0]
    row = pl.BlockSpec((None, tm, D), lambda b, i: (b, i, 0))
    return pl.pallas_call(
        functools.partial(_ffn_kernel, d_ff=d_ff, ck=ck),
        out_shape=jax.ShapeDtypeStruct((B, S, D), F32),
        grid=(B, S // tm),
        in_specs=[row, _resident((1, D)), _resident(w_gate_up.shape), _resident(w_down.shape)],
        out_specs=row,
        compiler_params=_cparams(("parallel", "parallel")),
        name="ffn",
    )(x, gain, w_gate_up, w_down)


def kernel(x, mix_norm, w_in, ret_gn, fox_q_norm, fox_k_norm, fox_f_bias, rwkv_mu, rwkv_w0, rwkv_w2,
           rwkv_a0, rwkv_a2, rwkv_g2, rwkv_k_k, rwkv_k_a, rwkv_r_k, rwkv_gn, p_ret, p_fox, p_rwkv,
           w_out, ffn_norm, w_gate_up, w_down):
    B, S, D = x.shape
    depth = w_in.shape[0]
    cos, sin = _rope_tables(S)
    o0, o1, o2, o3 = (int(v) for v in IN_OFF)
    f_lo = o1 + FOXQKV_COLS
    f_hi = f_lo + N_HEADS
    row = lambda t: t.reshape(1, -1)
    for l in range(depth):
        w = w_in[l]
        w_cat = jnp.concatenate(
            [w[:, o0:f_lo], w[:, f_hi:], jnp.pad(w[:, f_lo:f_hi], ((0, 0), (0, LANES - N_HEADS)))],
            axis=1).astype(BF16)
        ret_c, fox_c, rwkv_c, gate_c, f_c = _in_proj(x, row(mix_norm[l]), w_cat)

        y_ret = _retention(ret_c, cos, sin, row(ret_gn[l]))

        f_bias = jnp.pad(fox_f_bias[l], (0, LANES - N_HEADS)).reshape(1, LANES)
        fq, fk, fv = _fox_prep(fox_c, f_c, f_bias, row(fox_q_norm[l]), row(fox_k_norm[l]))
        y_fox = _fox_attn(fq, fk, fv)

        y_rwkv = _rwkv(rwkv_c, row(rwkv_mu[l]), row(rwkv_w0[l]), rwkv_w2[l].astype(BF16),
                       row(rwkv_a0[l]), rwkv_a2[l].astype(BF16), rwkv_g2[l].astype(BF16),
                       row(rwkv_k_k[l]), row(rwkv_k_a[l]), row(rwkv_r_k[l]), row(rwkv_gn[l]))

        x = _merge(x, gate_c, y_ret, y_fox, y_rwkv, p_ret[l].astype(BF16), p_fox[l].astype(BF16),
                   p_rwkv[l].astype(BF16), w_out[l].astype(BF16))
        x = _ffn(x, row(ffn_norm[l]), w_gate_up[l].astype(BF16), w_down[l].astype(BF16))
    return x
```

```python
import functools
import math

import jax
import jax.numpy as jnp
import numpy as np
from jax import lax
from jax.experimental import pallas as pl
from jax.experimental.pallas import tpu as pltpu

F32 = jnp.float32
BF16 = jnp.bfloat16

HEAD_DIM = 64
N_HEADS = 4
MIX_W = N_HEADS * HEAD_DIM
RET_CHUNK = 128
ROPE_BASE = 10000.0
DECAY_LORA = 64
AAA_LORA = 64
GATE_LORA = 128
NORM_EPS = 1e-6
RET_GN_EPS = 1e-5
RWKV_GN_EPS = 64e-5
RWKV_CHUNK = 64
RWKV_SUB = 16
LANES = 128
VMEM_LIMIT = 52 * 1024 * 1024


def _cparams(sem):
    return pltpu.CompilerParams(dimension_semantics=sem, vmem_limit_bytes=VMEM_LIMIT)


def _resident(shape):
    nd = len(shape)
    return pl.BlockSpec(shape, lambda *_: (0,) * nd, pipeline_mode=pl.Buffered(1))


def _mm(a, b):
    return jnp.dot(a.astype(BF16), b.astype(BF16), preferred_element_type=F32)


def _mm_nt(a, b):
    return lax.dot_general(a.astype(BF16), b.astype(BF16), (((1,), (1,)), ((), ())),
                           preferred_element_type=F32)


def _mm_tn(a, b):
    return lax.dot_general(a.astype(BF16), b.astype(BF16), (((0,), (0,)), ((), ())),
                           preferred_element_type=F32)


def _mask_mm_f32(mask_bf16, x):
    h1 = x.astype(BF16)
    r1 = x - h1.astype(F32)
    h2 = r1.astype(BF16)
    h3 = (r1 - h2.astype(F32)).astype(BF16)
    dot = lambda h: jnp.dot(mask_bf16, h, preferred_element_type=F32)
    return dot(h1) + dot(h2) + dot(h3)


def _sigmoid(x):
    return 1.0 / (1.0 + jnp.exp(-x))


def _log_sigmoid(x):
    return jnp.minimum(x, 0.0) - jnp.log1p(jnp.exp(-jnp.abs(x)))


def _iota2(shape, dim):
    return lax.broadcasted_iota(jnp.int32, shape, dim)


RET_COLS = 4 * MIX_W
FOXQKV_COLS = 3 * MIX_W
RWKV_COLS = 3 * MIX_W + DECAY_LORA + AAA_LORA + GATE_LORA
IN_OFF = np.cumsum([0, RET_COLS, FOXQKV_COLS, RWKV_COLS])


def _w_in_relayout_kernel(w_ref, o_ref, *, f_lo, n_f):
    ncol = w_ref.shape[1]
    rest = ncol - (f_lo + n_f)
    o_ref[:, 0:f_lo] = w_ref[:, 0:f_lo].astype(BF16)
    o_ref[:, f_lo:f_lo + rest] = w_ref[:, f_lo + n_f:ncol].astype(BF16)
    win = w_ref[:, f_lo:f_lo + LANES]
    o_ref[:, f_lo + rest:] = jnp.where(_iota2(win.shape, 1) < n_f, win, 0.0).astype(BF16)


def _w_in_relayout(w, f_lo, n_f, tr=128):
    D, ncol = w.shape
    out_cols = ncol - n_f + LANES
    return pl.pallas_call(
        functools.partial(_w_in_relayout_kernel, f_lo=f_lo, n_f=n_f),
        out_shape=jax.ShapeDtypeStruct((D, out_cols), BF16),
        grid=(D // tr,),
        in_specs=[pl.BlockSpec((tr, ncol), lambda i: (i, 0))],
        out_specs=pl.BlockSpec((tr, out_cols), lambda i: (i, 0)),
        compiler_params=_cparams(("parallel",)),
        name="w_in_relayout",
    )(w)


def _in_proj_kernel(x_ref, g_ref, w_ref, ret_ref, fox_ref, rwkv_ref, gate_ref, f_ref, *, d_model):
    x = x_ref[...]
    h = x * lax.rsqrt(jnp.mean(x * x, axis=-1, keepdims=True) + NORM_EPS) * g_ref[...]
    hb = h.astype(BF16)
    o0, o1, o2, o3 = (int(v) for v in IN_OFF)
    o4 = o3 + 3 * d_model
    dot = lambda lo, hi: jnp.dot(hb, w_ref[:, lo:hi], preferred_element_type=F32)
    ret_ref[...] = dot(o0, o1)
    fox_ref[...] = dot(o1, o2)
    rwkv_ref[...] = dot(o2, o3)
    gate_ref[...] = dot(o3, o4)
    f_ref[...] = dot(o4, o4 + LANES)


def _in_proj(x, gain, w_cat, tm=256):
    B, S, D = x.shape
    ncol = w_cat.shape[1]
    row = lambda w: pl.BlockSpec((None, tm, w), lambda b, i: (b, i, 0))
    widths = (RET_COLS, FOXQKV_COLS, RWKV_COLS, 3 * D, LANES)
    return pl.pallas_call(
        functools.partial(_in_proj_kernel, d_model=D),
        out_shape=tuple(jax.ShapeDtypeStruct((B, S, w), F32) for w in widths),
        grid=(B, S // tm),
        in_specs=[row(D), _resident((1, D)), _resident((D, ncol))],
        out_specs=tuple(row(w) for w in widths),
        compiler_params=_cparams(("parallel", "parallel")),
        name="in_proj",
    )(x, gain, w_cat)


def _retention_kernel(c_ref, cos_ref, sin_ref, gn_ref, o_ref, state_ref):
    C = RET_CHUNK

    @pl.when(pl.program_id(1) == 0)
    def _():
        state_ref[...] = jnp.zeros_like(state_ref)

    blk = c_ref[...]
    cos = cos_ref[...]
    sin = sin_ref[...]
    first_half = (_iota2((C, MIX_W), 1) & (HEAD_DIM - 1)) < HEAD_DIM // 2

    def rope(t):
        partner = jnp.where(first_half, pltpu.roll(t, MIX_W - HEAD_DIM // 2, 1),
                            pltpu.roll(t, HEAD_DIM // 2, 1))
        return t * cos + partner * sin

    q = rope(blk[:, 0:MIX_W])
    k = rope(blk[:, MIX_W:2 * MIX_W]) * (HEAD_DIM ** -0.5)
    v = blk[:, 2 * MIX_W:3 * MIX_W]
    g = blk[:, 3 * MIX_W:4 * MIX_W]
    gate = g * _sigmoid(g)
    gn = gn_ref[...]

    ii = _iota2((C, C), 0)
    jj = _iota2((C, C), 1)
    dist = jnp.maximum(ii - jj, 0).astype(F32)
    causal = ii >= jj
    row = _iota2((C, HEAD_DIM), 0).astype(F32)

    for h in range(N_HEADS):
        sl = slice(h * HEAD_DIM, (h + 1) * HEAD_DIM)
        log_gamma = math.log1p(-2.0 ** (-5.0 - h))
        qh, kh, vh = q[:, sl], k[:, sl], v[:, sl]
        dmat = jnp.where(causal, jnp.exp(log_gamma * dist), 0.0)
        scores = _mm_nt(qh, kh) * dmat
        inner = _mm(scores, vh)
        r_prev = state_ref[h]
        cross = _mm(qh * jnp.exp(log_gamma * (row + 1.0)), r_prev)
        k_dec = jnp.exp(log_gamma * (C - 1.0 - row))
        state_ref[h] = r_prev * math.exp(log_gamma * C) + _mm_tn(kh * k_dec, vh)
        y = inner + cross
        mu = jnp.mean(y, axis=-1, keepdims=True)
        yc = y - mu
        var = jnp.mean(yc * yc, axis=-1, keepdims=True)
        yn = yc * lax.rsqrt(var + RET_GN_EPS)
        o_ref[:, sl] = (gate[:, sl] * (yn * gn[:, sl])).astype(o_ref.dtype)


def _retention(ret_c, cos, sin, gn):
    B, S, W = ret_c.shape
    C = RET_CHUNK
    return pl.pallas_call(
        _retention_kernel,
        out_shape=jax.ShapeDtypeStruct((B, S, MIX_W), BF16),
        grid=(B, S // C),
        in_specs=[pl.BlockSpec((None, C, W), lambda b, i: (b, i, 0)),
                  pl.BlockSpec((C, MIX_W), lambda b, i: (i, 0)),
                  pl.BlockSpec((C, MIX_W), lambda b, i: (i, 0)),
                  _resident((1, MIX_W))],
        out_specs=pl.BlockSpec((None, C, MIX_W), lambda b, i: (b, i, 0)),
        scratch_shapes=[pltpu.VMEM((N_HEADS, HEAD_DIM, HEAD_DIM), F32)],
        compiler_params=_cparams(("parallel", "arbitrary")),
        name="retention",
    )(ret_c, cos, sin, gn)


def _rope_tables(S):
    half = HEAD_DIM // 2
    inv_freq = ROPE_BASE ** (-jnp.arange(half, dtype=F32) / half)
    ang = jnp.arange(S, dtype=F32)[:, None] * inv_freq[None, :]
    cos, sin = jnp.cos(ang), jnp.sin(ang)
    cos_h = jnp.concatenate([cos, cos], axis=-1)
    sin_h = jnp.concatenate([-sin, sin], axis=-1)
    return jnp.tile(cos_h, (1, N_HEADS)), jnp.tile(sin_h, (1, N_HEADS))


LOG2E = math.log2(math.e)
FOX_AUG = HEAD_DIM
FOX_TQ = 512
FOX_TK = 256


def _fox_prep_kernel(c_ref, f_ref, bias_ref, qg_ref, kg_ref, q_ref, k_ref, vt_ref, carry_ref, *, ts,
                     tk):
    @pl.when(pl.program_id(1) == 0)
    def _():
        carry_ref[...] = jnp.zeros_like(carry_ref)

    log_f = _log_sigmoid(f_ref[...] + bias_ref[...])
    tril = (_iota2((ts, ts), 0) >= _iota2((ts, ts), 1)).astype(BF16)
    c = _mask_mm_f32(tril, log_f) + carry_ref[0:1, :]
    carry_ref[0:1, :] = c[ts - 1:ts, :]
    c = c * LOG2E

    blk = c_ref[...]
    lane = _iota2((ts, LANES), 1)
    zeros = jnp.zeros((ts, LANES - HEAD_DIM), F32)
    pad = lambda t: jnp.concatenate([t, zeros], axis=1)

    def rms(t, gain):
        return t * lax.rsqrt(jnp.mean(t * t, axis=-1, keepdims=True) + NORM_EPS) * gain

    def at_lanes(lo, vals, rest):
        out = rest
        for i, val in reversed(list(enumerate(vals))):
            out = jnp.where(lane == lo + i, val, out)
        return out

    for h in range(N_HEADS):
        lo = h * HEAD_DIM
        ch = jnp.broadcast_to(c[:, h:h + 1], (ts, LANES))
        c1 = ch.astype(BF16).astype(F32)
        r1 = ch - c1
        c2 = r1.astype(BF16).astype(F32)
        c3 = (r1 - c2).astype(BF16).astype(F32)
        qh = rms(blk[:, lo:lo + HEAD_DIM], qg_ref[...]) * (LOG2E * HEAD_DIM ** -0.5)
        kh = rms(blk[:, MIX_W + lo:MIX_W + lo + HEAD_DIM], kg_ref[...])
        vh = blk[:, 2 * MIX_W + lo:2 * MIX_W + lo + HEAD_DIM]
        one = jnp.ones((ts, LANES), F32)
        q_aug = at_lanes(FOX_AUG, [c1, c2, c3, one, one, one], 0.0)
        k_aug = at_lanes(FOX_AUG, [one, one, one, -c1, -c2, -c3], 0.0)
        v_aug = at_lanes(FOX_AUG, [one], 0.0)
        q_ref[h] = jnp.where(lane < HEAD_DIM, pad(qh), q_aug).astype(BF16)
        k_ref[h] = jnp.where(lane < HEAD_DIM, pad(kh), k_aug).astype(BF16)
        v_t = jnp.where(lane < HEAD_DIM, pad(vh), v_aug).T
        for kb in range(ts // tk):
            vt_ref[h, kb] = v_t[:, kb * tk:(kb + 1) * tk].astype(BF16)


def _fox_prep(fox_c, f_c, f_bias, q_gain, k_gain, ts=512):
    B, S, W = fox_c.shape
    tk = FOX_TK
    head_spec = pl.BlockSpec((None, N_HEADS, ts, LANES), lambda b, i: (b, 0, i, 0))
    shp = jax.ShapeDtypeStruct((B, N_HEADS, S, LANES), BF16)
    return pl.pallas_call(
        functools.partial(_fox_prep_kernel, ts=ts, tk=tk),
        out_shape=(shp, shp, jax.ShapeDtypeStruct((B, N_HEADS, S // tk, LANES, tk), BF16)),
        grid=(B, S // ts),
        in_specs=[pl.BlockSpec((None, ts, W), lambda b, i: (b, i, 0)),
                  pl.BlockSpec((None, ts, LANES), lambda b, i: (b, i, 0)),
                  _resident((1, LANES)), _resident((1, HEAD_DIM)), _resident((1, HEAD_DIM))],
        out_specs=(head_spec, head_spec,
                   pl.BlockSpec((None, N_HEADS, ts // tk, LANES, tk), lambda b, i: (b, 0, i, 0, 0))),
        scratch_shapes=[pltpu.VMEM((8, LANES), F32)],
        compiler_params=_cparams(("parallel", "arbitrary")),
        name="fox_prep",
    )(fox_c, f_c, f_bias, q_gain, k_gain)


def _fox_attn_kernel(q_ref, k_ref, vt_ref, o_ref, s0_ref, s1_ref, p0_ref, p1_ref, m_ref, alpha_ref,
                     acc_ref, *, tq, tk):
    qi = pl.program_id(2)
    m_ref[...] = jnp.full(m_ref.shape, -jnp.inf, F32)
    acc_ref[...] = jnp.zeros_like(acc_ref)
    p1_ref[...] = jnp.zeros_like(p1_ref)

    def qk(j, s_ref):
        start = pl.multiple_of(j * tk, tk)
        s_ref[...] = _mm_nt(k_ref[pl.ds(start, tk), :], q_ref[...])

    def pv(j, p_ref):
        return jnp.dot(vt_ref[j], p_ref[...], preferred_element_type=F32)

    def softmax(s_ref, p_ref, key_offset):
        for c in range(0, tq, LANES):
            cols = slice(c, c + LANES)
            s = s_ref[:, cols]
            if key_offset is not None and key_offset + tk - 1 > c:
                keep = _iota2((tk, LANES), 0) + key_offset <= _iota2((tk, LANES), 1) + c
                s = jnp.where(keep, s, -jnp.inf)
            m_old = m_ref[:, cols]
            m_new = jnp.maximum(m_old, jnp.max(s, axis=0, keepdims=True))
            p_ref[:, cols] = jnp.exp2(s - m_new).astype(BF16)
            m_ref[:, cols] = m_new
            alpha_ref[:, cols] = jnp.exp2(m_old - m_new)

    def advance(pv_prev):
        acc_ref[...] = (acc_ref[...] + pv_prev) * alpha_ref[...]

    qk(0, s0_ref)

    def pair(i, carry):
        j = 2 * i
        qk(j + 1, s1_ref)
        pv_prev = pv(jnp.maximum(j - 1, 0), p1_ref)
        softmax(s0_ref, p0_ref, None)
        advance(pv_prev)
        qk(j + 2, s0_ref)
        pv_prev = pv(j, p0_ref)
        softmax(s1_ref, p1_ref, None)
        advance(pv_prev)
        return carry

    lax.fori_loop(0, qi, pair, 0)

    j = 2 * qi
    qk(j + 1, s1_ref)
    pv_prev = pv(jnp.maximum(j - 1, 0), p1_ref)
    softmax(s0_ref, p0_ref, 0)
    advance(pv_prev)
    pv_prev = pv(j, p0_ref)
    softmax(s1_ref, p1_ref, tk)
    advance(pv_prev)
    acc = acc_ref[...] + pv(j + 1, p1_ref)
    o_ref[...] = (acc[0:HEAD_DIM] / acc[FOX_AUG:FOX_AUG + 1]).astype(o_ref.dtype)


def _fox_attn(q, k, vt):
    B, H, S, W = q.shape
    tq, tk = FOX_TQ, FOX_TK
    assert tq == 2 * tk and S % tq == 0
    return pl.pallas_call(
        functools.partial(_fox_attn_kernel, tq=tq, tk=tk),
        out_shape=jax.ShapeDtypeStruct((B, H, HEAD_DIM, S), BF16),
        grid=(B, H, S // tq),
        in_specs=[pl.BlockSpec((None, None, tq, W), lambda b, h, i: (b, h, i, 0)),
                  pl.BlockSpec((None, None, S, W), lambda b, h, i: (b, h, 0, 0)),
                  pl.BlockSpec((None, None, S // tk, W, tk), lambda b, h, i: (b, h, 0, 0, 0))],
        out_specs=pl.BlockSpec((None, None, HEAD_DIM, tq), lambda b, h, i: (b, h, 0, i)),
        scratch_shapes=[pltpu.VMEM((tk, tq), F32), pltpu.VMEM((tk, tq), F32),
                        pltpu.VMEM((tk, tq), BF16), pltpu.VMEM((tk, tq), BF16),
                        pltpu.VMEM((1, tq), F32), pltpu.VMEM((1, tq), F32),
                        pltpu.VMEM((W, tq), F32)],
        compiler_params=_cparams(("parallel", "parallel", "arbitrary")),
        name="fox_attn",
    )(q, k, vt)


def _head_blockdiag(x):
    C, W = x.shape
    n = W // HEAD_DIM
    tiled = jnp.concatenate([x] * n, axis=0)
    shift = int(math.log2(C))
    same = (_iota2((n * C, W), 0) >> shift) == (_iota2((n * C, W), 1) >> int(math.log2(HEAD_DIM)))
    return jnp.where(same, tiled, 0.0).astype(BF16)


def _bmm(a, bd):
    return jnp.dot(a.astype(BF16), bd, preferred_element_type=F32)


def _bmm_nt(a, bd):
    return lax.dot_general(a.astype(BF16), bd, (((1,), (1,)), ((), ())), preferred_element_type=F32)


def _unit_lower_inverse(a_list):
    C, W = a_list[0].shape
    ii = _iota2((C, W), 0)
    jj = _iota2((C, W), 1) & (C - 1)
    shift = int(math.log2(RWKV_SUB))
    same_blk = (ii >> shift) == (jj >> shift)
    eye = (ii == jj).astype(F32)
    d = [jnp.where(same_blk, a, 0.0) for a in a_list]
    low = [a - di for a, di in zip(a_list, d)]
    x = [eye + di for di in d]
    dp = d
    bd = [_head_blockdiag(t) for t in dp]
    for _ in range(shift - 1):
        dp = [_bmm(t, b) for t, b in zip(dp, bd)]
        bd = [_head_blockdiag(t) for t in dp]
        x = [xi + _bmm(xi, b) for xi, b in zip(x, bd)]
    n = [_bmm(xi, _head_blockdiag(lo)) for xi, lo in zip(x, low)]
    y = [eye + ni for ni in n]
    npow = n
    for _ in range(int(math.log2(C // RWKV_SUB)) - 1):
        bd = [_head_blockdiag(t) for t in npow]
        npow = [_bmm(t, b) for t, b in zip(npow, bd)]
        y = [yi + _bmm(yi, _head_blockdiag(t)) for yi, t in zip(y, npow)]
    return [_bmm(yi, _head_blockdiag(xi)) for yi, xi in zip(y, x)]


def _rwkv_kernel(c_ref, mu_ref, w0_ref, w2_ref, a0_ref, a2_ref, g2_ref, kk_ref, ka_ref, rk_ref,
                 gn_ref, o_ref, state_ref, carry_ref, y_ref, *, tb):
    C = RWKV_CHUNK
    W = MIX_W

    @pl.when(pl.program_id(1) == 0)
    def _():
        state_ref[...] = jnp.zeros_like(state_ref)
        carry_ref[...] = jnp.zeros_like(carry_ref)

    cur = c_ref[...]
    prev = jnp.where(_iota2(cur.shape, 0) == 0, carry_ref[0:1, :], pltpu.roll(cur, 1, 0))
    carry_ref[0:1, :] = cur[tb - 1:tb, :]
    mixed = cur + (prev - cur) * mu_ref[...]
    r = mixed[:, 0:W]
    k = mixed[:, W:2 * W]
    v = mixed[:, 2 * W:3 * W]
    xw = mixed[:, 3 * W:3 * W + DECAY_LORA]
    xa = mixed[:, 3 * W + DECAY_LORA:3 * W + DECAY_LORA + AAA_LORA]
    xg = mixed[:, 3 * W + DECAY_LORA + AAA_LORA:]

    log_w = -jnp.exp(_log_sigmoid(w0_ref[...] + _mm(jnp.tanh(xw), w2_ref[...])) - 0.5)
    a = _sigmoid(a0_ref[...] + _mm(xa, a2_ref[...]))
    g = _mm(_sigmoid(xg), g2_ref[...])
    kk = k * kk_ref[...]
    k2 = k * (1.0 + (a - 1.0) * ka_ref[...])

    heads = [slice(h * HEAD_DIM, (h + 1) * HEAD_DIM) for h in range(N_HEADS)]
    kk = jnp.concatenate(
        [kk[:, sl] / jnp.maximum(jnp.sqrt(jnp.sum(kk[:, sl] * kk[:, sl], axis=-1, keepdims=True)),
                                 1e-12) for sl in heads], axis=-1)
    a_vec = -kk
    b_vec = kk * a

    ii = _iota2((C, W), 0)
    jj = _iota2((C, W), 1) & (C - 1)
    strict = ii > jj
    incl = ii >= jj
    tril = (_iota2((C, C), 0) >= _iota2((C, C), 1)).astype(BF16)
    lane_head = _iota2((HEAD_DIM, W), 1) >> int(math.log2(HEAD_DIM))

    n_chunks = tb // C
    pre = []
    for c in range(n_chunks):
        rows = slice(c * C, (c + 1) * C)
        lw = log_w[rows]
        cum = _mask_mm_f32(tril, lw)
        e_pos = jnp.exp(cum)
        e_neg = jnp.exp(-cum)
        p_end = e_pos[C - 1:C, :]
        at = a_vec[rows] * jnp.exp(cum - lw)
        rt = r[rows] * e_pos
        bt = b_vec[rows] * e_neg
        kt = k2[rows] * e_neg
        ar = jnp.concatenate([at, rt], axis=0)
        gb = _bmm_nt(ar, _head_blockdiag(bt))
        gk = _bmm_nt(ar, _head_blockdiag(kt))
        vc = v[rows]
        pre.append(dict(
            ar=ar, p_end=p_end, vc=vc, bd_v=_head_blockdiag(vc),
            a_ab=jnp.where(strict, gb[0:C], 0.0), a_ak=jnp.where(strict, gk[0:C], 0.0),
            a_r=jnp.concatenate([jnp.where(incl, gb[C:], 0.0), jnp.where(incl, gk[C:], 0.0)], axis=1),
            bk_end=jnp.concatenate([bt * p_end, kt * p_end], axis=0)))
    t_inv = _unit_lower_inverse([p["a_ab"] for p in pre])
    akv = [_bmm(p["a_ak"], p["bd_v"]) for p in pre]

    s_all = state_ref[...]
    for c, p in enumerate(pre):
        ars = _bmm_nt(p["ar"], _head_blockdiag(s_all))
        u = _bmm(t_inv[c], _head_blockdiag(ars[0:C] + akv[c]))
        uv = jnp.concatenate([u, p["vc"]], axis=0)
        y_ref[c * C:(c + 1) * C, :] = ars[C:] + jnp.dot(
            p["a_r"].astype(BF16), jnp.concatenate([_head_blockdiag(u), p["bd_v"]], axis=0),
            preferred_element_type=F32)
        full = _mm_tn(uv, p["bk_end"])
        upd = full[0:HEAD_DIM]
        for h in range(1, N_HEADS):
            upd = jnp.where(lane_head == h, full[h * HEAD_DIM:(h + 1) * HEAD_DIM], upd)
        s_all = s_all * p["p_end"] + upd
    state_ref[...] = s_all

    y = y_ref[...]
    gn = gn_ref[...]
    rk = r * k2 * rk_ref[...]
    for sl in heads:
        yh = y[:, sl]
        mean = jnp.mean(yh, axis=-1, keepdims=True)
        yc = yh - mean
        var = jnp.mean(yc * yc, axis=-1, keepdims=True)
        yn = yc * lax.rsqrt(var + RWKV_GN_EPS) * gn[:, sl]
        bonus = jnp.sum(rk[:, sl], axis=-1, keepdims=True) * v[:, sl]
        o_ref[:, sl] = ((yn + bonus) * g[:, sl]).astype(o_ref.dtype)


def _rwkv(rwkv_c, mu, w0, w2, a0, a2, g2, k_k, k_a, r_k, gn, tb=256):
    B, S, W = rwkv_c.shape
    vec = lambda t: _resident((1, t.shape[-1]))
    mat = lambda t: _resident(t.shape)
    return pl.pallas_call(
        functools.partial(_rwkv_kernel, tb=tb),
        out_shape=jax.ShapeDtypeStruct((B, S, MIX_W), BF16),
        grid=(B, S // tb),
        in_specs=[pl.BlockSpec((None, tb, W), lambda b, i: (b, i, 0)),
                  vec(mu), vec(w0), mat(w2), vec(a0), mat(a2), mat(g2), vec(k_k), vec(k_a),
                  vec(r_k), vec(gn)],
        out_specs=pl.BlockSpec((None, tb, MIX_W), lambda b, i: (b, i, 0)),
        scratch_shapes=[pltpu.VMEM((HEAD_DIM, MIX_W), F32),
                        pltpu.VMEM((8, W), F32),
                        pltpu.VMEM((tb, MIX_W), F32)],
        compiler_params=_cparams(("parallel", "arbitrary")),
        name="rwkv7",
    )(rwkv_c, mu, w0, w2, a0, a2, g2, k_k, k_a, r_k, gn)


def _merge_kernel(x_ref, gate_ref, yret_ref, yfox_ref, yrwkv_ref, pret_ref, pfox_ref, prwkv_ref,
                  wout_ref, o_ref, *, d_model):
    D = d_model
    dot = lambda a, b: jnp.dot(a, b, preferred_element_type=F32)
    gates = gate_ref[...]
    fox = lax.dot_general(yfox_ref[...], pfox_ref[...], (((0,), (0,)), ((), ())),
                          preferred_element_type=F32)
    merged = (_sigmoid(gates[:, 0:D]) * dot(yret_ref[...], pret_ref[...])
              + _sigmoid(gates[:, D:2 * D]) * fox
              + _sigmoid(gates[:, 2 * D:3 * D]) * dot(yrwkv_ref[...], prwkv_ref[...]))
    o_ref[...] = x_ref[...] + dot(merged.astype(BF16), wout_ref[...])


def _merge(x, gates, y_ret, y_fox, y_rwkv, p_ret, p_fox, p_rwkv, w_out, tm=512):
    B, S, D = x.shape
    row = lambda w: pl.BlockSpec((None, tm, w), lambda b, i: (b, i, 0))
    return pl.pallas_call(
        functools.partial(_merge_kernel, d_model=D),
        out_shape=jax.ShapeDtypeStruct((B, S, D), F32),
        grid=(B, S // tm),
        in_specs=[row(D), row(3 * D), row(MIX_W),
                  pl.BlockSpec((None, MIX_W, tm), lambda b, i: (b, 0, i)),
                  row(MIX_W), _resident(p_ret.shape), _resident(p_fox.shape),
                  _resident(p_rwkv.shape), _resident(w_out.shape)],
        out_specs=row(D),
        compiler_params=_cparams(("parallel", "parallel")),
        name="merge_out",
    )(x, gates, y_ret, y_fox, y_rwkv, p_ret, p_fox, p_rwkv, w_out)


def _ffn_kernel(x_ref, g_ref, wgu_ref, wd_ref, o_ref, *, d_ff, ck):
    x = x_ref[...]
    h = (x * lax.rsqrt(jnp.mean(x * x, axis=-1, keepdims=True) + NORM_EPS) * g_ref[...]).astype(BF16)
    acc = x
    for c in range(d_ff // ck):
        lo = c * ck
        gate = jnp.dot(h, wgu_ref[:, lo:lo + ck], preferred_element_type=F32)
        up = jnp.dot(h, wgu_ref[:, d_ff + lo:d_ff + lo + ck], preferred_element_type=F32)
        act = (gate * _sigmoid(gate) * up).astype(BF16)
        acc = acc + jnp.dot(act, wd_ref[lo:lo + ck, :], preferred_element_type=F32)
    o_ref[...] = acc


def _ffn(x, gain, w_gate_up, w_down, tm=512, ck=256):
    B, S, D = x.shape
    d_ff = w_down.shape[0]
    row = pl.BlockSpec((None, tm, D), lambda b, i: (b, i, 0))
    return pl.pallas_call(
        functools.partial(_ffn_kernel, d_ff=d_ff, ck=ck),
        out_shape=jax.ShapeDtypeStruct((B, S, D), F32),
        grid=(B, S // tm),
        in_specs=[row, _resident((1, D)), _resident(w_gate_up.shape), _resident(w_down.shape)],
        out_specs=row,
        compiler_params=_cparams(("parallel", "parallel")),
        name="ffn",
    )(x, gain, w_gate_up, w_down)


def kernel(x, mix_norm, w_in, ret_gn, fox_q_norm, fox_k_norm, fox_f_bias, rwkv_mu, rwkv_w0, rwkv_w2,
           rwkv_a0, rwkv_a2, rwkv_g2, rwkv_k_k, rwkv_k_a, rwkv_r_k, rwkv_gn, p_ret, p_fox, p_rwkv,
           w_out, ffn_norm, w_gate_up, w_down):
    B, S, D = x.shape
    depth = w_in.shape[0]
    cos, sin = _rope_tables(S)
    o0, o1, o2, o3 = (int(v) for v in IN_OFF)
    f_lo = o1 + FOXQKV_COLS
    row = lambda t: t.reshape(1, -1)
    for l in range(depth):
        w_cat = _w_in_relayout(w_in[l], f_lo, N_HEADS)
        ret_c, fox_c, rwkv_c, gate_c, f_c = _in_proj(x, row(mix_norm[l]), w_cat)

        y_ret = _retention(ret_c, cos, sin, row(ret_gn[l]))

        f_bias = jnp.pad(fox_f_bias[l], (0, LANES - N_HEADS)).reshape(1, LANES)
        fq, fk, fv = _fox_prep(fox_c, f_c, f_bias, row(fox_q_norm[l]), row(fox_k_norm[l]))
        y_fox = _fox_attn(fq, fk, fv).reshape(B, MIX_W, S)

        y_rwkv = _rwkv(rwkv_c, row(rwkv_mu[l]), row(rwkv_w0[l]), rwkv_w2[l].astype(BF16),
                       row(rwkv_a0[l]), rwkv_a2[l].astype(BF16), rwkv_g2[l].astype(BF16),
                       row(rwkv_k_k[l]), row(rwkv_k_a[l]), row(rwkv_r_k[l]), row(rwkv_gn[l]))

        x = _merge(x, gate_c, y_ret, y_fox, y_rwkv, p_ret[l].astype(BF16), p_fox[l].astype(BF16),
                   p_rwkv[l].astype(BF16), w_out[l].astype(BF16))
        x = _ffn(x, row(ffn_norm[l]), w_gate_up[l].astype(BF16), w_down[l].astype(BF16))
    return x
```

```python
import functools
import math

import jax
import jax.numpy as jnp
import numpy as np
from jax import lax
from jax.experimental import pallas as pl
from jax.experimental.pallas import tpu as pltpu

F32 = jnp.float32
BF16 = jnp.bfloat16

HEAD_DIM = 64
N_HEADS = 4
MIX_W = N_HEADS * HEAD_DIM
RET_CHUNK = 128
ROPE_BASE = 10000.0
DECAY_LORA = 64
AAA_LORA = 64
GATE_LORA = 128
NORM_EPS = 1e-6
RET_GN_EPS = 1e-5
RWKV_GN_EPS = 64e-5
RWKV_CHUNK = 64
RWKV_SUB = 16
LANES = 128
VMEM_LIMIT = 52 * 1024 * 1024


def _cparams(sem):
    return pltpu.CompilerParams(dimension_semantics=sem, vmem_limit_bytes=VMEM_LIMIT)


def _resident(shape):
    nd = len(shape)
    return pl.BlockSpec(shape, lambda *_: (0,) * nd, pipeline_mode=pl.Buffered(1))


def _mm(a, b):
    return jnp.dot(a.astype(BF16), b.astype(BF16), preferred_element_type=F32)


def _mm_nt(a, b):
    return lax.dot_general(a.astype(BF16), b.astype(BF16), (((1,), (1,)), ((), ())),
                           preferred_element_type=F32)


def _mm_tn(a, b):
    return lax.dot_general(a.astype(BF16), b.astype(BF16), (((0,), (0,)), ((), ())),
                           preferred_element_type=F32)


def _mask_mm_f32(mask_bf16, x):
    h1 = x.astype(BF16)
    r1 = x - h1.astype(F32)
    h2 = r1.astype(BF16)
    h3 = (r1 - h2.astype(F32)).astype(BF16)
    dot = lambda h: jnp.dot(mask_bf16, h, preferred_element_type=F32)
    return dot(h1) + dot(h2) + dot(h3)


def _sigmoid(x):
    return 1.0 / (1.0 + jnp.exp(-x))


def _log_sigmoid(x):
    return jnp.minimum(x, 0.0) - jnp.log1p(jnp.exp(-jnp.abs(x)))


def _iota2(shape, dim):
    return lax.broadcasted_iota(jnp.int32, shape, dim)


def _head_mask():
    same = (_iota2((MIX_W, MIX_W), 0) >> 6) == (_iota2((MIX_W, MIX_W), 1) >> 6)
    return jnp.where(same, 1.0, 0.0).astype(BF16)


def _head_blockdiag(x, head_mask):
    xb = x.astype(BF16)
    return jnp.concatenate([xb] * N_HEADS, axis=0) * head_mask


def _head_sum(x, head_mask):
    hi = x.astype(BF16)
    lo = (x - hi.astype(F32)).astype(BF16)
    dot = lambda t: jnp.dot(t, head_mask, preferred_element_type=F32)
    return dot(hi) + dot(lo)


def _bmm(a, bd):
    return jnp.dot(a.astype(BF16), bd, preferred_element_type=F32)


def _bmm_nt(a, bd):
    return lax.dot_general(a.astype(BF16), bd, (((1,), (1,)), ((), ())), preferred_element_type=F32)


RET_COLS = 4 * MIX_W
FOXQKV_COLS = 3 * MIX_W
RWKV_COLS = 3 * MIX_W + DECAY_LORA + AAA_LORA + GATE_LORA
IN_OFF = np.cumsum([0, RET_COLS, FOXQKV_COLS, RWKV_COLS])


def _w_in_relayout_kernel(w_ref, o_ref, *, f_lo, n_f):
    ncol = w_ref.shape[1]
    rest = ncol - (f_lo + n_f)
    o_ref[:, 0:f_lo] = w_ref[:, 0:f_lo].astype(BF16)
    o_ref[:, f_lo:f_lo + rest] = w_ref[:, f_lo + n_f:ncol].astype(BF16)
    win = w_ref[:, f_lo:f_lo + LANES]
    o_ref[:, f_lo + rest:] = jnp.where(_iota2(win.shape, 1) < n_f, win, 0.0).astype(BF16)


def _w_in_relayout(w, f_lo, n_f, tr=128):
    D, ncol = w.shape
    out_cols = ncol - n_f + LANES
    return pl.pallas_call(
        functools.partial(_w_in_relayout_kernel, f_lo=f_lo, n_f=n_f),
        out_shape=jax.ShapeDtypeStruct((D, out_cols), BF16),
        grid=(D // tr,),
        in_specs=[pl.BlockSpec((tr, ncol), lambda i: (i, 0))],
        out_specs=pl.BlockSpec((tr, out_cols), lambda i: (i, 0)),
        compiler_params=_cparams(("parallel",)),
        name="w_in_relayout",
    )(w)


def _in_proj_kernel(x_ref, g_ref, w_ref, ret_ref, fox_ref, rwkv_ref, gate_ref, f_ref, *, d_model):
    x = x_ref[...]
    h = x * lax.rsqrt(jnp.mean(x * x, axis=-1, keepdims=True) + NORM_EPS) * g_ref[...]
    hb = h.astype(BF16)
    o0, o1, o2, o3 = (int(v) for v in IN_OFF)
    o4 = o3 + 3 * d_model
    dot = lambda lo, hi: jnp.dot(hb, w_ref[:, lo:hi], preferred_element_type=F32)
    ret_ref[...] = dot(o0, o1)
    fox_ref[...] = dot(o1, o2)
    rwkv_ref[...] = dot(o2, o3)
    gate_ref[...] = dot(o3, o4)
    f_ref[...] = dot(o4, o4 + LANES)


def _in_proj(x, gain, w_cat, tm=256):
    B, S, D = x.shape
    ncol = w_cat.shape[1]
    row = lambda w: pl.BlockSpec((None, tm, w), lambda b, i: (b, i, 0))
    widths = (RET_COLS, FOXQKV_COLS, RWKV_COLS, 3 * D, LANES)
    return pl.pallas_call(
        functools.partial(_in_proj_kernel, d_model=D),
        out_shape=tuple(jax.ShapeDtypeStruct((B, S, w), F32) for w in widths),
        grid=(B, S // tm),
        in_specs=[row(D), _resident((1, D)), _resident((D, ncol))],
        out_specs=tuple(row(w) for w in widths),
        compiler_params=_cparams(("parallel", "parallel")),
        name="in_proj",
    )(x, gain, w_cat)


def _retention_kernel(c_ref, cos_ref, sin_ref, gn_ref, dmat_ref, qdec_ref, kdec_ref, cdec_ref, o_ref,
                      state_ref, *, tb):
    C = RET_CHUNK
    n_batch = c_ref.shape[0]

    @pl.when(pl.program_id(0) == 0)
    def _():
        state_ref[...] = jnp.zeros_like(state_ref)

    head_mask = _head_mask()
    first_half = (_iota2((tb, MIX_W), 1) & (HEAD_DIM - 1)) < HEAD_DIM // 2
    kv_mask = jnp.where((_iota2((N_HEADS * C, MIX_W), 0) >> int(math.log2(C)))
                        == (_iota2((N_HEADS * C, MIX_W), 1) >> int(math.log2(HEAD_DIM))),
                        1.0, 0.0).astype(BF16)
    lane_head = _iota2((HEAD_DIM, MIX_W), 1) >> int(math.log2(HEAD_DIM))
    cos = cos_ref[...]
    sin = sin_ref[...]

    def rope(t):
        partner = jnp.where(first_half, pltpu.roll(t, MIX_W - HEAD_DIM // 2, 1),
                            pltpu.roll(t, HEAD_DIM // 2, 1))
        return t * cos + partner * sin

    def stack_heads(t):
        return jnp.concatenate([t.astype(BF16)] * N_HEADS, axis=0) * kv_mask

    for b in range(n_batch):
        blk = c_ref[b]
        q = rope(blk[:, 0:MIX_W])
        k = rope(blk[:, MIX_W:2 * MIX_W]) * (HEAD_DIM ** -0.5)
        v = blk[:, 2 * MIX_W:3 * MIX_W]
        g = blk[:, 3 * MIX_W:4 * MIX_W]
        state = state_ref[b]
        ys = []
        for c in range(tb // C):
            rows = slice(c * C, (c + 1) * C)
            qc, kc, vc = q[rows], k[rows], v[rows]
            scores = _bmm_nt(qc, stack_heads(kc)) * dmat_ref[...]
            inner = _bmm(scores, stack_heads(vc))
            cross = _bmm(qc * qdec_ref[...], _head_blockdiag(state, head_mask))
            ys.append(inner + cross)
            full = _mm_tn(kc * kdec_ref[...], vc)
            upd = full[0:HEAD_DIM]
            for h in range(1, N_HEADS):
                upd = jnp.where(lane_head == h, full[h * HEAD_DIM:(h + 1) * HEAD_DIM], upd)
            state = state * cdec_ref[...] + upd
        state_ref[b] = state
        y = jnp.concatenate(ys, axis=0)
        yc = y - _head_sum(y, head_mask) * (1.0 / HEAD_DIM)
        var = _head_sum(yc * yc, head_mask) * (1.0 / HEAD_DIM)
        yn = yc * lax.rsqrt(var + RET_GN_EPS)
        o_ref[b] = (g * _sigmoid(g) * (yn * gn_ref[...])).astype(o_ref.dtype)


def _retention_tables():
    C = RET_CHUNK
    log_gamma = jnp.log1p(-jnp.exp2(-5.0 - jnp.arange(N_HEADS, dtype=F32)))
    idx = jnp.arange(C, dtype=F32)
    dist = idx[:, None] - idx[None, :]
    dmat = jnp.where(dist >= 0, jnp.exp(log_gamma[:, None, None] * jnp.maximum(dist, 0.0)), 0.0)
    dmat = dmat.transpose(1, 0, 2).reshape(C, N_HEADS * C)
    lanes = lambda t: jnp.repeat(t, HEAD_DIM, axis=-1)
    q_dec = lanes(jnp.exp(log_gamma[None, :] * (idx[:, None] + 1.0)))
    k_dec = lanes(jnp.exp(log_gamma[None, :] * (C - 1.0 - idx[:, None])))
    c_dec = lanes(jnp.exp(log_gamma * C)[None, :])
    return dmat, q_dec, k_dec, c_dec


def _retention(ret_c, cos, sin, gn, tb=256):
    B, S, W = ret_c.shape
    tables = _retention_tables()
    return pl.pallas_call(
        functools.partial(_retention_kernel, tb=tb),
        out_shape=jax.ShapeDtypeStruct((B, S, MIX_W), BF16),
        grid=(S // tb,),
        in_specs=[pl.BlockSpec((B, tb, W), lambda i: (0, i, 0)),
                  pl.BlockSpec((tb, MIX_W), lambda i: (i, 0)),
                  pl.BlockSpec((tb, MIX_W), lambda i: (i, 0)),
                  _resident((1, MIX_W))] + [_resident(t.shape) for t in tables],
        out_specs=pl.BlockSpec((B, tb, MIX_W), lambda i: (0, i, 0)),
        scratch_shapes=[pltpu.VMEM((B, HEAD_DIM, MIX_W), F32)],
        compiler_params=_cparams(("arbitrary",)),
        name="retention",
    )(ret_c, cos, sin, gn, *tables)


def _rope_tables(S):
    half = HEAD_DIM // 2
    inv_freq = ROPE_BASE ** (-jnp.arange(half, dtype=F32) / half)
    ang = jnp.arange(S, dtype=F32)[:, None] * inv_freq[None, :]
    cos, sin = jnp.cos(ang), jnp.sin(ang)
    cos_h = jnp.concatenate([cos, cos], axis=-1)
    sin_h = jnp.concatenate([-sin, sin], axis=-1)
    return jnp.tile(cos_h, (1, N_HEADS)), jnp.tile(sin_h, (1, N_HEADS))


LOG2E = math.log2(math.e)
FOX_AUG = HEAD_DIM
FOX_TQ = 512
FOX_TK = 256


def _fox_prep_kernel(c_ref, f_ref, bias_ref, qg_ref, kg_ref, q_ref, k_ref, vt_ref, carry_ref, *, ts,
                     tk):
    @pl.when(pl.program_id(1) == 0)
    def _():
        carry_ref[...] = jnp.zeros_like(carry_ref)

    log_f = _log_sigmoid(f_ref[...] + bias_ref[...])
    tril = (_iota2((ts, ts), 0) >= _iota2((ts, ts), 1)).astype(BF16)
    c = _mask_mm_f32(tril, log_f) + carry_ref[0:1, :]
    carry_ref[0:1, :] = c[ts - 1:ts, :]
    c = c * LOG2E

    blk = c_ref[...]
    lane = _iota2((ts, LANES), 1)
    zeros = jnp.zeros((ts, LANES - HEAD_DIM), F32)
    pad = lambda t: jnp.concatenate([t, zeros], axis=1)

    def rms(t, gain):
        return t * lax.rsqrt(jnp.mean(t * t, axis=-1, keepdims=True) + NORM_EPS) * gain

    def at_lanes(lo, vals, rest):
        out = rest
        for i, val in reversed(list(enumerate(vals))):
            out = jnp.where(lane == lo + i, val, out)
        return out

    for h in range(N_HEADS):
        lo = h * HEAD_DIM
        ch = jnp.broadcast_to(c[:, h:h + 1], (ts, LANES))
        c1 = ch.astype(BF16).astype(F32)
        r1 = ch - c1
        c2 = r1.astype(BF16).astype(F32)
        c3 = (r1 - c2).astype(BF16).astype(F32)
        qh = rms(blk[:, lo:lo + HEAD_DIM], qg_ref[...]) * (LOG2E * HEAD_DIM ** -0.5)
        kh = rms(blk[:, MIX_W + lo:MIX_W + lo + HEAD_DIM], kg_ref[...])
        vh = blk[:, 2 * MIX_W + lo:2 * MIX_W + lo + HEAD_DIM]
        one = jnp.ones((ts, LANES), F32)
        q_aug = at_lanes(FOX_AUG, [c1, c2, c3, one, one, one], 0.0)
        k_aug = at_lanes(FOX_AUG, [one, one, one, -c1, -c2, -c3], 0.0)
        v_aug = at_lanes(FOX_AUG, [one], 0.0)
        q_ref[h] = jnp.where(lane < HEAD_DIM, pad(qh), q_aug).astype(BF16)
        k_ref[h] = jnp.where(lane < HEAD_DIM, pad(kh), k_aug).astype(BF16)
        v_t = jnp.where(lane < HEAD_DIM, pad(vh), v_aug).T
        for kb in range(ts // tk):
            vt_ref[h, kb] = v_t[:, kb * tk:(kb + 1) * tk].astype(BF16)


def _fox_prep(fox_c, f_c, f_bias, q_gain, k_gain, ts=512):
    B, S, W = fox_c.shape
    tk = FOX_TK
    head_spec = pl.BlockSpec((None, N_HEADS, ts, LANES), lambda b, i: (b, 0, i, 0))
    shp = jax.ShapeDtypeStruct((B, N_HEADS, S, LANES), BF16)
    return pl.pallas_call(
        functools.partial(_fox_prep_kernel, ts=ts, tk=tk),
        out_shape=(shp, shp, jax.ShapeDtypeStruct((B, N_HEADS, S // tk, LANES, tk), BF16)),
        grid=(B, S // ts),
        in_specs=[pl.BlockSpec((None, ts, W), lambda b, i: (b, i, 0)),
                  pl.BlockSpec((None, ts, LANES), lambda b, i: (b, i, 0)),
                  _resident((1, LANES)), _resident((1, HEAD_DIM)), _resident((1, HEAD_DIM))],
        out_specs=(head_spec, head_spec,
                   pl.BlockSpec((None, N_HEADS, ts // tk, LANES, tk), lambda b, i: (b, 0, i, 0, 0))),
        scratch_shapes=[pltpu.VMEM((8, LANES), F32)],
        compiler_params=_cparams(("parallel", "arbitrary")),
        name="fox_prep",
    )(fox_c, f_c, f_bias, q_gain, k_gain)


def _fox_attn_kernel(q_ref, k_ref, vt_ref, o_ref, s0_ref, s1_ref, p0_ref, p1_ref, m_ref, alpha_ref,
                     acc_ref, *, tq, tk):
    qi = pl.program_id(2)
    m_ref[...] = jnp.full(m_ref.shape, -jnp.inf, F32)
    acc_ref[...] = jnp.zeros_like(acc_ref)
    p1_ref[...] = jnp.zeros_like(p1_ref)

    def qk(j, s_ref):
        start = pl.multiple_of(j * tk, tk)
        s_ref[...] = _mm_nt(k_ref[pl.ds(start, tk), :], q_ref[...])

    def pv(j, p_ref):
        return jnp.dot(vt_ref[j], p_ref[...], preferred_element_type=F32)

    def softmax(s_ref, p_ref, key_offset):
        for c in range(0, tq, LANES):
            cols = slice(c, c + LANES)
            s = s_ref[:, cols]
            if key_offset is not None and key_offset + tk - 1 > c:
                keep = _iota2((tk, LANES), 0) + key_offset <= _iota2((tk, LANES), 1) + c
                s = jnp.where(keep, s, -jnp.inf)
            m_old = m_ref[:, cols]
            m_new = jnp.maximum(m_old, jnp.max(s, axis=0, keepdims=True))
            p_ref[:, cols] = jnp.exp2(s - m_new).astype(BF16)
            m_ref[:, cols] = m_new
            alpha_ref[:, cols] = jnp.exp2(m_old - m_new)

    def advance(pv_prev):
        acc_ref[...] = (acc_ref[...] + pv_prev) * alpha_ref[...]

    qk(0, s0_ref)

    def pair(i, carry):
        j = 2 * i
        qk(j + 1, s1_ref)
        pv_prev = pv(jnp.maximum(j - 1, 0), p1_ref)
        softmax(s0_ref, p0_ref, None)
        advance(pv_prev)
        qk(j + 2, s0_ref)
        pv_prev = pv(j, p0_ref)
        softmax(s1_ref, p1_ref, None)
        advance(pv_prev)
        return carry

    lax.fori_loop(0, qi, pair, 0)

    j = 2 * qi
    qk(j + 1, s1_ref)
    pv_prev = pv(jnp.maximum(j - 1, 0), p1_ref)
    softmax(s0_ref, p0_ref, 0)
    advance(pv_prev)
    pv_prev = pv(j, p0_ref)
    softmax(s1_ref, p1_ref, tk)
    advance(pv_prev)
    acc = acc_ref[...] + pv(j + 1, p1_ref)
    o_ref[...] = (acc[0:HEAD_DIM] / acc[FOX_AUG:FOX_AUG + 1]).astype(o_ref.dtype)


def _fox_attn(q, k, vt):
    B, H, S, W = q.shape
    tq, tk = FOX_TQ, FOX_TK
    assert tq == 2 * tk and S % tq == 0
    return pl.pallas_call(
        functools.partial(_fox_attn_kernel, tq=tq, tk=tk),
        out_shape=jax.ShapeDtypeStruct((B, H, HEAD_DIM, S), BF16),
        grid=(B, H, S // tq),
        in_specs=[pl.BlockSpec((None, None, tq, W), lambda b, h, i: (b, h, i, 0)),
                  pl.BlockSpec((None, None, S, W), lambda b, h, i: (b, h, 0, 0)),
                  pl.BlockSpec((None, None, S // tk, W, tk), lambda b, h, i: (b, h, 0, 0, 0))],
        out_specs=pl.BlockSpec((None, None, HEAD_DIM, tq), lambda b, h, i: (b, h, 0, i)),
        scratch_shapes=[pltpu.VMEM((tk, tq), F32), pltpu.VMEM((tk, tq), F32),
                        pltpu.VMEM((tk, tq), BF16), pltpu.VMEM((tk, tq), BF16),
                        pltpu.VMEM((1, tq), F32), pltpu.VMEM((1, tq), F32),
                        pltpu.VMEM((W, tq), F32)],
        compiler_params=_cparams(("parallel", "parallel", "arbitrary")),
        name="fox_attn",
    )(q, k, vt)


def _unit_lower_inverse(a_list, head_mask):
    C, W = a_list[0].shape
    bdiag = lambda t: _head_blockdiag(t, head_mask)
    ii = _iota2((C, W), 0)
    jj = _iota2((C, W), 1) & (C - 1)
    shift = int(math.log2(RWKV_SUB))
    same_blk = (ii >> shift) == (jj >> shift)
    eye = (ii == jj).astype(F32)
    d = [jnp.where(same_blk, a, 0.0) for a in a_list]
    low = [a - di for a, di in zip(a_list, d)]
    x = [eye + di for di in d]
    dp = d
    bd = [bdiag(t) for t in dp]
    for _ in range(shift - 1):
        dp = [_bmm(t, b) for t, b in zip(dp, bd)]
        bd = [bdiag(t) for t in dp]
        x = [xi + _bmm(xi, b) for xi, b in zip(x, bd)]
    n = [_bmm(xi, bdiag(lo)) for xi, lo in zip(x, low)]
    y = [eye + ni for ni in n]
    npow = n
    for _ in range(int(math.log2(C // RWKV_SUB)) - 1):
        bd = [bdiag(t) for t in npow]
        npow = [_bmm(t, b) for t, b in zip(npow, bd)]
        y = [yi + _bmm(yi, bdiag(t)) for yi, t in zip(y, npow)]
    return [_bmm(yi, bdiag(xi)) for yi, xi in zip(y, x)]


def _rwkv_kernel(c_ref, mu_ref, w0_ref, w2_ref, a0_ref, a2_ref, g2_ref, kk_ref, ka_ref, rk_ref,
                 gn_ref, o_ref, state_ref, carry_ref, y_ref, *, tb):
    C = RWKV_CHUNK
    W = MIX_W
    n_batch = c_ref.shape[0]
    n_chunks = tb // C

    @pl.when(pl.program_id(0) == 0)
    def _():
        state_ref[...] = jnp.zeros_like(state_ref)
        carry_ref[...] = jnp.zeros_like(carry_ref)

    head_mask = _head_mask()
    bdiag = lambda t: _head_blockdiag(t, head_mask)
    ii = _iota2((C, W), 0)
    jj = _iota2((C, W), 1) & (C - 1)
    strict = ii > jj
    incl = ii >= jj
    tril = (_iota2((C, C), 0) >= _iota2((C, C), 1)).astype(BF16)
    lane_head = _iota2((HEAD_DIM, W), 1) >> int(math.log2(HEAD_DIM))

    post = []
    pre = []
    for b in range(n_batch):
        cur = c_ref[b]
        prev = jnp.where(_iota2(cur.shape, 0) == 0, carry_ref[b, 0:1, :], pltpu.roll(cur, 1, 0))
        carry_ref[b, 0:1, :] = cur[tb - 1:tb, :]
        mixed = cur + (prev - cur) * mu_ref[...]
        r = mixed[:, 0:W]
        k = mixed[:, W:2 * W]
        v = mixed[:, 2 * W:3 * W]
        xw = mixed[:, 3 * W:3 * W + DECAY_LORA]
        xa = mixed[:, 3 * W + DECAY_LORA:3 * W + DECAY_LORA + AAA_LORA]
        xg = mixed[:, 3 * W + DECAY_LORA + AAA_LORA:]

        log_w = -jnp.exp(_log_sigmoid(w0_ref[...] + _mm(jnp.tanh(xw), w2_ref[...])) - 0.5)
        a = _sigmoid(a0_ref[...] + _mm(xa, a2_ref[...]))
        g = _mm(_sigmoid(xg), g2_ref[...])
        kk = k * kk_ref[...]
        k2 = k * (1.0 + (a - 1.0) * ka_ref[...])
        kk = kk / jnp.maximum(jnp.sqrt(_head_sum(kk * kk, head_mask)), 1e-12)
        a_vec = -kk
        b_vec = kk * a
        post.append((g, _head_sum(r * k2 * rk_ref[...], head_mask) * v))

        for c in range(n_chunks):
            rows = slice(c * C, (c + 1) * C)
            lw = log_w[rows]
            cum = _mask_mm_f32(tril, lw)
            e_pos = jnp.exp(cum)
            e_neg = jnp.exp(-cum)
            p_end = e_pos[C - 1:C, :]
            at = a_vec[rows] * jnp.exp(cum - lw)
            rt = r[rows] * e_pos
            bt = b_vec[rows] * e_neg
            kt = k2[rows] * e_neg
            ar = jnp.concatenate([at, rt], axis=0)
            gb = _bmm_nt(ar, bdiag(bt))
            gk = _bmm_nt(ar, bdiag(kt))
            vc = v[rows]
            pre.append(dict(
                b=b, c=c, ar=ar, p_end=p_end, vc=vc, bd_v=bdiag(vc),
                a_ab=jnp.where(strict, gb[0:C], 0.0), a_ak=jnp.where(strict, gk[0:C], 0.0),
                a_r=jnp.concatenate([jnp.where(incl, gb[C:], 0.0), jnp.where(incl, gk[C:], 0.0)],
                                    axis=1),
                bk_end=jnp.concatenate([bt * p_end, kt * p_end], axis=0)))
    t_inv = _unit_lower_inverse([p["a_ab"] for p in pre], head_mask)
    akv = [_bmm(p["a_ak"], p["bd_v"]) for p in pre]

    s_all = [state_ref[b] for b in range(n_batch)]
    for c in range(n_chunks):
        for b in range(n_batch):
            i = b * n_chunks + c
            p = pre[i]
            ars = _bmm_nt(p["ar"], bdiag(s_all[b]))
            u = _bmm(t_inv[i], bdiag(ars[0:C] + akv[i]))
            uv = jnp.concatenate([u, p["vc"]], axis=0)
            y_ref[b, c * C:(c + 1) * C, :] = ars[C:] + jnp.dot(
                p["a_r"].astype(BF16), jnp.concatenate([bdiag(u), p["bd_v"]], axis=0),
                preferred_element_type=F32)
            full = _mm_tn(uv, p["bk_end"])
            upd = full[0:HEAD_DIM]
            for h in range(1, N_HEADS):
                upd = jnp.where(lane_head == h, full[h * HEAD_DIM:(h + 1) * HEAD_DIM], upd)
            s_all[b] = s_all[b] * p["p_end"] + upd
    for b in range(n_batch):
        state_ref[b] = s_all[b]

    for b in range(n_batch):
        g, bonus = post[b]
        y = y_ref[b]
        yc = y - _head_sum(y, head_mask) * (1.0 / HEAD_DIM)
        var = _head_sum(yc * yc, head_mask) * (1.0 / HEAD_DIM)
        yn = yc * lax.rsqrt(var + RWKV_GN_EPS) * gn_ref[...]
        o_ref[b] = ((yn + bonus) * g).astype(o_ref.dtype)


def _rwkv(rwkv_c, mu, w0, w2, a0, a2, g2, k_k, k_a, r_k, gn, tb=256):
    B, S, W = rwkv_c.shape
    vec = lambda t: _resident((1, t.shape[-1]))
    mat = lambda t: _resident(t.shape)
    return pl.pallas_call(
        functools.partial(_rwkv_kernel, tb=tb),
        out_shape=jax.ShapeDtypeStruct((B, S, MIX_W), BF16),
        grid=(S // tb,),
        in_specs=[pl.BlockSpec((B, tb, W), lambda i: (0, i, 0)),
                  vec(mu), vec(w0), mat(w2), vec(a0), mat(a2), mat(g2), vec(k_k), vec(k_a),
                  vec(r_k), vec(gn)],
        out_specs=pl.BlockSpec((B, tb, MIX_W), lambda i: (0, i, 0)),
        scratch_shapes=[pltpu.VMEM((B, HEAD_DIM, MIX_W), F32),
                        pltpu.VMEM((B, 8, W), F32),
                        pltpu.VMEM((B, tb, MIX_W), F32)],
        compiler_params=_cparams(("arbitrary",)),
        name="rwkv7",
    )(rwkv_c, mu, w0, w2, a0, a2, g2, k_k, k_a, r_k, gn)


def _merge_kernel(x_ref, gate_ref, yret_ref, yfox_ref, yrwkv_ref, pret_ref, pfox_ref, prwkv_ref,
                  wout_ref, o_ref, *, d_model):
    D = d_model
    dot = lambda a, b: jnp.dot(a, b, preferred_element_type=F32)
    gates = gate_ref[...]
    fox = lax.dot_general(yfox_ref[...], pfox_ref[...], (((0,), (0,)), ((), ())),
                          preferred_element_type=F32)
    merged = (_sigmoid(gates[:, 0:D]) * dot(yret_ref[...], pret_ref[...])
              + _sigmoid(gates[:, D:2 * D]) * fox
              + _sigmoid(gates[:, 2 * D:3 * D]) * dot(yrwkv_ref[...], prwkv_ref[...]))
    o_ref[...] = x_ref[...] + dot(merged.astype(BF16), wout_ref[...])


def _merge(x, gates, y_ret, y_fox, y_rwkv, p_ret, p_fox, p_rwkv, w_out, tm=512):
    B, S, D = x.shape
    row = lambda w: pl.BlockSpec((None, tm, w), lambda b, i: (b, i, 0))
    return pl.pallas_call(
        functools.partial(_merge_kernel, d_model=D),
        out_shape=jax.ShapeDtypeStruct((B, S, D), F32),
        grid=(B, S // tm),
        in_specs=[row(D), row(3 * D), row(MIX_W),
                  pl.BlockSpec((None, MIX_W, tm), lambda b, i: (b, 0, i)),
                  row(MIX_W), _resident(p_ret.shape), _resident(p_fox.shape),
                  _resident(p_rwkv.shape), _resident(w_out.shape)],
        out_specs=row(D),
        compiler_params=_cparams(("parallel", "parallel")),
        name="merge_out",
    )(x, gates, y_ret, y_fox, y_rwkv, p_ret, p_fox, p_rwkv, w_out)


def _ffn_kernel(x_ref, g_ref, wgu_ref, wd_ref, o_ref, *, d_ff, ck):
    x = x_ref[...]
    h = (x * lax.rsqrt(jnp.mean(x * x, axis=-1, keepdims=True) + NORM_EPS) * g_ref[...]).astype(BF16)
    acc = x
    for c in range(d_ff // ck):
        lo = c * ck
        gate = jnp.dot(h, wgu_ref[:, lo:lo + ck], preferred_element_type=F32)
        up = jnp.dot(h, wgu_ref[:, d_ff + lo:d_ff + lo + ck], preferred_element_type=F32)
        act = (gate * _sigmoid(gate) * up).astype(BF16)
        acc = acc + jnp.dot(act, wd_ref[lo:lo + ck, :], preferred_element_type=F32)
    o_ref[...] = acc


def _ffn(x, gain, w_gate_up, w_down, tm=512, ck=256):
    B, S, D = x.shape
    d_ff = w_down.shape[0]
    row = pl.BlockSpec((None, tm, D), lambda b, i: (b, i, 0))
    return pl.pallas_call(
        functools.partial(_ffn_kernel, d_ff=d_ff, ck=ck),
        out_shape=jax.ShapeDtypeStruct((B, S, D), F32),
        grid=(B, S // tm),
        in_specs=[row, _resident((1, D)), _resident(w_gate_up.shape), _resident(w_down.shape)],
        out_specs=row,
        compiler_params=_cparams(("parallel", "parallel")),
        name="ffn",
    )(x, gain, w_gate_up, w_down)


def kernel(x, mix_norm, w_in, ret_gn, fox_q_norm, fox_k_norm, fox_f_bias, rwkv_mu, rwkv_w0, rwkv_w2,
           rwkv_a0, rwkv_a2, rwkv_g2, rwkv_k_k, rwkv_k_a, rwkv_r_k, rwkv_gn, p_ret, p_fox, p_rwkv,
           w_out, ffn_norm, w_gate_up, w_down):
    B, S, D = x.shape
    depth = w_in.shape[0]
    cos, sin = _rope_tables(S)
    o0, o1, o2, o3 = (int(v) for v in IN_OFF)
    f_lo = o1 + FOXQKV_COLS
    row = lambda t: t.reshape(1, -1)
    for l in range(depth):
        w_cat = _w_in_relayout(w_in[l], f_lo, N_HEADS)
        ret_c, fox_c, rwkv_c, gate_c, f_c = _in_proj(x, row(mix_norm[l]), w_cat)

        y_ret = _retention(ret_c, cos, sin, row(ret_gn[l]))

        f_bias = jnp.pad(fox_f_bias[l], (0, LANES - N_HEADS)).reshape(1, LANES)
        fq, fk, fv = _fox_prep(fox_c, f_c, f_bias, row(fox_q_norm[l]), row(fox_k_norm[l]))
        y_fox = _fox_attn(fq, fk, fv).reshape(B, MIX_W, S)

        y_rwkv = _rwkv(rwkv_c, row(rwkv_mu[l]), row(rwkv_w0[l]), rwkv_w2[l].astype(BF16),
                       row(rwkv_a0[l]), rwkv_a2[l].astype(BF16), rwkv_g2[l].astype(BF16),
                       row(rwkv_k_k[l]), row(rwkv_k_a[l]), row(rwkv_r_k[l]), row(rwkv_gn[l]))

        x = _merge(x, gate_c, y_ret, y_fox, y_rwkv, p_ret[l].astype(BF16), p_fox[l].astype(BF16),
                   p_rwkv[l].astype(BF16), w_out[l].astype(BF16))
        x = _ffn(x, row(ffn_norm[l]), w_gate_up[l].astype(BF16), w_down[l].astype(BF16))
    return x
```

```python
import functools
import math

import jax
import jax.numpy as jnp
import numpy as np
from jax import lax
from jax.experimental import pallas as pl
from jax.experimental.pallas import tpu as pltpu

F32 = jnp.float32
BF16 = jnp.bfloat16

HEAD_DIM = 64
N_HEADS = 4
MIX_W = N_HEADS * HEAD_DIM
RET_CHUNK = 128
ROPE_BASE = 10000.0
DECAY_LORA = 64
AAA_LORA = 64
GATE_LORA = 128
NORM_EPS = 1e-6
RET_GN_EPS = 1e-5
RWKV_GN_EPS = 64e-5
RWKV_CHUNK = 64
RWKV_SUB = 16
LANES = 128
VMEM_LIMIT = 52 * 1024 * 1024


def _cparams(sem):
    return pltpu.CompilerParams(dimension_semantics=sem, vmem_limit_bytes=VMEM_LIMIT)


def _resident(shape):
    nd = len(shape)
    return pl.BlockSpec(shape, lambda *_: (0,) * nd, pipeline_mode=pl.Buffered(1))


def _mm(a, b):
    return jnp.dot(a.astype(BF16), b.astype(BF16), preferred_element_type=F32)


def _mm_nt(a, b):
    return lax.dot_general(a.astype(BF16), b.astype(BF16), (((1,), (1,)), ((), ())),
                           preferred_element_type=F32)


def _mm_tn(a, b):
    return lax.dot_general(a.astype(BF16), b.astype(BF16), (((0,), (0,)), ((), ())),
                           preferred_element_type=F32)


def _mask_mm_f32(mask_bf16, x):
    h1 = x.astype(BF16)
    r1 = x - h1.astype(F32)
    h2 = r1.astype(BF16)
    h3 = (r1 - h2.astype(F32)).astype(BF16)
    dot = lambda h: jnp.dot(mask_bf16, h, preferred_element_type=F32)
    return dot(h1) + dot(h2) + dot(h3)


def _sigmoid(x):
    return 1.0 / (1.0 + jnp.exp(-x))


def _log_sigmoid(x):
    return jnp.minimum(x, 0.0) - jnp.log1p(jnp.exp(-jnp.abs(x)))


def _iota2(shape, dim):
    return lax.broadcasted_iota(jnp.int32, shape, dim)


def _head_mask():
    same = (_iota2((MIX_W, MIX_W), 0) >> 6) == (_iota2((MIX_W, MIX_W), 1) >> 6)
    return jnp.where(same, 1.0, 0.0).astype(BF16)


def _head_blockdiag(x, head_mask):
    xb = x.astype(BF16)
    return jnp.concatenate([xb] * N_HEADS, axis=0) * head_mask


def _head_sum(x, head_mask):
    hi = x.astype(BF16)
    lo = (x - hi.astype(F32)).astype(BF16)
    dot = lambda t: jnp.dot(t, head_mask, preferred_element_type=F32)
    return dot(hi) + dot(lo)


def _bmm(a, bd):
    return jnp.dot(a.astype(BF16), bd, preferred_element_type=F32)


def _bmm_nt(a, bd):
    return lax.dot_general(a.astype(BF16), bd, (((1,), (1,)), ((), ())), preferred_element_type=F32)


RET_COLS = 4 * MIX_W
FOXQKV_COLS = 3 * MIX_W
RWKV_COLS = 3 * MIX_W + DECAY_LORA + AAA_LORA + GATE_LORA
IN_OFF = np.cumsum([0, RET_COLS, FOXQKV_COLS, RWKV_COLS])


def _w_in_relayout_kernel(w_ref, o_ref, *, f_lo, n_f):
    ncol = w_ref.shape[1]
    rest = ncol - (f_lo + n_f)
    o_ref[:, 0:f_lo] = w_ref[:, 0:f_lo].astype(BF16)
    o_ref[:, f_lo:f_lo + rest] = w_ref[:, f_lo + n_f:ncol].astype(BF16)
    win = w_ref[:, f_lo:f_lo + LANES]
    o_ref[:, f_lo + rest:] = jnp.where(_iota2(win.shape, 1) < n_f, win, 0.0).astype(BF16)


def _w_in_relayout(w, f_lo, n_f, tr=128):
    D, ncol = w.shape
    out_cols = ncol - n_f + LANES
    return pl.pallas_call(
        functools.partial(_w_in_relayout_kernel, f_lo=f_lo, n_f=n_f),
        out_shape=jax.ShapeDtypeStruct((D, out_cols), BF16),
        grid=(D // tr,),
        in_specs=[pl.BlockSpec((tr, ncol), lambda i: (i, 0))],
        out_specs=pl.BlockSpec((tr, out_cols), lambda i: (i, 0)),
        compiler_params=_cparams(("parallel",)),
        name="w_in_relayout",
    )(w)


def _in_proj_kernel(x_ref, g_ref, w_ref, ret_ref, fox_ref, rwkv_ref, gate_ref, f_ref, *, d_model):
    x = x_ref[...]
    h = x * lax.rsqrt(jnp.mean(x * x, axis=-1, keepdims=True) + NORM_EPS) * g_ref[...]
    hb = h.astype(BF16)
    o0, o1, o2, o3 = (int(v) for v in IN_OFF)
    o4 = o3 + 3 * d_model
    dot = lambda lo, hi: jnp.dot(hb, w_ref[:, lo:hi], preferred_element_type=F32)
    ret_ref[...] = dot(o0, o1)
    fox_ref[...] = dot(o1, o2)
    rwkv_ref[...] = dot(o2, o3)
    gate_ref[...] = dot(o3, o4)
    f_ref[...] = dot(o4, o4 + LANES)


def _in_proj(x, gain, w_cat, tm=256):
    B, S, D = x.shape
    ncol = w_cat.shape[1]
    row = lambda w: pl.BlockSpec((None, tm, w), lambda b, i: (b, i, 0))
    widths = (RET_COLS, FOXQKV_COLS, RWKV_COLS, 3 * D, LANES)
    return pl.pallas_call(
        functools.partial(_in_proj_kernel, d_model=D),
        out_shape=tuple(jax.ShapeDtypeStruct((B, S, w), F32) for w in widths),
        grid=(B, S // tm),
        in_specs=[row(D), _resident((1, D)), _resident((D, ncol))],
        out_specs=tuple(row(w) for w in widths),
        compiler_params=_cparams(("parallel", "parallel")),
        name="in_proj",
    )(x, gain, w_cat)


def _retention_kernel(c_ref, cos_ref, sin_ref, gn_ref, dmat_ref, qdec_ref, kdec_ref, cdec_ref, o_ref,
                      state_ref, *, tb):
    C = RET_CHUNK
    n_batch = c_ref.shape[0]

    @pl.when(pl.program_id(0) == 0)
    def _():
        state_ref[...] = jnp.zeros_like(state_ref)

    head_mask = _head_mask()
    first_half = (_iota2((tb, MIX_W), 1) & (HEAD_DIM - 1)) < HEAD_DIM // 2
    kv_mask = jnp.where((_iota2((N_HEADS * C, MIX_W), 0) >> int(math.log2(C)))
                        == (_iota2((N_HEADS * C, MIX_W), 1) >> int(math.log2(HEAD_DIM))),
                        1.0, 0.0).astype(BF16)
    lane_head = _iota2((HEAD_DIM, MIX_W), 1) >> int(math.log2(HEAD_DIM))
    cos = cos_ref[...]
    sin = sin_ref[...]

    def rope(t):
        partner = jnp.where(first_half, pltpu.roll(t, MIX_W - HEAD_DIM // 2, 1),
                            pltpu.roll(t, HEAD_DIM // 2, 1))
        return t * cos + partner * sin

    def stack_heads(t):
        return jnp.concatenate([t.astype(BF16)] * N_HEADS, axis=0) * kv_mask

    for b in range(n_batch):
        blk = c_ref[b]
        q = rope(blk[:, 0:MIX_W])
        k = rope(blk[:, MIX_W:2 * MIX_W]) * (HEAD_DIM ** -0.5)
        v = blk[:, 2 * MIX_W:3 * MIX_W]
        g = blk[:, 3 * MIX_W:4 * MIX_W]
        state = state_ref[b]
        ys = []
        for c in range(tb // C):
            rows = slice(c * C, (c + 1) * C)
            qc, kc, vc = q[rows], k[rows], v[rows]
            scores = _bmm_nt(qc, stack_heads(kc)) * dmat_ref[...]
            inner = _bmm(scores, stack_heads(vc))
            cross = _bmm(qc * qdec_ref[...], _head_blockdiag(state, head_mask))
            ys.append(inner + cross)
            full = _mm_tn(kc * kdec_ref[...], vc)
            upd = full[0:HEAD_DIM]
            for h in range(1, N_HEADS):
                upd = jnp.where(lane_head == h, full[h * HEAD_DIM:(h + 1) * HEAD_DIM], upd)
            state = state * cdec_ref[...] + upd
        state_ref[b] = state
        y = jnp.concatenate(ys, axis=0)
        yc = y - _head_sum(y, head_mask) * (1.0 / HEAD_DIM)
        var = _head_sum(yc * yc, head_mask) * (1.0 / HEAD_DIM)
        yn = yc * lax.rsqrt(var + RET_GN_EPS)
        o_ref[b] = (g * _sigmoid(g) * (yn * gn_ref[...])).astype(o_ref.dtype)


def _retention_tables():
    C = RET_CHUNK
    log_gamma = jnp.log1p(-jnp.exp2(-5.0 - jnp.arange(N_HEADS, dtype=F32)))
    idx = jnp.arange(C, dtype=F32)
    dist = idx[:, None] - idx[None, :]
    dmat = jnp.where(dist >= 0, jnp.exp(log_gamma[:, None, None] * jnp.maximum(dist, 0.0)), 0.0)
    dmat = dmat.transpose(1, 0, 2).reshape(C, N_HEADS * C)
    lanes = lambda t: jnp.repeat(t, HEAD_DIM, axis=-1)
    q_dec = lanes(jnp.exp(log_gamma[None, :] * (idx[:, None] + 1.0)))
    k_dec = lanes(jnp.exp(log_gamma[None, :] * (C - 1.0 - idx[:, None])))
    c_dec = lanes(jnp.exp(log_gamma * C)[None, :])
    return dmat, q_dec, k_dec, c_dec


def _retention(ret_c, cos, sin, gn, tb=256):
    B, S, W = ret_c.shape
    tables = _retention_tables()
    return pl.pallas_call(
        functools.partial(_retention_kernel, tb=tb),
        out_shape=jax.ShapeDtypeStruct((B, S, MIX_W), BF16),
        grid=(S // tb,),
        in_specs=[pl.BlockSpec((B, tb, W), lambda i: (0, i, 0)),
                  pl.BlockSpec((tb, MIX_W), lambda i: (i, 0)),
                  pl.BlockSpec((tb, MIX_W), lambda i: (i, 0)),
                  _resident((1, MIX_W))] + [_resident(t.shape) for t in tables],
        out_specs=pl.BlockSpec((B, tb, MIX_W), lambda i: (0, i, 0)),
        scratch_shapes=[pltpu.VMEM((B, HEAD_DIM, MIX_W), F32)],
        compiler_params=_cparams(("arbitrary",)),
        name="retention",
    )(ret_c, cos, sin, gn, *tables)


def _rope_tables(S):
    half = HEAD_DIM // 2
    inv_freq = ROPE_BASE ** (-jnp.arange(half, dtype=F32) / half)
    ang = jnp.arange(S, dtype=F32)[:, None] * inv_freq[None, :]
    cos, sin = jnp.cos(ang), jnp.sin(ang)
    cos_h = jnp.concatenate([cos, cos], axis=-1)
    sin_h = jnp.concatenate([-sin, sin], axis=-1)
    return jnp.tile(cos_h, (1, N_HEADS)), jnp.tile(sin_h, (1, N_HEADS))


LOG2E = math.log2(math.e)
FOX_AUG = HEAD_DIM
FOX_TQ = 512
FOX_TK = 256
FOX_UNROLL = 4


def _fox_prep_kernel(c_ref, f_ref, bias_ref, qg_ref, kg_ref, q_ref, k_ref, vt_ref, carry_ref, *, ts,
                     tk):
    @pl.when(pl.program_id(1) == 0)
    def _():
        carry_ref[...] = jnp.zeros_like(carry_ref)

    log_f = _log_sigmoid(f_ref[...] + bias_ref[...])
    tril = (_iota2((ts, ts), 0) >= _iota2((ts, ts), 1)).astype(BF16)
    c = _mask_mm_f32(tril, log_f) + carry_ref[0:1, :]
    carry_ref[0:1, :] = c[ts - 1:ts, :]
    c = c * LOG2E

    blk = c_ref[...]
    lane = _iota2((ts, LANES), 1)
    zeros = jnp.zeros((ts, LANES - HEAD_DIM), F32)
    pad = lambda t: jnp.concatenate([t, zeros], axis=1)

    def rms(t, gain):
        return t * lax.rsqrt(jnp.mean(t * t, axis=-1, keepdims=True) + NORM_EPS) * gain

    def at_lanes(lo, vals, rest):
        out = rest
        for i, val in reversed(list(enumerate(vals))):
            out = jnp.where(lane == lo + i, val, out)
        return out

    for h in range(N_HEADS):
        lo = h * HEAD_DIM
        ch = jnp.broadcast_to(c[:, h:h + 1], (ts, LANES))
        c1 = ch.astype(BF16).astype(F32)
        r1 = ch - c1
        c2 = r1.astype(BF16).astype(F32)
        c3 = (r1 - c2).astype(BF16).astype(F32)
        qh = rms(blk[:, lo:lo + HEAD_DIM], qg_ref[...]) * (LOG2E * HEAD_DIM ** -0.5)
        kh = rms(blk[:, MIX_W + lo:MIX_W + lo + HEAD_DIM], kg_ref[...])
        vh = blk[:, 2 * MIX_W + lo:2 * MIX_W + lo + HEAD_DIM]
        one = jnp.ones((ts, LANES), F32)
        q_aug = at_lanes(FOX_AUG, [c1, c2, c3, one, one, one], 0.0)
        k_aug = at_lanes(FOX_AUG, [one, one, one, -c1, -c2, -c3], 0.0)
        v_aug = at_lanes(FOX_AUG, [one], 0.0)
        q_ref[h] = jnp.where(lane < HEAD_DIM, pad(qh), q_aug).astype(BF16)
        k_ref[h] = jnp.where(lane < HEAD_DIM, pad(kh), k_aug).astype(BF16)
        v_t = jnp.where(lane < HEAD_DIM, pad(vh), v_aug).T
        for kb in range(ts // tk):
            vt_ref[h, kb] = v_t[:, kb * tk:(kb + 1) * tk].astype(BF16)


def _fox_prep(fox_c, f_c, f_bias, q_gain, k_gain, ts=512):
    B, S, W = fox_c.shape
    tk = FOX_TK
    head_spec = pl.BlockSpec((None, N_HEADS, ts, LANES), lambda b, i: (b, 0, i, 0))
    shp = jax.ShapeDtypeStruct((B, N_HEADS, S, LANES), BF16)
    return pl.pallas_call(
        functools.partial(_fox_prep_kernel, ts=ts, tk=tk),
        out_shape=(shp, shp, jax.ShapeDtypeStruct((B, N_HEADS, S // tk, LANES, tk), BF16)),
        grid=(B, S // ts),
        in_specs=[pl.BlockSpec((None, ts, W), lambda b, i: (b, i, 0)),
                  pl.BlockSpec((None, ts, LANES), lambda b, i: (b, i, 0)),
                  _resident((1, LANES)), _resident((1, HEAD_DIM)), _resident((1, HEAD_DIM))],
        out_specs=(head_spec, head_spec,
                   pl.BlockSpec((None, N_HEADS, ts // tk, LANES, tk), lambda b, i: (b, 0, i, 0, 0))),
        scratch_shapes=[pltpu.VMEM((8, LANES), F32)],
        compiler_params=_cparams(("parallel", "arbitrary")),
        name="fox_prep",
    )(fox_c, f_c, f_bias, q_gain, k_gain)


def _fox_attn_kernel(q_ref, k_ref, vt_ref, o_ref, s0_ref, s1_ref, p0_ref, p1_ref, m_ref, alpha_ref,
                     acc_ref, *, tq, tk):
    qi = pl.program_id(2)
    m_ref[...] = jnp.full(m_ref.shape, -jnp.inf, F32)
    acc_ref[...] = jnp.zeros_like(acc_ref)
    p1_ref[...] = jnp.zeros_like(p1_ref)

    def qk(j, s_ref):
        start = pl.multiple_of(j * tk, tk)
        s_ref[...] = _mm_nt(k_ref[pl.ds(start, tk), :], q_ref[...])

    def pv(j, p_ref):
        return jnp.dot(vt_ref[j], p_ref[...], preferred_element_type=F32)

    def softmax(s_ref, p_ref, key_offset):
        for c in range(0, tq, LANES):
            cols = slice(c, c + LANES)
            s = s_ref[:, cols]
            if key_offset is not None and key_offset + tk - 1 > c:
                keep = _iota2((tk, LANES), 0) + key_offset <= _iota2((tk, LANES), 1) + c
                s = jnp.where(keep, s, -jnp.inf)
            m_old = m_ref[:, cols]
            m_new = jnp.maximum(m_old, jnp.max(s, axis=0, keepdims=True))
            p_ref[:, cols] = jnp.exp2(s - m_new).astype(BF16)
            m_ref[:, cols] = m_new
            alpha_ref[:, cols] = jnp.exp2(m_old - m_new)

    def advance(pv_prev):
        acc_ref[...] = (acc_ref[...] + pv_prev) * alpha_ref[...]

    qk(0, s0_ref)

    def pair(i):
        j = 2 * i
        qk(j + 1, s1_ref)
        pv_prev = pv(jnp.maximum(j - 1, 0), p1_ref)
        softmax(s0_ref, p0_ref, None)
        advance(pv_prev)
        qk(j + 2, s0_ref)
        pv_prev = pv(j, p0_ref)
        softmax(s1_ref, p1_ref, None)
        advance(pv_prev)

    def unrolled(i, carry):
        for u in range(FOX_UNROLL):
            pair(FOX_UNROLL * i + u)
        return carry

    def single(i, carry):
        pair(i)
        return carry

    n_main = qi // FOX_UNROLL
    lax.fori_loop(0, n_main, unrolled, 0)
    lax.fori_loop(n_main * FOX_UNROLL, qi, single, 0)

    j = 2 * qi
    qk(j + 1, s1_ref)
    pv_prev = pv(jnp.maximum(j - 1, 0), p1_ref)
    softmax(s0_ref, p0_ref, 0)
    advance(pv_prev)
    pv_prev = pv(j, p0_ref)
    softmax(s1_ref, p1_ref, tk)
    advance(pv_prev)
    acc = acc_ref[...] + pv(j + 1, p1_ref)
    o_ref[...] = (acc[0:HEAD_DIM] / acc[FOX_AUG:FOX_AUG + 1]).astype(o_ref.dtype)


def _fox_attn(q, k, vt):
    B, H, S, W = q.shape
    tq, tk = FOX_TQ, FOX_TK
    assert tq == 2 * tk and S % tq == 0
    return pl.pallas_call(
        functools.partial(_fox_attn_kernel, tq=tq, tk=tk),
        out_shape=jax.ShapeDtypeStruct((B, H, HEAD_DIM, S), BF16),
        grid=(B, H, S // tq),
        in_specs=[pl.BlockSpec((None, None, tq, W), lambda b, h, i: (b, h, i, 0)),
                  pl.BlockSpec((None, None, S, W), lambda b, h, i: (b, h, 0, 0)),
                  pl.BlockSpec((None, None, S // tk, W, tk), lambda b, h, i: (b, h, 0, 0, 0))],
        out_specs=pl.BlockSpec((None, None, HEAD_DIM, tq), lambda b, h, i: (b, h, 0, i)),
        scratch_shapes=[pltpu.VMEM((tk, tq), F32), pltpu.VMEM((tk, tq), F32),
                        pltpu.VMEM((tk, tq), BF16), pltpu.VMEM((tk, tq), BF16),
                        pltpu.VMEM((1, tq), F32), pltpu.VMEM((1, tq), F32),
                        pltpu.VMEM((W, tq), F32)],
        compiler_params=_cparams(("parallel", "parallel", "arbitrary")),
        name="fox_attn",
    )(q, k, vt)


def _unit_lower_inverse(a_list, head_mask):
    C, W = a_list[0].shape
    bdiag = lambda t: _head_blockdiag(t, head_mask)
    ii = _iota2((C, W), 0)
    jj = _iota2((C, W), 1) & (C - 1)
    shift = int(math.log2(RWKV_SUB))
    same_blk = (ii >> shift) == (jj >> shift)
    eye = (ii == jj).astype(F32)
    d = [jnp.where(same_blk, a, 0.0) for a in a_list]
    low = [a - di for a, di in zip(a_list, d)]
    x = [eye + di for di in d]
    dp = d
    bd = [bdiag(t) for t in dp]
    for _ in range(shift - 1):
        dp = [_bmm(t, b) for t, b in zip(dp, bd)]
        yield
        bd = [bdiag(t) for t in dp]
        x = [xi + _bmm(xi, b) for xi, b in zip(x, bd)]
        yield
    n = [_bmm(xi, bdiag(lo)) for xi, lo in zip(x, low)]
    yield
    y = [eye + ni for ni in n]
    npow = n
    for _ in range(int(math.log2(C // RWKV_SUB)) - 1):
        bd = [bdiag(t) for t in npow]
        npow = [_bmm(t, b) for t, b in zip(npow, bd)]
        yield
        y = [yi + _bmm(yi, bdiag(t)) for yi, t in zip(y, npow)]
        yield
    return [_bmm(yi, bdiag(xi)) for yi, xi in zip(y, x)]


RWKV_STASH = (
    ("ar", lambda C, tb: (2 * C, MIX_W), BF16, True),
    ("vc", lambda C, tb: (C, MIX_W), BF16, True),
    ("bd_v", lambda C, tb: (MIX_W, MIX_W), BF16, True),
    ("a_r", lambda C, tb: (C, 2 * MIX_W), BF16, True),
    ("bk_end", lambda C, tb: (2 * C, MIX_W), BF16, True),
    ("t_inv", lambda C, tb: (C, MIX_W), BF16, True),
    ("akv", lambda C, tb: (C, MIX_W), F32, True),
    ("p_end", lambda C, tb: (1, MIX_W), F32, True),
    ("g", lambda C, tb: (tb, MIX_W), F32, False),
    ("bonus", lambda C, tb: (tb, MIX_W), F32, False),
)


def _interleave(*generators):
    alive = list(generators)
    while alive:
        for gen in list(alive):
            try:
                next(gen)
            except StopIteration:
                alive.remove(gen)


def _rwkv_kernel(c_ref, mu_ref, w0_ref, w2_ref, a0_ref, a2_ref, g2_ref, kk_ref, ka_ref, rk_ref,
                 gn_ref, o_ref, state_ref, carry_ref, y_ref, *stash_refs, tb):
    C = RWKV_CHUNK
    W = MIX_W
    n_batch = c_ref.shape[0]
    n_chunks = tb // C
    stash = {name: ref for (name, _, _, _), ref in zip(RWKV_STASH, stash_refs)}
    step = pl.program_id(0)
    n_entries = n_batch * n_chunks
    wr_chunk = (step % 2) * n_entries
    rd_chunk = ((step + 1) % 2) * n_entries
    wr_batch = (step % 2) * n_batch
    rd_batch = ((step + 1) % 2) * n_batch

    @pl.when(pl.program_id(0) == 0)
    def _():
        state_ref[...] = jnp.zeros_like(state_ref)
        carry_ref[...] = jnp.zeros_like(carry_ref)
        for ref in stash_refs:
            ref[...] = jnp.zeros_like(ref)

    head_mask = _head_mask()
    bdiag = lambda t: _head_blockdiag(t, head_mask)
    ii = _iota2((C, W), 0)
    jj = _iota2((C, W), 1) & (C - 1)
    strict = ii > jj
    incl = ii >= jj
    tril = (_iota2((C, C), 0) >= _iota2((C, C), 1)).astype(BF16)
    lane_head = _iota2((HEAD_DIM, W), 1) >> int(math.log2(HEAD_DIM))
    prepared = {}

    def prepare():
        post = []
        pre = []
        for b in range(n_batch):
            cur = c_ref[b]
            prev = jnp.where(_iota2(cur.shape, 0) == 0, carry_ref[b, 0:1, :],
                             pltpu.roll(cur, 1, 0))
            carry_ref[b, 0:1, :] = cur[tb - 1:tb, :]
            mixed = cur + (prev - cur) * mu_ref[...]
            r = mixed[:, 0:W]
            k = mixed[:, W:2 * W]
            v = mixed[:, 2 * W:3 * W]
            xw = mixed[:, 3 * W:3 * W + DECAY_LORA]
            xa = mixed[:, 3 * W + DECAY_LORA:3 * W + DECAY_LORA + AAA_LORA]
            xg = mixed[:, 3 * W + DECAY_LORA + AAA_LORA:]

            log_w = -jnp.exp(_log_sigmoid(w0_ref[...] + _mm(jnp.tanh(xw), w2_ref[...])) - 0.5)
            a = _sigmoid(a0_ref[...] + _mm(xa, a2_ref[...]))
            g = _mm(_sigmoid(xg), g2_ref[...])
            kk = k * kk_ref[...]
            k2 = k * (1.0 + (a - 1.0) * ka_ref[...])
            kk = kk / jnp.maximum(jnp.sqrt(_head_sum(kk * kk, head_mask)), 1e-12)
            a_vec = -kk
            b_vec = kk * a
            post.append((g, _head_sum(r * k2 * rk_ref[...], head_mask) * v))
            yield

            for c in range(n_chunks):
                rows = slice(c * C, (c + 1) * C)
                lw = log_w[rows]
                cum = _mask_mm_f32(tril, lw)
                e_pos = jnp.exp(cum)
                e_neg = jnp.exp(-cum)
                p_end = e_pos[C - 1:C, :]
                at = a_vec[rows] * jnp.exp(cum - lw)
                rt = r[rows] * e_pos
                bt = b_vec[rows] * e_neg
                kt = k2[rows] * e_neg
                ar = jnp.concatenate([at, rt], axis=0)
                gb = _bmm_nt(ar, bdiag(bt))
                gk = _bmm_nt(ar, bdiag(kt))
                vc = v[rows]
                pre.append(dict(
                    ar=ar, p_end=p_end, vc=vc, bd_v=bdiag(vc),
                    a_ab=jnp.where(strict, gb[0:C], 0.0), a_ak=jnp.where(strict, gk[0:C], 0.0),
                    a_r=jnp.concatenate([jnp.where(incl, gb[C:], 0.0),
                                         jnp.where(incl, gk[C:], 0.0)], axis=1),
                    bk_end=jnp.concatenate([bt * p_end, kt * p_end], axis=0)))
                yield
        t_inv = yield from _unit_lower_inverse([p["a_ab"] for p in pre], head_mask)
        for p, t in zip(pre, t_inv):
            p["t_inv"] = t
            p["akv"] = _bmm(p["a_ak"], p["bd_v"])
        prepared["pre"] = pre
        prepared["post"] = post

    def recur():
        s_all = [state_ref[b] for b in range(n_batch)]
        batches = range(n_batch)
        for c in range(n_chunks):
            ent = [rd_chunk + b * n_chunks + c for b in batches]
            ars = [_bmm_nt(stash["ar"][e], bdiag(s_all[b])) for b, e in zip(batches, ent)]
            yield
            u = [_bmm(stash["t_inv"][e], bdiag(ars[b][0:C] + stash["akv"][e]))
                 for b, e in zip(batches, ent)]
            yield
            for b, e in zip(batches, ent):
                uv = jnp.concatenate([u[b].astype(BF16), stash["vc"][e]], axis=0)
                y_ref[b, c * C:(c + 1) * C, :] = ars[b][C:] + jnp.dot(
                    stash["a_r"][e], jnp.concatenate([bdiag(u[b]), stash["bd_v"][e]], axis=0),
                    preferred_element_type=F32)
                full = _mm_tn(uv, stash["bk_end"][e])
                upd = full[0:HEAD_DIM]
                for h in range(1, N_HEADS):
                    upd = jnp.where(lane_head == h, full[h * HEAD_DIM:(h + 1) * HEAD_DIM], upd)
                s_all[b] = s_all[b] * stash["p_end"][e] + upd
            yield
        for b in range(n_batch):
            state_ref[b] = s_all[b]
            y = y_ref[b]
            yc = y - _head_sum(y, head_mask) * (1.0 / HEAD_DIM)
            var = _head_sum(yc * yc, head_mask) * (1.0 / HEAD_DIM)
            yn = yc * lax.rsqrt(var + RWKV_GN_EPS) * gn_ref[...]
            o_ref[b] = ((yn + stash["bonus"][rd_batch + b])
                        * stash["g"][rd_batch + b]).astype(o_ref.dtype)
            yield

    _interleave(prepare(), recur())

    for e, p in enumerate(prepared["pre"]):
        for name, _, dtype, per_chunk in RWKV_STASH:
            if per_chunk:
                stash[name][wr_chunk + e] = p[name].astype(dtype)
    for b, (g, bonus) in enumerate(prepared["post"]):
        stash["g"][wr_batch + b] = g
        stash["bonus"][wr_batch + b] = bonus


def _rwkv(rwkv_c, mu, w0, w2, a0, a2, g2, k_k, k_a, r_k, gn, tb=256):
    B, S, W = rwkv_c.shape
    n_blocks = S // tb
    n_entries = B * (tb // RWKV_CHUNK)
    vec = lambda t: _resident((1, t.shape[-1]))
    mat = lambda t: _resident(t.shape)
    stash_shapes = [pltpu.VMEM((2 * (n_entries if per_chunk else B),) + shape(RWKV_CHUNK, tb), dtype)
                    for _, shape, dtype, per_chunk in RWKV_STASH]
    return pl.pallas_call(
        functools.partial(_rwkv_kernel, tb=tb),
        out_shape=jax.ShapeDtypeStruct((B, S, MIX_W), BF16),
        grid=(n_blocks + 1,),
        in_specs=[pl.BlockSpec((B, tb, W), lambda i: (0, jnp.minimum(i, n_blocks - 1), 0)),
                  vec(mu), vec(w0), mat(w2), vec(a0), mat(a2), mat(g2), vec(k_k), vec(k_a),
                  vec(r_k), vec(gn)],
        out_specs=pl.BlockSpec((B, tb, MIX_W), lambda i: (0, jnp.maximum(i - 1, 0), 0)),
        scratch_shapes=[pltpu.VMEM((B, HEAD_DIM, MIX_W), F32),
                        pltpu.VMEM((B, 8, W), F32),
                        pltpu.VMEM((B, tb, MIX_W), F32)] + stash_shapes,
        compiler_params=_cparams(("arbitrary",)),
        name="rwkv7",
    )(rwkv_c, mu, w0, w2, a0, a2, g2, k_k, k_a, r_k, gn)


def _merge_kernel(x_ref, gate_ref, yret_ref, yfox_ref, yrwkv_ref, pret_ref, pfox_ref, prwkv_ref,
                  wout_ref, o_ref, *, d_model):
    D = d_model
    dot = lambda a, b: jnp.dot(a, b, preferred_element_type=F32)
    gates = gate_ref[...]
    fox = lax.dot_general(yfox_ref[...], pfox_ref[...], (((0,), (0,)), ((), ())),
                          preferred_element_type=F32)
    merged = (_sigmoid(gates[:, 0:D]) * dot(yret_ref[...], pret_ref[...])
              + _sigmoid(gates[:, D:2 * D]) * fox
              + _sigmoid(gates[:, 2 * D:3 * D]) * dot(yrwkv_ref[...], prwkv_ref[...]))
    o_ref[...] = x_ref[...] + dot(merged.astype(BF16), wout_ref[...])


def _merge(x, gates, y_ret, y_fox, y_rwkv, p_ret, p_fox, p_rwkv, w_out, tm=512):
    B, S, D = x.shape
    row = lambda w: pl.BlockSpec((None, tm, w), lambda b, i: (b, i, 0))
    return pl.pallas_call(
        functools.partial(_merge_kernel, d_model=D),
        out_shape=jax.ShapeDtypeStruct((B, S, D), F32),
        grid=(B, S // tm),
        in_specs=[row(D), row(3 * D), row(MIX_W),
                  pl.BlockSpec((None, MIX_W, tm), lambda b, i: (b, 0, i)),
                  row(MIX_W), _resident(p_ret.shape), _resident(p_fox.shape),
                  _resident(p_rwkv.shape), _resident(w_out.shape)],
        out_specs=row(D),
        compiler_params=_cparams(("parallel", "parallel")),
        name="merge_out",
    )(x, gates, y_ret, y_fox, y_rwkv, p_ret, p_fox, p_rwkv, w_out)


def _ffn_kernel(x_ref, g_ref, wgu_ref, wd_ref, o_ref, *, d_ff, ck):
    x = x_ref[...]
    h = (x * lax.rsqrt(jnp.mean(x * x, axis=-1, keepdims=True) + NORM_EPS) * g_ref[...]).astype(BF16)
    acc = x
    for c in range(d_ff // ck):
        lo = c * ck
        gate = jnp.dot(h, wgu_ref[:, lo:lo + ck], preferred_element_type=F32)
        up = jnp.dot(h, wgu_ref[:, d_ff + lo:d_ff + lo + ck], preferred_element_type=F32)
        act = (gate * _sigmoid(gate) * up).astype(BF16)
        acc = acc + jnp.dot(act, wd_ref[lo:lo + ck, :], preferred_element_type=F32)
    o_ref[...] = acc


def _ffn(x, gain, w_gate_up, w_down, tm=512, ck=256):
    B, S, D = x.shape
    d_ff = w_down.shape[0]
    row = pl.BlockSpec((None, tm, D), lambda b, i: (b, i, 0))
    return pl.pallas_call(
        functools.partial(_ffn_kernel, d_ff=d_ff, ck=ck),
        out_shape=jax.ShapeDtypeStruct((B, S, D), F32),
        grid=(B, S // tm),
        in_specs=[row, _resident((1, D)), _resident(w_gate_up.shape), _resident(w_down.shape)],
        out_specs=row,
        compiler_params=_cparams(("parallel", "parallel")),
        name="ffn",
    )(x, gain, w_gate_up, w_down)


def kernel(x, mix_norm, w_in, ret_gn, fox_q_norm, fox_k_norm, fox_f_bias, rwkv_mu, rwkv_w0, rwkv_w2,
           rwkv_a0, rwkv_a2, rwkv_g2, rwkv_k_k, rwkv_k_a, rwkv_r_k, rwkv_gn, p_ret, p_fox, p_rwkv,
           w_out, ffn_norm, w_gate_up, w_down):
    B, S, D = x.shape
    depth = w_in.shape[0]
    cos, sin = _rope_tables(S)
    o0, o1, o2, o3 = (int(v) for v in IN_OFF)
    f_lo = o1 + FOXQKV_COLS
    row = lambda t: t.reshape(1, -1)
    for l in range(depth):
        w_cat = _w_in_relayout(w_in[l], f_lo, N_HEADS)
        ret_c, fox_c, rwkv_c, gate_c, f_c = _in_proj(x, row(mix_norm[l]), w_cat)

        y_ret = _retention(ret_c, cos, sin, row(ret_gn[l]))

        f_bias = jnp.pad(fox_f_bias[l], (0, LANES - N_HEADS)).reshape(1, LANES)
        fq, fk, fv = _fox_prep(fox_c, f_c, f_bias, row(fox_q_norm[l]), row(fox_k_norm[l]))
        y_fox = _fox_attn(fq, fk, fv).reshape(B, MIX_W, S)

        y_rwkv = _rwkv(rwkv_c, row(rwkv_mu[l]), row(rwkv_w0[l]), rwkv_w2[l].astype(BF16),
                       row(rwkv_a0[l]), rwkv_a2[l].astype(BF16), rwkv_g2[l].astype(BF16),
                       row(rwkv_k_k[l]), row(rwkv_k_a[l]), row(rwkv_r_k[l]), row(rwkv_gn[l]))

        x = _merge(x, gate_c, y_ret, y_fox, y_rwkv, p_ret[l].astype(BF16), p_fox[l].astype(BF16),
                   p_rwkv[l].astype(BF16), w_out[l].astype(BF16))
        x = _ffn(x, row(ffn_norm[l]), w_gate_up[l].astype(BF16), w_down[l].astype(BF16))
    return x
```

```python
import functools
import math

import jax
import jax.numpy as jnp
import numpy as np
from jax import lax
from jax.experimental import pallas as pl
from jax.experimental.pallas import tpu as pltpu

F32 = jnp.float32
BF16 = jnp.bfloat16

HEAD_DIM = 64
N_HEADS = 4
MIX_W = N_HEADS * HEAD_DIM
RET_CHUNK = 128
ROPE_BASE = 10000.0
DECAY_LORA = 64
AAA_LORA = 64
GATE_LORA = 128
NORM_EPS = 1e-6
RET_GN_EPS = 1e-5
RWKV_GN_EPS = 64e-5
RWKV_CHUNK = 64
RWKV_SUB = 16
LANES = 128
FOX_SPLIT = 3
VMEM_LIMIT = 52 * 1024 * 1024


def _cparams(sem):
    return pltpu.CompilerParams(dimension_semantics=sem, vmem_limit_bytes=VMEM_LIMIT)


def _resident(shape):
    nd = len(shape)
    return pl.BlockSpec(shape, lambda *_: (0,) * nd, pipeline_mode=pl.Buffered(1))


def _mm(a, b):
    return jnp.dot(a.astype(BF16), b.astype(BF16), preferred_element_type=F32)


def _mm_nt(a, b):
    return lax.dot_general(a.astype(BF16), b.astype(BF16), (((1,), (1,)), ((), ())),
                           preferred_element_type=F32)


def _mm_tn(a, b):
    return lax.dot_general(a.astype(BF16), b.astype(BF16), (((0,), (0,)), ((), ())),
                           preferred_element_type=F32)


def _mask_mm_f32(mask_bf16, x):
    h1 = x.astype(BF16)
    r1 = x - h1.astype(F32)
    h2 = r1.astype(BF16)
    h3 = (r1 - h2.astype(F32)).astype(BF16)
    dot = lambda h: jnp.dot(mask_bf16, h, preferred_element_type=F32)
    return dot(h1) + dot(h2) + dot(h3)


def _sigmoid(x):
    return 0.5 * jnp.tanh(0.5 * x) + 0.5


def _log_sigmoid(x):
    return jnp.minimum(x, 0.0) - jnp.log1p(jnp.exp(-jnp.abs(x)))


def _iota2(shape, dim):
    return lax.broadcasted_iota(jnp.int32, shape, dim)


def _head_mask():
    same = (_iota2((MIX_W, MIX_W), 0) >> 6) == (_iota2((MIX_W, MIX_W), 1) >> 6)
    return jnp.where(same, 1.0, 0.0).astype(BF16)


def _head_blockdiag(x, head_mask):
    xb = x.astype(BF16)
    return jnp.concatenate([xb] * N_HEADS, axis=0) * head_mask


def _head_sum(x, head_mask):
    hi = x.astype(BF16)
    lo = (x - hi.astype(F32)).astype(BF16)
    dot = lambda t: jnp.dot(t, head_mask, preferred_element_type=F32)
    return dot(hi) + dot(lo)


def _bmm(a, bd):
    return jnp.dot(a.astype(BF16), bd, preferred_element_type=F32)


def _bmm_nt(a, bd):
    return lax.dot_general(a.astype(BF16), bd, (((1,), (1,)), ((), ())), preferred_element_type=F32)


RET_COLS = 4 * MIX_W
FOXQKV_COLS = 3 * MIX_W
RWKV_COLS = 3 * MIX_W + DECAY_LORA + AAA_LORA + GATE_LORA
IN_OFF = np.cumsum([0, RET_COLS, FOXQKV_COLS, RWKV_COLS])


def _w_in_relayout_kernel(w_ref, o_ref, *, f_lo, n_f):
    ncol = w_ref.shape[1]
    rest = ncol - (f_lo + n_f)
    o_ref[:, 0:f_lo] = w_ref[:, 0:f_lo].astype(BF16)
    o_ref[:, f_lo:f_lo + rest] = w_ref[:, f_lo + n_f:ncol].astype(BF16)
    win = w_ref[:, f_lo:f_lo + LANES]
    lane = _iota2(win.shape, 1)
    reps = jnp.where(lane < n_f, win, 0.0)
    for t in range(1, FOX_SPLIT):
        reps = jnp.where((lane >= t * n_f) & (lane < (t + 1) * n_f), pltpu.roll(win, t * n_f, 1), reps)
    o_ref[:, f_lo + rest:] = reps.astype(BF16)


def _w_in_relayout(w, f_lo, n_f, tr=128):
    D, ncol = w.shape
    out_cols = ncol - n_f + LANES
    return pl.pallas_call(
        functools.partial(_w_in_relayout_kernel, f_lo=f_lo, n_f=n_f),
        out_shape=jax.ShapeDtypeStruct((D, out_cols), BF16),
        grid=(D // tr,),
        in_specs=[pl.BlockSpec((tr, ncol), lambda i: (i, 0))],
        out_specs=pl.BlockSpec((tr, out_cols), lambda i: (i, 0)),
        compiler_params=_cparams(("parallel",)),
        name="w_in_relayout",
    )(w)


def _in_proj_kernel(x_ref, g_ref, w_ref, ret_ref, fox_ref, rwkv_ref, gate_ref, f_ref, *, d_model):
    x = x_ref[...]
    h = x * lax.rsqrt(jnp.mean(x * x, axis=-1, keepdims=True) + NORM_EPS) * g_ref[...]
    hb = h.astype(BF16)
    o0, o1, o2, o3 = (int(v) for v in IN_OFF)
    o4 = o3 + 3 * d_model
    dot = lambda lo, hi: jnp.dot(hb, w_ref[:, lo:hi], preferred_element_type=F32)
    ret_ref[...] = dot(o0, o1)
    fox_ref[...] = dot(o1, o2)
    rwkv_ref[...] = dot(o2, o3)
    gate_ref[...] = dot(o3, o4)
    f_ref[...] = dot(o4, o4 + LANES)


def _in_proj(x, gain, w_cat, tm=256):
    B, S, D = x.shape
    ncol = w_cat.shape[1]
    row = lambda w: pl.BlockSpec((None, tm, w), lambda b, i: (b, i, 0))
    widths = (RET_COLS, FOXQKV_COLS, RWKV_COLS, 3 * D, LANES)
    return pl.pallas_call(
        functools.partial(_in_proj_kernel, d_model=D),
        out_shape=tuple(jax.ShapeDtypeStruct((B, S, w), F32) for w in widths),
        grid=(B, S // tm),
        in_specs=[row(D), _resident((1, D)), _resident((D, ncol))],
        out_specs=tuple(row(w) for w in widths),
        compiler_params=_cparams(("parallel", "parallel")),
        name="in_proj",
    )(x, gain, w_cat)


def _retention_kernel(c_ref, cos_ref, sin_ref, gn_ref, dmat_ref, qdec_ref, kdec_ref, cdec_ref, o_ref,
                      state_ref, *, tb):
    C = RET_CHUNK
    n_batch = c_ref.shape[0]

    @pl.when(pl.program_id(0) == 0)
    def _():
        state_ref[...] = jnp.zeros_like(state_ref)

    head_mask = _head_mask()
    first_half = (_iota2((tb, MIX_W), 1) & (HEAD_DIM - 1)) < HEAD_DIM // 2
    kv_mask = jnp.where((_iota2((N_HEADS * C, MIX_W), 0) >> int(math.log2(C)))
                        == (_iota2((N_HEADS * C, MIX_W), 1) >> int(math.log2(HEAD_DIM))),
                        1.0, 0.0).astype(BF16)
    lane_head = _iota2((HEAD_DIM, MIX_W), 1) >> int(math.log2(HEAD_DIM))
    cos = cos_ref[...]
    sin = sin_ref[...]

    def rope(t):
        partner = jnp.where(first_half, pltpu.roll(t, MIX_W - HEAD_DIM // 2, 1),
                            pltpu.roll(t, HEAD_DIM // 2, 1))
        return t * cos + partner * sin

    def stack_heads(t):
        return jnp.concatenate([t.astype(BF16)] * N_HEADS, axis=0) * kv_mask

    for b in range(n_batch):
        blk = c_ref[b]
        q = rope(blk[:, 0:MIX_W])
        k = rope(blk[:, MIX_W:2 * MIX_W]) * (HEAD_DIM ** -0.5)
        v = blk[:, 2 * MIX_W:3 * MIX_W]
        g = blk[:, 3 * MIX_W:4 * MIX_W]
        state = state_ref[b]
        ys = []
        for c in range(tb // C):
            rows = slice(c * C, (c + 1) * C)
            qc, kc, vc = q[rows], k[rows], v[rows]
            scores = _bmm_nt(qc, stack_heads(kc)) * dmat_ref[...]
            inner = _bmm(scores, stack_heads(vc))
            cross = _bmm(qc * qdec_ref[...], _head_blockdiag(state, head_mask))
            ys.append(inner + cross)
            full = _mm_tn(kc * kdec_ref[...], vc)
            upd = full[0:HEAD_DIM]
            for h in range(1, N_HEADS):
                upd = jnp.where(lane_head == h, full[h * HEAD_DIM:(h + 1) * HEAD_DIM], upd)
            state = state * cdec_ref[...] + upd
        state_ref[b] = state
        y = jnp.concatenate(ys, axis=0)
        yc = y - _head_sum(y, head_mask) * (1.0 / HEAD_DIM)
        var = _head_sum(yc * yc, head_mask) * (1.0 / HEAD_DIM)
        yn = yc * lax.rsqrt(var + RET_GN_EPS)
        o_ref[b] = (g * _sigmoid(g) * (yn * gn_ref[...])).astype(o_ref.dtype)


def _retention_tables():
    C = RET_CHUNK
    log_gamma = jnp.log1p(-jnp.exp2(-5.0 - jnp.arange(N_HEADS, dtype=F32)))
    idx = jnp.arange(C, dtype=F32)
    dist = idx[:, None] - idx[None, :]
    dmat = jnp.where(dist >= 0, jnp.exp(log_gamma[:, None, None] * jnp.maximum(dist, 0.0)), 0.0)
    dmat = dmat.transpose(1, 0, 2).reshape(C, N_HEADS * C)
    lanes = lambda t: jnp.repeat(t, HEAD_DIM, axis=-1)
    q_dec = lanes(jnp.exp(log_gamma[None, :] * (idx[:, None] + 1.0)))
    k_dec = lanes(jnp.exp(log_gamma[None, :] * (C - 1.0 - idx[:, None])))
    c_dec = lanes(jnp.exp(log_gamma * C)[None, :])
    return dmat, q_dec, k_dec, c_dec


def _retention(ret_c, cos, sin, gn, tb=256):
    B, S, W = ret_c.shape
    tables = _retention_tables()
    return pl.pallas_call(
        functools.partial(_retention_kernel, tb=tb),
        out_shape=jax.ShapeDtypeStruct((B, S, MIX_W), BF16),
        grid=(S // tb,),
        in_specs=[pl.BlockSpec((B, tb, W), lambda i: (0, i, 0)),
                  pl.BlockSpec((tb, MIX_W), lambda i: (i, 0)),
                  pl.BlockSpec((tb, MIX_W), lambda i: (i, 0)),
                  _resident((1, MIX_W))] + [_resident(t.shape) for t in tables],
        out_specs=pl.BlockSpec((B, tb, MIX_W), lambda i: (0, i, 0)),
        scratch_shapes=[pltpu.VMEM((B, HEAD_DIM, MIX_W), F32)],
        compiler_params=_cparams(("arbitrary",)),
        name="retention",
    )(ret_c, cos, sin, gn, *tables)


def _rope_tables(S):
    half = HEAD_DIM // 2
    inv_freq = ROPE_BASE ** (-jnp.arange(half, dtype=F32) / half)
    ang = jnp.arange(S, dtype=F32)[:, None] * inv_freq[None, :]
    cos, sin = jnp.cos(ang), jnp.sin(ang)
    cos_h = jnp.concatenate([cos, cos], axis=-1)
    sin_h = jnp.concatenate([-sin, sin], axis=-1)
    return jnp.tile(cos_h, (1, N_HEADS)), jnp.tile(sin_h, (1, N_HEADS))


LOG2E = math.log2(math.e)
FOX_AUG = HEAD_DIM
FOX_TQ = 512
FOX_TK = 256
FOX_UNROLL = 4


def _fox_place_tables():
    H, Dh, L = N_HEADS, HEAD_DIM, LANES
    place_q = np.zeros((H // 2, MIX_W + L, 2 * L), np.float32)
    place_k = np.zeros((H // 2, MIX_W + L, 2 * L), np.float32)
    place_vt = np.zeros((H, L, MIX_W), np.float32)
    for h in range(H):
        out = (h % 2) * L
        for d in range(Dh):
            place_q[h // 2, h * Dh + d, out + d] = 1.0
            place_k[h // 2, h * Dh + d, out + d] = 1.0
            place_vt[h, d, h * Dh + d] = 1.0
        for t in range(FOX_SPLIT):
            place_q[h // 2, MIX_W + t * H + h, out + FOX_AUG + t] = 1.0
            place_k[h // 2, MIX_W + t * H + h, out + FOX_AUG + FOX_SPLIT + t] = -1.0
    return tuple(jnp.asarray(t, BF16) for t in (place_q, place_k, place_vt))


def _fox_prep_kernel(c_ref, f_ref, bias_ref, qg_ref, kg_ref, pq_ref, pk_ref, pvt_ref, q_ref, k_ref,
                     vt_ref, carry_ref, *, ts, tk):
    @pl.when(pl.program_id(1) == 0)
    def _():
        carry_ref[...] = jnp.zeros_like(carry_ref)

    log_f = _log_sigmoid(f_ref[...] + bias_ref[...])
    tril = (_iota2((ts, ts), 0) >= _iota2((ts, ts), 1)).astype(BF16)
    c = _mask_mm_f32(tril, log_f) + carry_ref[0:1, :]
    carry_ref[0:1, :] = c[ts - 1:ts, :]
    c = c * LOG2E
    lane_c = _iota2(c.shape, 1)
    csel = jnp.zeros(c.shape, BF16)
    rest = c
    for t in range(FOX_SPLIT):
        term = rest.astype(BF16)
        csel = jnp.where((lane_c >= t * N_HEADS) & (lane_c < (t + 1) * N_HEADS), term, csel)
        rest = rest - term.astype(F32)

    blk = c_ref[...]
    head_mask = _head_mask()

    def rms(t, gain):
        ms = _head_sum(t * t, head_mask) * (1.0 / HEAD_DIM)
        return t * lax.rsqrt(ms + NORM_EPS) * gain

    qn = rms(blk[:, 0:MIX_W], qg_ref[...]) * (LOG2E * HEAD_DIM ** -0.5)
    kn = rms(blk[:, MIX_W:2 * MIX_W], kg_ref[...])
    q_stack = jnp.concatenate([qn.astype(BF16), csel], axis=1)
    k_stack = jnp.concatenate([kn.astype(BF16), csel], axis=1)
    vb = blk[:, 2 * MIX_W:3 * MIX_W].astype(BF16)

    lane = _iota2((1, 2 * LANES), 1) & (LANES - 1)
    ones_q = jnp.where((lane >= FOX_AUG + FOX_SPLIT) & (lane < FOX_AUG + 2 * FOX_SPLIT), 1.0, 0.0)
    ones_k = jnp.where((lane >= FOX_AUG) & (lane < FOX_AUG + FOX_SPLIT), 1.0, 0.0)
    ones_row = _iota2((LANES, ts), 0) == FOX_AUG

    for hp in range(N_HEADS // 2):
        q2 = (jnp.dot(q_stack, pq_ref[hp], preferred_element_type=F32) + ones_q).astype(BF16)
        k2 = (jnp.dot(k_stack, pk_ref[hp], preferred_element_type=F32) + ones_k).astype(BF16)
        for sub in range(2):
            q_ref[2 * hp + sub] = q2[:, sub * LANES:(sub + 1) * LANES]
            k_ref[2 * hp + sub] = k2[:, sub * LANES:(sub + 1) * LANES]
    for h in range(N_HEADS):
        v_t = jnp.where(ones_row, 1.0, _mm_nt(pvt_ref[h], vb))
        for kb in range(ts // tk):
            vt_ref[h, kb] = v_t[:, kb * tk:(kb + 1) * tk].astype(BF16)


def _fox_prep(fox_c, f_c, f_bias, q_gain, k_gain, ts=512):
    B, S, W = fox_c.shape
    tk = FOX_TK
    tables = _fox_place_tables()
    head_spec = pl.BlockSpec((None, N_HEADS, ts, LANES), lambda b, i: (b, 0, i, 0))
    shp = jax.ShapeDtypeStruct((B, N_HEADS, S, LANES), BF16)
    return pl.pallas_call(
        functools.partial(_fox_prep_kernel, ts=ts, tk=tk),
        out_shape=(shp, shp, jax.ShapeDtypeStruct((B, N_HEADS, S // tk, LANES, tk), BF16)),
        grid=(B, S // ts),
        in_specs=[pl.BlockSpec((None, ts, W), lambda b, i: (b, i, 0)),
                  pl.BlockSpec((None, ts, LANES), lambda b, i: (b, i, 0)),
                  _resident((1, LANES)), _resident((1, MIX_W)), _resident((1, MIX_W))]
                 + [_resident(t.shape) for t in tables],
        out_specs=(head_spec, head_spec,
                   pl.BlockSpec((None, N_HEADS, ts // tk, LANES, tk), lambda b, i: (b, 0, i, 0, 0))),
        scratch_shapes=[pltpu.VMEM((8, LANES), F32)],
        compiler_params=_cparams(("parallel", "arbitrary")),
        name="fox_prep",
    )(fox_c, f_c, f_bias, q_gain, k_gain, *tables)


def _fox_attn_kernel(q_ref, k_ref, vt_ref, o_ref, s0_ref, s1_ref, p0_ref, p1_ref, m_ref, alpha_ref,
                     acc_ref, *, tq, tk):
    qi = pl.program_id(2)
    m_ref[...] = jnp.full(m_ref.shape, -jnp.inf, F32)
    acc_ref[...] = jnp.zeros_like(acc_ref)
    p1_ref[...] = jnp.zeros_like(p1_ref)

    def qk(j, s_ref):
        start = pl.multiple_of(j * tk, tk)
        s_ref[...] = _mm_nt(k_ref[pl.ds(start, tk), :], q_ref[...])

    def pv(j, p_ref):
        return jnp.dot(vt_ref[j], p_ref[...], preferred_element_type=F32)

    def softmax(s_ref, p_ref, key_offset):
        for c in range(0, tq, LANES):
            cols = slice(c, c + LANES)
            s = s_ref[:, cols]
            if key_offset is not None and key_offset + tk - 1 > c:
                keep = _iota2((tk, LANES), 0) + key_offset <= _iota2((tk, LANES), 1) + c
                s = jnp.where(keep, s, -jnp.inf)
            m_old = m_ref[:, cols]
            m_new = jnp.maximum(m_old, jnp.max(s, axis=0, keepdims=True))
            p_ref[:, cols] = jnp.exp2(s - m_new).astype(BF16)
            m_ref[:, cols] = m_new
            alpha_ref[:, cols] = jnp.exp2(m_old - m_new)

    def advance(pv_prev):
        acc_ref[...] = (acc_ref[...] + pv_prev) * alpha_ref[...]

    qk(0, s0_ref)

    def pair(i):
        j = 2 * i
        qk(j + 1, s1_ref)
        pv_prev = pv(jnp.maximum(j - 1, 0), p1_ref)
        softmax(s0_ref, p0_ref, None)
        advance(pv_prev)
        qk(j + 2, s0_ref)
        pv_prev = pv(j, p0_ref)
        softmax(s1_ref, p1_ref, None)
        advance(pv_prev)

    def unrolled(i, carry):
        for u in range(FOX_UNROLL):
            pair(FOX_UNROLL * i + u)
        return carry

    def single(i, carry):
        pair(i)
        return carry

    n_main = qi // FOX_UNROLL
    lax.fori_loop(0, n_main, unrolled, 0)
    lax.fori_loop(n_main * FOX_UNROLL, qi, single, 0)

    j = 2 * qi
    qk(j + 1, s1_ref)
    pv_prev = pv(jnp.maximum(j - 1, 0), p1_ref)
    softmax(s0_ref, p0_ref, 0)
    advance(pv_prev)
    pv_prev = pv(j, p0_ref)
    softmax(s1_ref, p1_ref, tk)
    advance(pv_prev)
    acc = acc_ref[...] + pv(j + 1, p1_ref)
    o_ref[...] = (acc[0:HEAD_DIM] / acc[FOX_AUG:FOX_AUG + 1]).astype(o_ref.dtype)


def _fox_attn(q, k, vt):
    B, H, S, W = q.shape
    tq, tk = FOX_TQ, FOX_TK
    assert tq == 2 * tk and S % tq == 0
    return pl.pallas_call(
        functools.partial(_fox_attn_kernel, tq=tq, tk=tk),
        out_shape=jax.ShapeDtypeStruct((B, H, HEAD_DIM, S), BF16),
        grid=(B, H, S // tq),
        in_specs=[pl.BlockSpec((None, None, tq, W), lambda b, h, i: (b, h, i, 0)),
                  pl.BlockSpec((None, None, S, W), lambda b, h, i: (b, h, 0, 0)),
                  pl.BlockSpec((None, None, S // tk, W, tk), lambda b, h, i: (b, h, 0, 0, 0))],
        out_specs=pl.BlockSpec((None, None, HEAD_DIM, tq), lambda b, h, i: (b, h, 0, i)),
        scratch_shapes=[pltpu.VMEM((tk, tq), F32), pltpu.VMEM((tk, tq), F32),
                        pltpu.VMEM((tk, tq), BF16), pltpu.VMEM((tk, tq), BF16),
                        pltpu.VMEM((1, tq), F32), pltpu.VMEM((1, tq), F32),
                        pltpu.VMEM((W, tq), F32)],
        compiler_params=_cparams(("parallel", "parallel", "arbitrary")),
        name="fox_attn",
    )(q, k, vt)


def _unit_lower_inverse(a_list, head_mask):
    C, W = a_list[0].shape
    bdiag = lambda t: _head_blockdiag(t, head_mask)
    ii = _iota2((C, W), 0)
    jj = _iota2((C, W), 1) & (C - 1)
    shift = int(math.log2(RWKV_SUB))
    same_blk = (ii >> shift) == (jj >> shift)
    eye = (ii == jj).astype(F32)
    d = [jnp.where(same_blk, a, 0.0) for a in a_list]
    low = [a - di for a, di in zip(a_list, d)]
    x = [eye + di for di in d]
    dp = d
    bd = [bdiag(t) for t in dp]
    for _ in range(shift - 1):
        dp = [_bmm(t, b) for t, b in zip(dp, bd)]
        yield
        bd = [bdiag(t) for t in dp]
        x = [xi + _bmm(xi, b) for xi, b in zip(x, bd)]
        yield
    n = [_bmm(xi, bdiag(lo)) for xi, lo in zip(x, low)]
    yield
    y = [eye + ni for ni in n]
    npow = n
    for _ in range(int(math.log2(C // RWKV_SUB)) - 1):
        bd = [bdiag(t) for t in npow]
        npow = [_bmm(t, b) for t, b in zip(npow, bd)]
        yield
        y = [yi + _bmm(yi, bdiag(t)) for yi, t in zip(y, npow)]
        yield
    return [_bmm(yi, bdiag(xi)) for yi, xi in zip(y, x)]


RWKV_STASH = (
    ("ar", lambda C, tb: (2 * C, MIX_W), BF16, True),
    ("vc", lambda C, tb: (C, MIX_W), BF16, True),
    ("bd_v", lambda C, tb: (MIX_W, MIX_W), BF16, True),
    ("a_r", lambda C, tb: (C, 2 * MIX_W), BF16, True),
    ("bk_end", lambda C, tb: (2 * C, MIX_W), BF16, True),
    ("t_inv", lambda C, tb: (C, MIX_W), BF16, True),
    ("akv", lambda C, tb: (C, MIX_W), F32, True),
    ("p_end", lambda C, tb: (1, MIX_W), F32, True),
    ("g", lambda C, tb: (tb, MIX_W), F32, False),
    ("bonus", lambda C, tb: (tb, MIX_W), F32, False),
)


def _interleave(*generators):
    alive = list(generators)
    while alive:
        for gen in list(alive):
            try:
                next(gen)
            except StopIteration:
                alive.remove(gen)


def _rwkv_kernel(c_ref, mu_ref, w0_ref, w2_ref, a0_ref, a2_ref, g2_ref, kk_ref, ka_ref, rk_ref,
                 gn_ref, o_ref, state_ref, carry_ref, y_ref, *stash_refs, tb):
    C = RWKV_CHUNK
    W = MIX_W
    n_batch = c_ref.shape[0]
    n_chunks = tb // C
    stash = {name: ref for (name, _, _, _), ref in zip(RWKV_STASH, stash_refs)}
    step = pl.program_id(0)
    n_entries = n_batch * n_chunks
    wr_chunk = (step % 2) * n_entries
    rd_chunk = ((step + 1) % 2) * n_entries
    wr_batch = (step % 2) * n_batch
    rd_batch = ((step + 1) % 2) * n_batch

    @pl.when(pl.program_id(0) == 0)
    def _():
        state_ref[...] = jnp.zeros_like(state_ref)
        carry_ref[...] = jnp.zeros_like(carry_ref)
        for ref in stash_refs:
            ref[...] = jnp.zeros_like(ref)

    head_mask = _head_mask()
    bdiag = lambda t: _head_blockdiag(t, head_mask)
    ii = _iota2((C, W), 0)
    jj = _iota2((C, W), 1) & (C - 1)
    strict = ii > jj
    incl = ii >= jj
    tril = (_iota2((C, C), 0) >= _iota2((C, C), 1)).astype(BF16)
    lane_head = _iota2((HEAD_DIM, W), 1) >> int(math.log2(HEAD_DIM))
    prepared = {}

    def prepare():
        post = []
        pre = []
        for b in range(n_batch):
            cur = c_ref[b]
            prev = jnp.where(_iota2(cur.shape, 0) == 0, carry_ref[b, 0:1, :],
                             pltpu.roll(cur, 1, 0))
            carry_ref[b, 0:1, :] = cur[tb - 1:tb, :]
            mixed = cur + (prev - cur) * mu_ref[...]
            r = mixed[:, 0:W]
            k = mixed[:, W:2 * W]
            v = mixed[:, 2 * W:3 * W]
            xw = mixed[:, 3 * W:3 * W + DECAY_LORA]
            xa = mixed[:, 3 * W + DECAY_LORA:3 * W + DECAY_LORA + AAA_LORA]
            xg = mixed[:, 3 * W + DECAY_LORA + AAA_LORA:]

            log_w = -jnp.exp(_log_sigmoid(w0_ref[...] + _mm(jnp.tanh(xw), w2_ref[...])) - 0.5)
            a = _sigmoid(a0_ref[...] + _mm(xa, a2_ref[...]))
            g = _mm(_sigmoid(xg), g2_ref[...])
            kk = k * kk_ref[...]
            k2 = k * (1.0 + (a - 1.0) * ka_ref[...])
            kk = kk / jnp.maximum(jnp.sqrt(_head_sum(kk * kk, head_mask)), 1e-12)
            a_vec = -kk
            b_vec = kk * a
            post.append((g, _head_sum(r * k2 * rk_ref[...], head_mask) * v))
            yield

            for c in range(n_chunks):
                rows = slice(c * C, (c + 1) * C)
                lw = log_w[rows]
                cum = _mask_mm_f32(tril, lw)
                e_pos = jnp.exp(cum)
                e_neg = jnp.exp(-cum)
                p_end = e_pos[C - 1:C, :]
                at = a_vec[rows] * jnp.exp(cum - lw)
                rt = r[rows] * e_pos
                bt = b_vec[rows] * e_neg
                kt = k2[rows] * e_neg
                ar = jnp.concatenate([at, rt], axis=0)
                gb = _bmm_nt(ar, bdiag(bt))
                gk = _bmm_nt(ar, bdiag(kt))
                vc = v[rows]
                pre.append(dict(
                    ar=ar, p_end=p_end, vc=vc, bd_v=bdiag(vc),
                    a_ab=jnp.where(strict, gb[0:C], 0.0), a_ak=jnp.where(strict, gk[0:C], 0.0),
                    a_r=jnp.concatenate([jnp.where(incl, gb[C:], 0.0),
                                         jnp.where(incl, gk[C:], 0.0)], axis=1),
                    bk_end=jnp.concatenate([bt * p_end, kt * p_end], axis=0)))
                yield
        t_inv = yield from _unit_lower_inverse([p["a_ab"] for p in pre], head_mask)
        for p, t in zip(pre, t_inv):
            p["t_inv"] = t
            p["akv"] = _bmm(p["a_ak"], p["bd_v"])
        prepared["pre"] = pre
        prepared["post"] = post

    def recur():
        s_all = [state_ref[b] for b in range(n_batch)]
        batches = range(n_batch)
        for c in range(n_chunks):
            ent = [rd_chunk + b * n_chunks + c for b in batches]
            ars = [_bmm_nt(stash["ar"][e], bdiag(s_all[b])) for b, e in zip(batches, ent)]
            yield
            u = [_bmm(stash["t_inv"][e], bdiag(ars[b][0:C] + stash["akv"][e]))
                 for b, e in zip(batches, ent)]
            yield
            for b, e in zip(batches, ent):
                uv = jnp.concatenate([u[b].astype(BF16), stash["vc"][e]], axis=0)
                y_ref[b, c * C:(c + 1) * C, :] = ars[b][C:] + jnp.dot(
                    stash["a_r"][e], jnp.concatenate([bdiag(u[b]), stash["bd_v"][e]], axis=0),
                    preferred_element_type=F32)
                full = _mm_tn(uv, stash["bk_end"][e])
                upd = full[0:HEAD_DIM]
                for h in range(1, N_HEADS):
                    upd = jnp.where(lane_head == h, full[h * HEAD_DIM:(h + 1) * HEAD_DIM], upd)
                s_all[b] = s_all[b] * stash["p_end"][e] + upd
            yield
        for b in range(n_batch):
            state_ref[b] = s_all[b]
            y = y_ref[b]
            yc = y - _head_sum(y, head_mask) * (1.0 / HEAD_DIM)
            var = _head_sum(yc * yc, head_mask) * (1.0 / HEAD_DIM)
            yn = yc * lax.rsqrt(var + RWKV_GN_EPS) * gn_ref[...]
            o_ref[b] = ((yn + stash["bonus"][rd_batch + b])
                        * stash["g"][rd_batch + b]).astype(o_ref.dtype)
            yield

    _interleave(prepare(), recur())

    for e, p in enumerate(prepared["pre"]):
        for name, _, dtype, per_chunk in RWKV_STASH:
            if per_chunk:
                stash[name][wr_chunk + e] = p[name].astype(dtype)
    for b, (g, bonus) in enumerate(prepared["post"]):
        stash["g"][wr_batch + b] = g
        stash["bonus"][wr_batch + b] = bonus


def _rwkv(rwkv_c, mu, w0, w2, a0, a2, g2, k_k, k_a, r_k, gn, tb=256):
    B, S, W = rwkv_c.shape
    n_blocks = S // tb
    n_entries = B * (tb // RWKV_CHUNK)
    vec = lambda t: _resident((1, t.shape[-1]))
    mat = lambda t: _resident(t.shape)
    stash_shapes = [pltpu.VMEM((2 * (n_entries if per_chunk else B),) + shape(RWKV_CHUNK, tb), dtype)
                    for _, shape, dtype, per_chunk in RWKV_STASH]
    return pl.pallas_call(
        functools.partial(_rwkv_kernel, tb=tb),
        out_shape=jax.ShapeDtypeStruct((B, S, MIX_W), BF16),
        grid=(n_blocks + 1,),
        in_specs=[pl.BlockSpec((B, tb, W), lambda i: (0, jnp.minimum(i, n_blocks - 1), 0)),
                  vec(mu), vec(w0), mat(w2), vec(a0), mat(a2), mat(g2), vec(k_k), vec(k_a),
                  vec(r_k), vec(gn)],
        out_specs=pl.BlockSpec((B, tb, MIX_W), lambda i: (0, jnp.maximum(i - 1, 0), 0)),
        scratch_shapes=[pltpu.VMEM((B, HEAD_DIM, MIX_W), F32),
                        pltpu.VMEM((B, 8, W), F32),
                        pltpu.VMEM((B, tb, MIX_W), F32)] + stash_shapes,
        compiler_params=_cparams(("arbitrary",)),
        name="rwkv7",
    )(rwkv_c, mu, w0, w2, a0, a2, g2, k_k, k_a, r_k, gn)


def _merge_ffn_kernel(x_ref, gate_ref, yret_ref, yfox_ref, yrwkv_ref, pret_ref, pfox_ref, prwkv_ref,
                      wout_ref, g_ref, wgu_ref, wd_ref, o_ref, *, d_model, d_ff, ck):
    D = d_model
    dot = lambda a, b: jnp.dot(a, b, preferred_element_type=F32)
    gates = gate_ref[...]
    fox = lax.dot_general(yfox_ref[...], pfox_ref[...], (((0,), (0,)), ((), ())),
                          preferred_element_type=F32)
    merged = (_sigmoid(gates[:, 0:D]) * dot(yret_ref[...], pret_ref[...])
              + _sigmoid(gates[:, D:2 * D]) * fox
              + _sigmoid(gates[:, 2 * D:3 * D]) * dot(yrwkv_ref[...], prwkv_ref[...]))
    x = x_ref[...] + dot(merged.astype(BF16), wout_ref[...])

    h = (x * lax.rsqrt(jnp.mean(x * x, axis=-1, keepdims=True) + NORM_EPS) * g_ref[...]).astype(BF16)
    acc = x
    for c in range(d_ff // ck):
        lo = c * ck
        gate = dot(h, wgu_ref[:, lo:lo + ck])
        up = dot(h, wgu_ref[:, d_ff + lo:d_ff + lo + ck])
        act = (gate * _sigmoid(gate) * up).astype(BF16)
        acc = acc + dot(act, wd_ref[lo:lo + ck, :])
    o_ref[...] = acc


def _merge_ffn(x, gates, y_ret, y_fox, y_rwkv, p_ret, p_fox, p_rwkv, w_out, gain, w_gate_up, w_down,
               tm=512, ck=256):
    B, S, D = x.shape
    d_ff = w_down.shape[0]
    row = lambda w: pl.BlockSpec((None, tm, w), lambda b, i: (b, i, 0))
    weights = (p_ret, p_fox, p_rwkv, w_out, gain, w_gate_up, w_down)
    return pl.pallas_call(
        functools.partial(_merge_ffn_kernel, d_model=D, d_ff=d_ff, ck=ck),
        out_shape=jax.ShapeDtypeStruct((B, S, D), F32),
        grid=(B, S // tm),
        in_specs=[row(D), row(3 * D), row(MIX_W),
                  pl.BlockSpec((None, MIX_W, tm), lambda b, i: (b, 0, i)),
                  row(MIX_W)] + [_resident(w.shape) for w in weights],
        out_specs=row(D),
        compiler_params=_cparams(("parallel", "parallel")),
        name="merge_ffn",
    )(x, gates, y_ret, y_fox, y_rwkv, *weights)


def kernel(x, mix_norm, w_in, ret_gn, fox_q_norm, fox_k_norm, fox_f_bias, rwkv_mu, rwkv_w0, rwkv_w2,
           rwkv_a0, rwkv_a2, rwkv_g2, rwkv_k_k, rwkv_k_a, rwkv_r_k, rwkv_gn, p_ret, p_fox, p_rwkv,
           w_out, ffn_norm, w_gate_up, w_down):
    B, S, D = x.shape
    depth = w_in.shape[0]
    cos, sin = _rope_tables(S)
    o0, o1, o2, o3 = (int(v) for v in IN_OFF)
    f_lo = o1 + FOXQKV_COLS
    row = lambda t: t.reshape(1, -1)
    for l in range(depth):
        w_cat = _w_in_relayout(w_in[l], f_lo, N_HEADS)
        ret_c, fox_c, rwkv_c, gate_c, f_c = _in_proj(x, row(mix_norm[l]), w_cat)

        y_ret = _retention(ret_c, cos, sin, row(ret_gn[l]))

        f_bias = jnp.pad(jnp.tile(fox_f_bias[l], FOX_SPLIT),
                         (0, LANES - FOX_SPLIT * N_HEADS)).reshape(1, LANES)
        fq, fk, fv = _fox_prep(fox_c, f_c, f_bias, jnp.tile(row(fox_q_norm[l]), (1, N_HEADS)),
                               jnp.tile(row(fox_k_norm[l]), (1, N_HEADS)))
        y_fox = _fox_attn(fq, fk, fv).reshape(B, MIX_W, S)

        y_rwkv = _rwkv(rwkv_c, row(rwkv_mu[l]), row(rwkv_w0[l]), rwkv_w2[l].astype(BF16),
                       row(rwkv_a0[l]), rwkv_a2[l].astype(BF16), rwkv_g2[l].astype(BF16),
                       row(rwkv_k_k[l]), row(rwkv_k_a[l]), row(rwkv_r_k[l]), row(rwkv_gn[l]))

        x = _merge_ffn(x, gate_c, y_ret, y_fox, y_rwkv, p_ret[l].astype(BF16), p_fox[l].astype(BF16),
                       p_rwkv[l].astype(BF16), w_out[l].astype(BF16), row(ffn_norm[l]),
                       w_gate_up[l].astype(BF16), w_down[l].astype(BF16))
    return x
```

```python
import functools
import math

import jax
import jax.numpy as jnp
import numpy as np
from jax import lax
from jax.experimental import pallas as pl
from jax.experimental.pallas import tpu as pltpu

F32 = jnp.float32
BF16 = jnp.bfloat16

HEAD_DIM = 64
N_HEADS = 4
MIX_W = N_HEADS * HEAD_DIM
RET_CHUNK = 128
ROPE_BASE = 10000.0
DECAY_LORA = 64
AAA_LORA = 64
GATE_LORA = 128
NORM_EPS = 1e-6
RET_GN_EPS = 1e-5
RWKV_GN_EPS = 64e-5
RWKV_CHUNK = 64
RWKV_SUB = 16
LANES = 128
FOX_SPLIT = 3
VMEM_LIMIT = 52 * 1024 * 1024


def _cparams(sem):
    return pltpu.CompilerParams(dimension_semantics=sem, vmem_limit_bytes=VMEM_LIMIT)


def _resident(shape, layer=None):
    nd = len(shape)
    if layer is None:
        return pl.BlockSpec(shape, lambda *_: (0,) * nd, pipeline_mode=pl.Buffered(1))
    return pl.BlockSpec((None,) + tuple(shape), lambda *_: (layer,) + (0,) * nd,
                        pipeline_mode=pl.Buffered(1))


def _mm(a, b):
    return jnp.dot(a.astype(BF16), b.astype(BF16), preferred_element_type=F32)


def _mm_nt(a, b):
    return lax.dot_general(a.astype(BF16), b.astype(BF16), (((1,), (1,)), ((), ())),
                           preferred_element_type=F32)


def _mm_tn(a, b):
    return lax.dot_general(a.astype(BF16), b.astype(BF16), (((0,), (0,)), ((), ())),
                           preferred_element_type=F32)


def _mask_mm_f32(mask_bf16, x):
    h1 = x.astype(BF16)
    r1 = x - h1.astype(F32)
    h2 = r1.astype(BF16)
    h3 = (r1 - h2.astype(F32)).astype(BF16)
    dot = lambda h: jnp.dot(mask_bf16, h, preferred_element_type=F32)
    return dot(h1) + dot(h2) + dot(h3)


def _sigmoid(x):
    return 0.5 * jnp.tanh(0.5 * x) + 0.5


def _log_sigmoid(x):
    return jnp.minimum(x, 0.0) - jnp.log1p(jnp.exp(-jnp.abs(x)))


def _iota2(shape, dim):
    return lax.broadcasted_iota(jnp.int32, shape, dim)


def _head_mask():
    same = (_iota2((MIX_W, MIX_W), 0) >> 6) == (_iota2((MIX_W, MIX_W), 1) >> 6)
    return jnp.where(same, 1.0, 0.0).astype(BF16)


def _head_blockdiag(x, head_mask):
    xb = x.astype(BF16)
    return jnp.concatenate([xb] * N_HEADS, axis=0) * head_mask


def _head_sum(x, head_mask):
    hi = x.astype(BF16)
    lo = (x - hi.astype(F32)).astype(BF16)
    dot = lambda t: jnp.dot(t, head_mask, preferred_element_type=F32)
    return dot(hi) + dot(lo)


def _bmm(a, bd):
    return jnp.dot(a.astype(BF16), bd, preferred_element_type=F32)


def _bmm_nt(a, bd):
    return lax.dot_general(a.astype(BF16), bd, (((1,), (1,)), ((), ())), preferred_element_type=F32)


RET_COLS = 4 * MIX_W
FOXQKV_COLS = 3 * MIX_W
RWKV_COLS = 3 * MIX_W + DECAY_LORA + AAA_LORA + GATE_LORA
IN_OFF = np.cumsum([0, RET_COLS, FOXQKV_COLS, RWKV_COLS])


def _w_in_relayout_kernel(w_ref, o_ref, *, f_lo, n_f):
    ncol = w_ref.shape[1]
    rest = ncol - (f_lo + n_f)
    o_ref[:, 0:f_lo] = w_ref[:, 0:f_lo].astype(BF16)
    o_ref[:, f_lo:f_lo + rest] = w_ref[:, f_lo + n_f:ncol].astype(BF16)
    win = w_ref[:, f_lo:f_lo + LANES]
    lane = _iota2(win.shape, 1)
    reps = jnp.where(lane < n_f, win, 0.0)
    for t in range(1, FOX_SPLIT):
        reps = jnp.where((lane >= t * n_f) & (lane < (t + 1) * n_f), pltpu.roll(win, t * n_f, 1), reps)
    o_ref[:, f_lo + rest:] = reps.astype(BF16)


def _w_in_relayout(w, layer, f_lo, n_f, tr=128):
    _, D, ncol = w.shape
    out_cols = ncol - n_f + LANES
    return pl.pallas_call(
        functools.partial(_w_in_relayout_kernel, f_lo=f_lo, n_f=n_f),
        out_shape=jax.ShapeDtypeStruct((D, out_cols), BF16),
        grid=(D // tr,),
        in_specs=[pl.BlockSpec((None, tr, ncol), lambda i: (layer, i, 0))],
        out_specs=pl.BlockSpec((tr, out_cols), lambda i: (i, 0)),
        compiler_params=_cparams(("parallel",)),
        name="w_in_relayout",
    )(w)


def _in_proj_kernel(x_ref, g_ref, w_ref, ret_ref, fox_ref, rwkv_ref, gate_ref, f_ref, *, d_model):
    x = x_ref[...]
    h = x * lax.rsqrt(jnp.mean(x * x, axis=-1, keepdims=True) + NORM_EPS) * g_ref[...]
    hb = h.astype(BF16)
    o0, o1, o2, o3 = (int(v) for v in IN_OFF)
    o4 = o3 + 3 * d_model
    dot = lambda lo, hi: jnp.dot(hb, w_ref[:, lo:hi], preferred_element_type=F32)
    ret_ref[...] = dot(o0, o1)
    fox_ref[...] = dot(o1, o2)
    rwkv_ref[...] = dot(o2, o3)
    gate_ref[...] = dot(o3, o4)
    f_ref[...] = dot(o4, o4 + LANES)


def _in_proj(x, gain, w_cat, tm=256):
    B, S, D = x.shape
    ncol = w_cat.shape[1]
    row = lambda w: pl.BlockSpec((None, tm, w), lambda b, i: (b, i, 0))
    widths = (RET_COLS, FOXQKV_COLS, RWKV_COLS, 3 * D, LANES)
    return pl.pallas_call(
        functools.partial(_in_proj_kernel, d_model=D),
        out_shape=tuple(jax.ShapeDtypeStruct((B, S, w), F32) for w in widths),
        grid=(B, S // tm),
        in_specs=[row(D), _resident((1, D)), _resident((D, ncol))],
        out_specs=tuple(row(w) for w in widths),
        compiler_params=_cparams(("parallel", "parallel")),
        name="in_proj",
    )(x, gain, w_cat)


def _retention_kernel(c_ref, cos_ref, sin_ref, gn_ref, dmat_ref, qdec_ref, kdec_ref, cdec_ref, o_ref,
                      state_ref, *, tb):
    C = RET_CHUNK
    n_batch = c_ref.shape[0]

    @pl.when(pl.program_id(0) == 0)
    def _():
        state_ref[...] = jnp.zeros_like(state_ref)

    head_mask = _head_mask()
    first_half = (_iota2((tb, MIX_W), 1) & (HEAD_DIM - 1)) < HEAD_DIM // 2
    kv_mask = jnp.where((_iota2((N_HEADS * C, MIX_W), 0) >> int(math.log2(C)))
                        == (_iota2((N_HEADS * C, MIX_W), 1) >> int(math.log2(HEAD_DIM))),
                        1.0, 0.0).astype(BF16)
    lane_head = _iota2((HEAD_DIM, MIX_W), 1) >> int(math.log2(HEAD_DIM))
    cos = jnp.concatenate([cos_ref[...]] * (MIX_W // LANES), axis=1)
    sin = jnp.concatenate([sin_ref[...]] * (MIX_W // LANES), axis=1)

    def rope(t):
        partner = jnp.where(first_half, pltpu.roll(t, MIX_W - HEAD_DIM // 2, 1),
                            pltpu.roll(t, HEAD_DIM // 2, 1))
        return t * cos + partner * sin

    def stack_heads(t):
        return jnp.concatenate([t.astype(BF16)] * N_HEADS, axis=0) * kv_mask

    for b in range(n_batch):
        blk = c_ref[b]
        q = rope(blk[:, 0:MIX_W])
        k = rope(blk[:, MIX_W:2 * MIX_W]) * (HEAD_DIM ** -0.5)
        v = blk[:, 2 * MIX_W:3 * MIX_W]
        g = blk[:, 3 * MIX_W:4 * MIX_W]
        state = state_ref[b]
        ys = []
        for c in range(tb // C):
            rows = slice(c * C, (c + 1) * C)
            qc, kc, vc = q[rows], k[rows], v[rows]
            scores = _bmm_nt(qc, stack_heads(kc)) * dmat_ref[...]
            inner = _bmm(scores, stack_heads(vc))
            cross = _bmm(qc * qdec_ref[...], _head_blockdiag(state, head_mask))
            ys.append(inner + cross)
            full = _mm_tn(kc * kdec_ref[...], vc)
            upd = full[0:HEAD_DIM]
            for h in range(1, N_HEADS):
                upd = jnp.where(lane_head == h, full[h * HEAD_DIM:(h + 1) * HEAD_DIM], upd)
            state = state * cdec_ref[...] + upd
        state_ref[b] = state
        y = jnp.concatenate(ys, axis=0)
        yc = y - _head_sum(y, head_mask) * (1.0 / HEAD_DIM)
        var = _head_sum(yc * yc, head_mask) * (1.0 / HEAD_DIM)
        yn = yc * lax.rsqrt(var + RET_GN_EPS)
        o_ref[b] = (g * _sigmoid(g) * (yn * gn_ref[...])).astype(o_ref.dtype)


def _retention_tables():
    C = RET_CHUNK
    log_gamma = jnp.log1p(-jnp.exp2(-5.0 - jnp.arange(N_HEADS, dtype=F32)))
    idx = jnp.arange(C, dtype=F32)
    dist = idx[:, None] - idx[None, :]
    dmat = jnp.where(dist >= 0, jnp.exp(log_gamma[:, None, None] * jnp.maximum(dist, 0.0)), 0.0)
    dmat = dmat.transpose(1, 0, 2).reshape(C, N_HEADS * C)
    lanes = lambda t: jnp.repeat(t, HEAD_DIM, axis=-1)
    q_dec = lanes(jnp.exp(log_gamma[None, :] * (idx[:, None] + 1.0)))
    k_dec = lanes(jnp.exp(log_gamma[None, :] * (C - 1.0 - idx[:, None])))
    c_dec = lanes(jnp.exp(log_gamma * C)[None, :])
    return dmat, q_dec, k_dec, c_dec


def _retention(ret_c, cos, sin, gn, tb=256):
    B, S, W = ret_c.shape
    tables = _retention_tables()
    return pl.pallas_call(
        functools.partial(_retention_kernel, tb=tb),
        out_shape=jax.ShapeDtypeStruct((B, S, MIX_W), BF16),
        grid=(S // tb,),
        in_specs=[pl.BlockSpec((B, tb, W), lambda i: (0, i, 0)),
                  pl.BlockSpec((tb, LANES), lambda i: (i, 0)),
                  pl.BlockSpec((tb, LANES), lambda i: (i, 0)),
                  _resident((1, MIX_W))] + [_resident(t.shape) for t in tables],
        out_specs=pl.BlockSpec((B, tb, MIX_W), lambda i: (0, i, 0)),
        scratch_shapes=[pltpu.VMEM((B, HEAD_DIM, MIX_W), F32)],
        compiler_params=_cparams(("arbitrary",)),
        name="retention",
    )(ret_c, cos, sin, gn, *tables)


def _rope_tables(S):
    half = HEAD_DIM // 2
    inv_freq = ROPE_BASE ** (-jnp.arange(half, dtype=F32) / half)
    ang = jnp.arange(S, dtype=F32)[:, None] * inv_freq[None, :]
    cos, sin = jnp.cos(ang), jnp.sin(ang)
    cos_h = jnp.concatenate([cos, cos], axis=-1)
    sin_h = jnp.concatenate([-sin, sin], axis=-1)
    reps = LANES // HEAD_DIM
    return jnp.tile(cos_h, (1, reps)), jnp.tile(sin_h, (1, reps))


LOG2E = math.log2(math.e)
FOX_AUG = HEAD_DIM
FOX_TQ = 512
FOX_TK = 256
FOX_UNROLL = 4


def _fox_place_tables():
    H, Dh, L = N_HEADS, HEAD_DIM, LANES
    place_q = np.zeros((H // 2, MIX_W + L, 2 * L), np.float32)
    place_k = np.zeros((H // 2, MIX_W + L, 2 * L), np.float32)
    place_vt = np.zeros((H, L, MIX_W), np.float32)
    for h in range(H):
        out = (h % 2) * L
        for d in range(Dh):
            place_q[h // 2, h * Dh + d, out + d] = 1.0
            place_k[h // 2, h * Dh + d, out + d] = 1.0
            place_vt[h, d, h * Dh + d] = 1.0
        for t in range(FOX_SPLIT):
            place_q[h // 2, MIX_W + t * H + h, out + FOX_AUG + t] = 1.0
            place_k[h // 2, MIX_W + t * H + h, out + FOX_AUG + FOX_SPLIT + t] = -1.0
    return tuple(jnp.asarray(t, BF16) for t in (place_q, place_k, place_vt))


def _fox_prep_kernel(c_ref, f_ref, bias_ref, qg_ref, kg_ref, pq_ref, pk_ref, pvt_ref, q_ref, k_ref,
                     vt_ref, carry_ref, *, ts, tk):
    @pl.when(pl.program_id(1) == 0)
    def _():
        carry_ref[...] = jnp.zeros_like(carry_ref)

    log_f = _log_sigmoid(f_ref[...] + bias_ref[...])
    tril = (_iota2((ts, ts), 0) >= _iota2((ts, ts), 1)).astype(BF16)
    c = _mask_mm_f32(tril, log_f) + carry_ref[0:1, :]
    carry_ref[0:1, :] = c[ts - 1:ts, :]
    c = c * LOG2E
    lane_c = _iota2(c.shape, 1)
    csel = jnp.zeros(c.shape, BF16)
    rest = c
    for t in range(FOX_SPLIT):
        term = rest.astype(BF16)
        csel = jnp.where((lane_c >= t * N_HEADS) & (lane_c < (t + 1) * N_HEADS), term, csel)
        rest = rest - term.astype(F32)

    blk = c_ref[...]
    head_mask = _head_mask()

    def rms(t, gain):
        ms = _head_sum(t * t, head_mask) * (1.0 / HEAD_DIM)
        return t * lax.rsqrt(ms + NORM_EPS) * gain

    qn = rms(blk[:, 0:MIX_W], qg_ref[...]) * (LOG2E * HEAD_DIM ** -0.5)
    kn = rms(blk[:, MIX_W:2 * MIX_W], kg_ref[...])
    q_stack = jnp.concatenate([qn.astype(BF16), csel], axis=1)
    k_stack = jnp.concatenate([kn.astype(BF16), csel], axis=1)
    vb = blk[:, 2 * MIX_W:3 * MIX_W].astype(BF16)

    lane = _iota2((1, 2 * LANES), 1) & (LANES - 1)
    ones_q = jnp.where((lane >= FOX_AUG + FOX_SPLIT) & (lane < FOX_AUG + 2 * FOX_SPLIT), 1.0, 0.0)
    ones_k = jnp.where((lane >= FOX_AUG) & (lane < FOX_AUG + FOX_SPLIT), 1.0, 0.0)
    ones_row = _iota2((LANES, ts), 0) == FOX_AUG

    for hp in range(N_HEADS // 2):
        q2 = (jnp.dot(q_stack, pq_ref[hp], preferred_element_type=F32) + ones_q).astype(BF16)
        k2 = (jnp.dot(k_stack, pk_ref[hp], preferred_element_type=F32) + ones_k).astype(BF16)
        for sub in range(2):
            q_ref[2 * hp + sub] = q2[:, sub * LANES:(sub + 1) * LANES]
            k_ref[2 * hp + sub] = k2[:, sub * LANES:(sub + 1) * LANES]
    for h in range(N_HEADS):
        v_t = jnp.where(ones_row, 1.0, _mm_nt(pvt_ref[h], vb))
        for kb in range(ts // tk):
            vt_ref[h, kb] = v_t[:, kb * tk:(kb + 1) * tk].astype(BF16)


def _fox_prep(fox_c, f_c, f_bias, q_gain, k_gain, ts=512):
    B, S, W = fox_c.shape
    tk = FOX_TK
    tables = _fox_place_tables()
    head_spec = pl.BlockSpec((None, N_HEADS, ts, LANES), lambda b, i: (b, 0, i, 0))
    shp = jax.ShapeDtypeStruct((B, N_HEADS, S, LANES), BF16)
    return pl.pallas_call(
        functools.partial(_fox_prep_kernel, ts=ts, tk=tk),
        out_shape=(shp, shp, jax.ShapeDtypeStruct((B, N_HEADS, S // tk, LANES, tk), BF16)),
        grid=(B, S // ts),
        in_specs=[pl.BlockSpec((None, ts, W), lambda b, i: (b, i, 0)),
                  pl.BlockSpec((None, ts, LANES), lambda b, i: (b, i, 0)),
                  _resident((1, LANES)), _resident((1, MIX_W)), _resident((1, MIX_W))]
                 + [_resident(t.shape) for t in tables],
        out_specs=(head_spec, head_spec,
                   pl.BlockSpec((None, N_HEADS, ts // tk, LANES, tk), lambda b, i: (b, 0, i, 0, 0))),
        scratch_shapes=[pltpu.VMEM((8, LANES), F32)],
        compiler_params=_cparams(("parallel", "arbitrary")),
        name="fox_prep",
    )(fox_c, f_c, f_bias, q_gain, k_gain, *tables)


def _fox_attn_kernel(q_ref, k_ref, vt_ref, o_ref, s0_ref, s1_ref, p0_ref, p1_ref, m_ref, alpha_ref,
                     acc_ref, *, tq, tk):
    qi = pl.program_id(2)
    m_ref[...] = jnp.full(m_ref.shape, -jnp.inf, F32)
    acc_ref[...] = jnp.zeros_like(acc_ref)
    p1_ref[...] = jnp.zeros_like(p1_ref)

    def qk(j, s_ref):
        start = pl.multiple_of(j * tk, tk)
        s_ref[...] = _mm_nt(k_ref[pl.ds(start, tk), :], q_ref[...])

    def pv(j, p_ref):
        return jnp.dot(vt_ref[j], p_ref[...], preferred_element_type=F32)

    def softmax(s_ref, p_ref, key_offset):
        for c in range(0, tq, LANES):
            cols = slice(c, c + LANES)
            s = s_ref[:, cols]
            if key_offset is not None and key_offset + tk - 1 > c:
                keep = _iota2((tk, LANES), 0) + key_offset <= _iota2((tk, LANES), 1) + c
                s = jnp.where(keep, s, -jnp.inf)
            m_old = m_ref[:, cols]
            m_new = jnp.maximum(m_old, jnp.max(s, axis=0, keepdims=True))
            p_ref[:, cols] = jnp.exp2(s - m_new).astype(BF16)
            m_ref[:, cols] = m_new
            alpha_ref[:, cols] = jnp.exp2(m_old - m_new)

    def advance(pv_prev):
        acc_ref[...] = (acc_ref[...] + pv_prev) * alpha_ref[...]

    qk(0, s0_ref)

    def pair(i):
        j = 2 * i
        qk(j + 1, s1_ref)
        pv_prev = pv(jnp.maximum(j - 1, 0), p1_ref)
        softmax(s0_ref, p0_ref, None)
        advance(pv_prev)
        qk(j + 2, s0_ref)
        pv_prev = pv(j, p0_ref)
        softmax(s1_ref, p1_ref, None)
        advance(pv_prev)

    def unrolled(i, carry):
        for u in range(FOX_UNROLL):
            pair(FOX_UNROLL * i + u)
        return carry

    def single(i, carry):
        pair(i)
        return carry

    n_main = qi // FOX_UNROLL
    lax.fori_loop(0, n_main, unrolled, 0)
    lax.fori_loop(n_main * FOX_UNROLL, qi, single, 0)

    j = 2 * qi
    qk(j + 1, s1_ref)
    pv_prev = pv(jnp.maximum(j - 1, 0), p1_ref)
    softmax(s0_ref, p0_ref, 0)
    advance(pv_prev)
    pv_prev = pv(j, p0_ref)
    softmax(s1_ref, p1_ref, tk)
    advance(pv_prev)
    acc = acc_ref[...] + pv(j + 1, p1_ref)
    o_ref[...] = (acc[0:HEAD_DIM] / acc[FOX_AUG:FOX_AUG + 1]).astype(o_ref.dtype)


def _fox_attn(q, k, vt):
    B, H, S, W = q.shape
    tq, tk = FOX_TQ, FOX_TK
    assert tq == 2 * tk and S % tq == 0
    return pl.pallas_call(
        functools.partial(_fox_attn_kernel, tq=tq, tk=tk),
        out_shape=jax.ShapeDtypeStruct((B, H, HEAD_DIM, S), BF16),
        grid=(B, H, S // tq),
        in_specs=[pl.BlockSpec((None, None, tq, W), lambda b, h, i: (b, h, i, 0)),
                  pl.BlockSpec((None, None, S, W), lambda b, h, i: (b, h, 0, 0)),
                  pl.BlockSpec((None, None, S // tk, W, tk), lambda b, h, i: (b, h, 0, 0, 0))],
        out_specs=pl.BlockSpec((None, None, HEAD_DIM, tq), lambda b, h, i: (b, h, 0, i)),
        scratch_shapes=[pltpu.VMEM((tk, tq), F32), pltpu.VMEM((tk, tq), F32),
                        pltpu.VMEM((tk, tq), BF16), pltpu.VMEM((tk, tq), BF16),
                        pltpu.VMEM((1, tq), F32), pltpu.VMEM((1, tq), F32),
                        pltpu.VMEM((W, tq), F32)],
        compiler_params=_cparams(("parallel", "parallel", "arbitrary")),
        name="fox_attn",
    )(q, k, vt)


def _unit_lower_inverse(a_list, head_mask):
    C, W = a_list[0].shape
    bdiag = lambda t: _head_blockdiag(t, head_mask)
    ii = _iota2((C, W), 0)
    jj = _iota2((C, W), 1) & (C - 1)
    shift = int(math.log2(RWKV_SUB))
    same_blk = (ii >> shift) == (jj >> shift)
    eye = (ii == jj).astype(F32)
    d = [jnp.where(same_blk, a, 0.0) for a in a_list]
    low = [a - di for a, di in zip(a_list, d)]
    x = [eye + di for di in d]
    dp = d
    bd = [bdiag(t) for t in dp]
    for _ in range(shift - 1):
        dp = [_bmm(t, b) for t, b in zip(dp, bd)]
        yield
        bd = [bdiag(t) for t in dp]
        x = [xi + _bmm(xi, b) for xi, b in zip(x, bd)]
        yield
    n = [_bmm(xi, bdiag(lo)) for xi, lo in zip(x, low)]
    yield
    y = [eye + ni for ni in n]
    npow = n
    for _ in range(int(math.log2(C // RWKV_SUB)) - 1):
        bd = [bdiag(t) for t in npow]
        npow = [_bmm(t, b) for t, b in zip(npow, bd)]
        yield
        y = [yi + _bmm(yi, bdiag(t)) for yi, t in zip(y, npow)]
        yield
    return [_bmm(yi, bdiag(xi)) for yi, xi in zip(y, x)]


RWKV_STASH = (
    ("ar", lambda C, tb: (2 * C, MIX_W), BF16, True),
    ("vc", lambda C, tb: (C, MIX_W), BF16, True),
    ("bd_v", lambda C, tb: (MIX_W, MIX_W), BF16, True),
    ("a_r", lambda C, tb: (C, 2 * MIX_W), BF16, True),
    ("bk_end", lambda C, tb: (2 * C, MIX_W), BF16, True),
    ("t_inv", lambda C, tb: (C, MIX_W), BF16, True),
    ("akv", lambda C, tb: (C, MIX_W), F32, True),
    ("p_end", lambda C, tb: (1, MIX_W), F32, True),
    ("g", lambda C, tb: (tb, MIX_W), F32, False),
    ("bonus", lambda C, tb: (tb, MIX_W), F32, False),
)


def _interleave(*generators):
    alive = list(generators)
    while alive:
        for gen in list(alive):
            try:
                next(gen)
            except StopIteration:
                alive.remove(gen)


def _rwkv_kernel(c_ref, mu_ref, w0_ref, w2_ref, a0_ref, a2_ref, g2_ref, kk_ref, ka_ref, rk_ref,
                 gn_ref, o_ref, state_ref, carry_ref, y_ref, *stash_refs, tb):
    C = RWKV_CHUNK
    W = MIX_W
    n_batch = c_ref.shape[0]
    n_chunks = tb // C
    stash = {name: ref for (name, _, _, _), ref in zip(RWKV_STASH, stash_refs)}
    step = pl.program_id(0)
    n_entries = n_batch * n_chunks
    wr_chunk = (step % 2) * n_entries
    rd_chunk = ((step + 1) % 2) * n_entries
    wr_batch = (step % 2) * n_batch
    rd_batch = ((step + 1) % 2) * n_batch

    @pl.when(pl.program_id(0) == 0)
    def _():
        state_ref[...] = jnp.zeros_like(state_ref)
        carry_ref[...] = jnp.zeros_like(carry_ref)
        for ref in stash_refs:
            ref[...] = jnp.zeros_like(ref)

    head_mask = _head_mask()
    bdiag = lambda t: _head_blockdiag(t, head_mask)
    ii = _iota2((C, W), 0)
    jj = _iota2((C, W), 1) & (C - 1)
    strict = ii > jj
    incl = ii >= jj
    tril = (_iota2((C, C), 0) >= _iota2((C, C), 1)).astype(BF16)
    lane_head = _iota2((HEAD_DIM, W), 1) >> int(math.log2(HEAD_DIM))
    prepared = {}

    def prepare():
        post = []
        pre = []
        for b in range(n_batch):
            cur = c_ref[b]
            prev = jnp.where(_iota2(cur.shape, 0) == 0, carry_ref[b, 0:1, :],
                             pltpu.roll(cur, 1, 0))
            carry_ref[b, 0:1, :] = cur[tb - 1:tb, :]
            mixed = cur + (prev - cur) * mu_ref[...]
            r = mixed[:, 0:W]
            k = mixed[:, W:2 * W]
            v = mixed[:, 2 * W:3 * W]
            xw = mixed[:, 3 * W:3 * W + DECAY_LORA]
            xa = mixed[:, 3 * W + DECAY_LORA:3 * W + DECAY_LORA + AAA_LORA]
            xg = mixed[:, 3 * W + DECAY_LORA + AAA_LORA:]

            log_w = -jnp.exp(_log_sigmoid(w0_ref[...] + _mm(jnp.tanh(xw), w2_ref[...])) - 0.5)
            a = _sigmoid(a0_ref[...] + _mm(xa, a2_ref[...]))
            g = _mm(_sigmoid(xg), g2_ref[...])
            kk = k * kk_ref[...]
            k2 = k * (1.0 + (a - 1.0) * ka_ref[...])
            kk = kk / jnp.maximum(jnp.sqrt(_head_sum(kk * kk, head_mask)), 1e-12)
            a_vec = -kk
            b_vec = kk * a
            post.append((g, _head_sum(r * k2 * rk_ref[...], head_mask) * v))
            yield

            for c in range(n_chunks):
                rows = slice(c * C, (c + 1) * C)
                lw = log_w[rows]
                cum = _mask_mm_f32(tril, lw)
                e_pos = jnp.exp(cum)
                e_neg = jnp.exp(-cum)
                p_end = e_pos[C - 1:C, :]
                at = a_vec[rows] * jnp.exp(cum - lw)
                rt = r[rows] * e_pos
                bt = b_vec[rows] * e_neg
                kt = k2[rows] * e_neg
                ar = jnp.concatenate([at, rt], axis=0)
                gb = _bmm_nt(ar, bdiag(bt))
                gk = _bmm_nt(ar, bdiag(kt))
                vc = v[rows]
                pre.append(dict(
                    ar=ar, p_end=p_end, vc=vc, bd_v=bdiag(vc),
                    a_ab=jnp.where(strict, gb[0:C], 0.0), a_ak=jnp.where(strict, gk[0:C], 0.0),
                    a_r=jnp.concatenate([jnp.where(incl, gb[C:], 0.0),
                                         jnp.where(incl, gk[C:], 0.0)], axis=1),
                    bk_end=jnp.concatenate([bt * p_end, kt * p_end], axis=0)))
                yield
        t_inv = yield from _unit_lower_inverse([p["a_ab"] for p in pre], head_mask)
        for p, t in zip(pre, t_inv):
            p["t_inv"] = t
            p["akv"] = _bmm(p["a_ak"], p["bd_v"])
        prepared["pre"] = pre
        prepared["post"] = post

    def recur():
        s_all = [state_ref[b] for b in range(n_batch)]
        batches = range(n_batch)
        for c in range(n_chunks):
            ent = [rd_chunk + b * n_chunks + c for b in batches]
            ars = [_bmm_nt(stash["ar"][e], bdiag(s_all[b])) for b, e in zip(batches, ent)]
            yield
            u = [_bmm(stash["t_inv"][e], bdiag(ars[b][0:C] + stash["akv"][e]))
                 for b, e in zip(batches, ent)]
            yield
            for b, e in zip(batches, ent):
                uv = jnp.concatenate([u[b].astype(BF16), stash["vc"][e]], axis=0)
                y_ref[b, c * C:(c + 1) * C, :] = ars[b][C:] + jnp.dot(
                    stash["a_r"][e], jnp.concatenate([bdiag(u[b]), stash["bd_v"][e]], axis=0),
                    preferred_element_type=F32)
                full = _mm_tn(uv, stash["bk_end"][e])
                upd = full[0:HEAD_DIM]
                for h in range(1, N_HEADS):
                    upd = jnp.where(lane_head == h, full[h * HEAD_DIM:(h + 1) * HEAD_DIM], upd)
                s_all[b] = s_all[b] * stash["p_end"][e] + upd
            yield
        for b in range(n_batch):
            state_ref[b] = s_all[b]
            y = y_ref[b]
            yc = y - _head_sum(y, head_mask) * (1.0 / HEAD_DIM)
            var = _head_sum(yc * yc, head_mask) * (1.0 / HEAD_DIM)
            yn = yc * lax.rsqrt(var + RWKV_GN_EPS) * gn_ref[...]
            o_ref[b] = ((yn + stash["bonus"][rd_batch + b])
                        * stash["g"][rd_batch + b]).astype(o_ref.dtype)
            yield

    _interleave(prepare(), recur())

    for e, p in enumerate(prepared["pre"]):
        for name, _, dtype, per_chunk in RWKV_STASH:
            if per_chunk:
                stash[name][wr_chunk + e] = p[name].astype(dtype)
    for b, (g, bonus) in enumerate(prepared["post"]):
        stash["g"][wr_batch + b] = g
        stash["bonus"][wr_batch + b] = bonus


def _rwkv(rwkv_c, mu, w0, w2, a0, a2, g2, k_k, k_a, r_k, gn, tb=256):
    B, S, W = rwkv_c.shape
    n_blocks = S // tb
    n_entries = B * (tb // RWKV_CHUNK)
    vec = lambda t: _resident((1, t.shape[-1]))
    mat = lambda t: _resident(t.shape)
    stash_shapes = [pltpu.VMEM((2 * (n_entries if per_chunk else B),) + shape(RWKV_CHUNK, tb), dtype)
                    for _, shape, dtype, per_chunk in RWKV_STASH]
    return pl.pallas_call(
        functools.partial(_rwkv_kernel, tb=tb),
        out_shape=jax.ShapeDtypeStruct((B, S, MIX_W), BF16),
        grid=(n_blocks + 1,),
        in_specs=[pl.BlockSpec((B, tb, W), lambda i: (0, jnp.minimum(i, n_blocks - 1), 0)),
                  vec(mu), vec(w0), mat(w2), vec(a0), mat(a2), mat(g2), vec(k_k), vec(k_a),
                  vec(r_k), vec(gn)],
        out_specs=pl.BlockSpec((B, tb, MIX_W), lambda i: (0, jnp.maximum(i - 1, 0), 0)),
        scratch_shapes=[pltpu.VMEM((B, HEAD_DIM, MIX_W), F32),
                        pltpu.VMEM((B, 8, W), F32),
                        pltpu.VMEM((B, tb, MIX_W), F32)] + stash_shapes,
        compiler_params=_cparams(("arbitrary",)),
        name="rwkv7",
    )(rwkv_c, mu, w0, w2, a0, a2, g2, k_k, k_a, r_k, gn)


def _merge_ffn_kernel(x_ref, gate_ref, yret_ref, yfox_ref, yrwkv_ref, pret_ref, pfox_ref, prwkv_ref,
                      wout_ref, g_ref, wgu_ref, wd_ref, o_ref, *, d_model, d_ff, ck):
    D = d_model
    dot = lambda a, b: jnp.dot(a, b, preferred_element_type=F32)
    gates = gate_ref[...]
    fox = lax.dot_general(yfox_ref[...], pfox_ref[...], (((0,), (0,)), ((), ())),
                          preferred_element_type=F32)
    merged = (_sigmoid(gates[:, 0:D]) * dot(yret_ref[...], pret_ref[...])
              + _sigmoid(gates[:, D:2 * D]) * fox
              + _sigmoid(gates[:, 2 * D:3 * D]) * dot(yrwkv_ref[...], prwkv_ref[...]))
    x = x_ref[...] + dot(merged.astype(BF16), wout_ref[...])

    h = (x * lax.rsqrt(jnp.mean(x * x, axis=-1, keepdims=True) + NORM_EPS) * g_ref[...]).astype(BF16)
    acc = x
    for c in range(d_ff // ck):
        lo = c * ck
        gate = dot(h, wgu_ref[:, lo:lo + ck])
        up = dot(h, wgu_ref[:, d_ff + lo:d_ff + lo + ck])
        act = (gate * _sigmoid(gate) * up).astype(BF16)
        acc = acc + dot(act, wd_ref[lo:lo + ck, :])
    o_ref[...] = acc


def _merge_ffn(x, gates, y_ret, y_fox, y_rwkv, p_ret, p_fox, p_rwkv, w_out, gain, w_gate_up, w_down,
               layer, tm=512, ck=256):
    B, S, D = x.shape
    d_ff = w_down.shape[1]
    row = lambda w: pl.BlockSpec((None, tm, w), lambda b, i: (b, i, 0))
    stacked = lambda w: _resident(w.shape[1:], layer)
    weights = (p_ret, p_fox, p_rwkv, w_out, gain, w_gate_up, w_down)
    weight_specs = [stacked(p_ret), stacked(p_fox), stacked(p_rwkv), stacked(w_out),
                    _resident(gain.shape), stacked(w_gate_up), stacked(w_down)]
    return pl.pallas_call(
        functools.partial(_merge_ffn_kernel, d_model=D, d_ff=d_ff, ck=ck),
        out_shape=jax.ShapeDtypeStruct((B, S, D), F32),
        grid=(B, S // tm),
        in_specs=[row(D), row(3 * D), row(MIX_W),
                  pl.BlockSpec((None, MIX_W, tm), lambda b, i: (b, 0, i)),
                  row(MIX_W)] + weight_specs,
        out_specs=row(D),
        compiler_params=_cparams(("parallel", "parallel")),
        name="merge_ffn",
    )(x, gates, y_ret, y_fox, y_rwkv, *weights)


def kernel(x, mix_norm, w_in, ret_gn, fox_q_norm, fox_k_norm, fox_f_bias, rwkv_mu, rwkv_w0, rwkv_w2,
           rwkv_a0, rwkv_a2, rwkv_g2, rwkv_k_k, rwkv_k_a, rwkv_r_k, rwkv_gn, p_ret, p_fox, p_rwkv,
           w_out, ffn_norm, w_gate_up, w_down):
    B, S, D = x.shape
    depth = w_in.shape[0]
    cos, sin = _rope_tables(S)
    o0, o1, o2, o3 = (int(v) for v in IN_OFF)
    f_lo = o1 + FOXQKV_COLS
    row = lambda t: t.reshape(1, -1)
    p_ret, p_fox, p_rwkv, w_out, w_gate_up, w_down = (
        t.astype(BF16) for t in (p_ret, p_fox, p_rwkv, w_out, w_gate_up, w_down))
    for l in range(depth):
        w_cat = _w_in_relayout(w_in, l, f_lo, N_HEADS)
        ret_c, fox_c, rwkv_c, gate_c, f_c = _in_proj(x, row(mix_norm[l]), w_cat)

        y_ret = _retention(ret_c, cos, sin, row(ret_gn[l]))

        f_bias = jnp.pad(jnp.tile(fox_f_bias[l], FOX_SPLIT),
                         (0, LANES - FOX_SPLIT * N_HEADS)).reshape(1, LANES)
        fq, fk, fv = _fox_prep(fox_c, f_c, f_bias, jnp.tile(row(fox_q_norm[l]), (1, N_HEADS)),
                               jnp.tile(row(fox_k_norm[l]), (1, N_HEADS)))
        y_fox = _fox_attn(fq, fk, fv).reshape(B, MIX_W, S)

        y_rwkv = _rwkv(rwkv_c, row(rwkv_mu[l]), row(rwkv_w0[l]), rwkv_w2[l].astype(BF16),
                       row(rwkv_a0[l]), rwkv_a2[l].astype(BF16), rwkv_g2[l].astype(BF16),
                       row(rwkv_k_k[l]), row(rwkv_k_a[l]), row(rwkv_r_k[l]), row(rwkv_gn[l]))

        x = _merge_ffn(x, gate_c, y_ret, y_fox, y_rwkv, p_ret, p_fox, p_rwkv, w_out,
                       row(ffn_norm[l]), w_gate_up, w_down, l)
    return x
```

```python
import functools
import math

import jax
import jax.numpy as jnp
import numpy as np
from jax import lax
from jax.experimental import pallas as pl
from jax.experimental.pallas import tpu as pltpu

F32 = jnp.float32
BF16 = jnp.bfloat16

HEAD_DIM = 64
N_HEADS = 4
MIX_W = N_HEADS * HEAD_DIM
RET_CHUNK = 128
ROPE_BASE = 10000.0
DECAY_LORA = 64
AAA_LORA = 64
GATE_LORA = 128
NORM_EPS = 1e-6
RET_GN_EPS = 1e-5
RWKV_GN_EPS = 64e-5
RWKV_CHUNK = 64
RWKV_SUB = 16
LANES = 128
FOX_SPLIT = 3
VMEM_LIMIT = 52 * 1024 * 1024


def _cparams(sem):
    return pltpu.CompilerParams(dimension_semantics=sem, vmem_limit_bytes=VMEM_LIMIT)


def _resident(shape, layer=None):
    nd = len(shape)
    if layer is None:
        return pl.BlockSpec(shape, lambda *_: (0,) * nd, pipeline_mode=pl.Buffered(1))
    return pl.BlockSpec((None,) + tuple(shape), lambda *_: (layer,) + (0,) * nd,
                        pipeline_mode=pl.Buffered(1))


def _mm(a, b):
    return jnp.dot(a.astype(BF16), b.astype(BF16), preferred_element_type=F32)


def _mm_nt(a, b):
    return lax.dot_general(a.astype(BF16), b.astype(BF16), (((1,), (1,)), ((), ())),
                           preferred_element_type=F32)


def _mm_tn(a, b):
    return lax.dot_general(a.astype(BF16), b.astype(BF16), (((0,), (0,)), ((), ())),
                           preferred_element_type=F32)


def _mask_mm_f32(mask_bf16, x):
    h1 = x.astype(BF16)
    r1 = x - h1.astype(F32)
    h2 = r1.astype(BF16)
    h3 = (r1 - h2.astype(F32)).astype(BF16)
    dot = lambda h: jnp.dot(mask_bf16, h, preferred_element_type=F32)
    return dot(h1) + dot(h2) + dot(h3)


def _sigmoid(x):
    return 0.5 * jnp.tanh(0.5 * x) + 0.5


def _log_sigmoid(x):
    return jnp.minimum(x, 0.0) - jnp.log1p(jnp.exp(-jnp.abs(x)))


def _iota2(shape, dim):
    return lax.broadcasted_iota(jnp.int32, shape, dim)


def _head_mask():
    same = (_iota2((MIX_W, MIX_W), 0) >> 6) == (_iota2((MIX_W, MIX_W), 1) >> 6)
    return jnp.where(same, 1.0, 0.0).astype(BF16)


def _head_blockdiag(x, head_mask):
    xb = x.astype(BF16)
    return jnp.concatenate([xb] * N_HEADS, axis=0) * head_mask


def _head_sum(x, head_mask):
    hi = x.astype(BF16)
    lo = (x - hi.astype(F32)).astype(BF16)
    dot = lambda t: jnp.dot(t, head_mask, preferred_element_type=F32)
    return dot(hi) + dot(lo)


def _bmm(a, bd):
    return jnp.dot(a.astype(BF16), bd, preferred_element_type=F32)


def _bmm_nt(a, bd):
    return lax.dot_general(a.astype(BF16), bd, (((1,), (1,)), ((), ())), preferred_element_type=F32)


RET_COLS = 4 * MIX_W
FOXQKV_COLS = 3 * MIX_W
RWKV_COLS = 3 * MIX_W + DECAY_LORA + AAA_LORA + GATE_LORA
IN_OFF = np.cumsum([0, RET_COLS, FOXQKV_COLS, RWKV_COLS])


def _w_in_relayout_kernel(w_ref, o_ref, *, f_lo, n_f):
    ncol = w_ref.shape[1]
    rest = ncol - (f_lo + n_f)
    o_ref[:, 0:f_lo] = w_ref[:, 0:f_lo].astype(BF16)
    o_ref[:, f_lo:f_lo + rest] = w_ref[:, f_lo + n_f:ncol].astype(BF16)
    win = w_ref[:, f_lo:f_lo + LANES]
    lane = _iota2(win.shape, 1)
    reps = jnp.where(lane < n_f, win, 0.0)
    for t in range(1, FOX_SPLIT):
        reps = jnp.where((lane >= t * n_f) & (lane < (t + 1) * n_f), pltpu.roll(win, t * n_f, 1), reps)
    o_ref[:, f_lo + rest:] = reps.astype(BF16)


def _w_in_relayout(w, f_lo, n_f, tr=128):
    D, ncol = w.shape
    out_cols = ncol - n_f + LANES
    return pl.pallas_call(
        functools.partial(_w_in_relayout_kernel, f_lo=f_lo, n_f=n_f),
        out_shape=jax.ShapeDtypeStruct((D, out_cols), BF16),
        grid=(D // tr,),
        in_specs=[pl.BlockSpec((tr, ncol), lambda i: (i, 0))],
        out_specs=pl.BlockSpec((tr, out_cols), lambda i: (i, 0)),
        compiler_params=_cparams(("parallel",)),
        name="w_in_relayout",
    )(w)


def _in_proj_kernel(x_ref, g_ref, w_ref, ret_ref, fox_ref, rwkv_ref, gate_ref, f_ref, *, d_model):
    x = x_ref[...]
    h = x * lax.rsqrt(jnp.mean(x * x, axis=-1, keepdims=True) + NORM_EPS) * g_ref[...]
    hb = h.astype(BF16)
    o0, o1, o2, o3 = (int(v) for v in IN_OFF)
    o4 = o3 + 3 * d_model
    dot = lambda lo, hi: jnp.dot(hb, w_ref[:, lo:hi], preferred_element_type=F32)
    ret_ref[...] = dot(o0, o1)
    fox_ref[...] = dot(o1, o2)
    rwkv_ref[...] = dot(o2, o3)
    gate_ref[...] = dot(o3, o4)
    f_ref[...] = dot(o4, o4 + LANES)


def _in_proj(x, gain, w_cat, tm=512):
    B, S, D = x.shape
    ncol = w_cat.shape[1]
    row = lambda w: pl.BlockSpec((None, tm, w), lambda b, i: (b, i, 0))
    widths = (RET_COLS, FOXQKV_COLS, RWKV_COLS, 3 * D, LANES)
    return pl.pallas_call(
        functools.partial(_in_proj_kernel, d_model=D),
        out_shape=tuple(jax.ShapeDtypeStruct((B, S, w), F32) for w in widths),
        grid=(B, S // tm),
        in_specs=[row(D), _resident((1, D)), _resident((D, ncol))],
        out_specs=tuple(row(w) for w in widths),
        compiler_params=_cparams(("parallel", "parallel")),
        name="in_proj",
    )(x, gain, w_cat)


def _retention_kernel(c_ref, cos_ref, sin_ref, gn_ref, dmat_ref, qdec_ref, kdec_ref, cdec_ref, o_ref,
                      state_ref, *, tb):
    C = RET_CHUNK
    n_batch = c_ref.shape[0]

    @pl.when(pl.program_id(0) == 0)
    def _():
        state_ref[...] = jnp.zeros_like(state_ref)

    head_mask = _head_mask()
    first_half = (_iota2((tb, MIX_W), 1) & (HEAD_DIM - 1)) < HEAD_DIM // 2
    kv_mask = jnp.where((_iota2((N_HEADS * C, MIX_W), 0) >> int(math.log2(C)))
                        == (_iota2((N_HEADS * C, MIX_W), 1) >> int(math.log2(HEAD_DIM))),
                        1.0, 0.0).astype(BF16)
    lane_head = _iota2((HEAD_DIM, MIX_W), 1) >> int(math.log2(HEAD_DIM))
    cos = jnp.concatenate([cos_ref[...]] * (MIX_W // LANES), axis=1)
    sin = jnp.concatenate([sin_ref[...]] * (MIX_W // LANES), axis=1)

    def rope(t):
        partner = jnp.where(first_half, pltpu.roll(t, MIX_W - HEAD_DIM // 2, 1),
                            pltpu.roll(t, HEAD_DIM // 2, 1))
        return t * cos + partner * sin

    def stack_heads(t):
        return jnp.concatenate([t.astype(BF16)] * N_HEADS, axis=0) * kv_mask

    for b in range(n_batch):
        blk = c_ref[b]
        q = rope(blk[:, 0:MIX_W])
        k = rope(blk[:, MIX_W:2 * MIX_W]) * (HEAD_DIM ** -0.5)
        v = blk[:, 2 * MIX_W:3 * MIX_W]
        g = blk[:, 3 * MIX_W:4 * MIX_W]
        state = state_ref[b]
        ys = []
        for c in range(tb // C):
            rows = slice(c * C, (c + 1) * C)
            qc, kc, vc = q[rows], k[rows], v[rows]
            scores = _bmm_nt(qc, stack_heads(kc)) * dmat_ref[...]
            inner = _bmm(scores, stack_heads(vc))
            cross = _bmm(qc * qdec_ref[...], _head_blockdiag(state, head_mask))
            ys.append(inner + cross)
            full = _mm_tn(kc * kdec_ref[...], vc)
            upd = full[0:HEAD_DIM]
            for h in range(1, N_HEADS):
                upd = jnp.where(lane_head == h, full[h * HEAD_DIM:(h + 1) * HEAD_DIM], upd)
            state = state * cdec_ref[...] + upd
        state_ref[b] = state
        y = jnp.concatenate(ys, axis=0)
        yc = y - _head_sum(y, head_mask) * (1.0 / HEAD_DIM)
        var = _head_sum(yc * yc, head_mask) * (1.0 / HEAD_DIM)
        yn = yc * lax.rsqrt(var + RET_GN_EPS)
        o_ref[b] = (g * _sigmoid(g) * (yn * gn_ref[...])).astype(o_ref.dtype)


def _retention_tables():
    C = RET_CHUNK
    log_gamma = jnp.log1p(-jnp.exp2(-5.0 - jnp.arange(N_HEADS, dtype=F32)))
    idx = jnp.arange(C, dtype=F32)
    dist = idx[:, None] - idx[None, :]
    dmat = jnp.where(dist >= 0, jnp.exp(log_gamma[:, None, None] * jnp.maximum(dist, 0.0)), 0.0)
    dmat = dmat.transpose(1, 0, 2).reshape(C, N_HEADS * C)
    lanes = lambda t: jnp.repeat(t, HEAD_DIM, axis=-1)
    q_dec = lanes(jnp.exp(log_gamma[None, :] * (idx[:, None] + 1.0)))
    k_dec = lanes(jnp.exp(log_gamma[None, :] * (C - 1.0 - idx[:, None])))
    c_dec = lanes(jnp.exp(log_gamma * C)[None, :])
    return dmat, q_dec, k_dec, c_dec


def _retention(ret_c, cos, sin, gn, tb=256):
    B, S, W = ret_c.shape
    tables = _retention_tables()
    return pl.pallas_call(
        functools.partial(_retention_kernel, tb=tb),
        out_shape=jax.ShapeDtypeStruct((B, S, MIX_W), BF16),
        grid=(S // tb,),
        in_specs=[pl.BlockSpec((B, tb, W), lambda i: (0, i, 0)),
                  pl.BlockSpec((tb, LANES), lambda i: (i, 0)),
                  pl.BlockSpec((tb, LANES), lambda i: (i, 0)),
                  _resident((1, MIX_W))] + [_resident(t.shape) for t in tables],
        out_specs=pl.BlockSpec((B, tb, MIX_W), lambda i: (0, i, 0)),
        scratch_shapes=[pltpu.VMEM((B, HEAD_DIM, MIX_W), F32)],
        compiler_params=_cparams(("arbitrary",)),
        name="retention",
    )(ret_c, cos, sin, gn, *tables)


def _rope_tables(S):
    half = HEAD_DIM // 2
    inv_freq = ROPE_BASE ** (-jnp.arange(half, dtype=F32) / half)
    ang = jnp.arange(S, dtype=F32)[:, None] * inv_freq[None, :]
    cos, sin = jnp.cos(ang), jnp.sin(ang)
    cos_h = jnp.concatenate([cos, cos], axis=-1)
    sin_h = jnp.concatenate([-sin, sin], axis=-1)
    reps = LANES // HEAD_DIM
    return jnp.tile(cos_h, (1, reps)), jnp.tile(sin_h, (1, reps))


LOG2E = math.log2(math.e)
FOX_AUG = HEAD_DIM
FOX_TQ = 512
FOX_TK = 256
FOX_VROWS = 80
FOX_UNROLL = 4


def _fox_place_tables():
    H, Dh, L = N_HEADS, HEAD_DIM, LANES
    place_q = np.zeros((H // 2, MIX_W + L, 2 * L), np.float32)
    place_k = np.zeros((H // 2, MIX_W + L, 2 * L), np.float32)
    place_vt = np.zeros((H, FOX_VROWS, MIX_W), np.float32)
    for h in range(H):
        out = (h % 2) * L
        for d in range(Dh):
            place_q[h // 2, h * Dh + d, out + d] = 1.0
            place_k[h // 2, h * Dh + d, out + d] = 1.0
            place_vt[h, d, h * Dh + d] = 1.0
        for t in range(FOX_SPLIT):
            place_q[h // 2, MIX_W + t * H + h, out + FOX_AUG + t] = 1.0
            place_k[h // 2, MIX_W + t * H + h, out + FOX_AUG + FOX_SPLIT + t] = -1.0
    return tuple(jnp.asarray(t, BF16) for t in (place_q, place_k, place_vt))


def _fox_prep_kernel(c_ref, f_ref, bias_ref, qg_ref, kg_ref, pq_ref, pk_ref, pvt_ref, q_ref, k_ref,
                     vt_ref, carry_ref, *, ts, tk):
    @pl.when(pl.program_id(1) == 0)
    def _():
        carry_ref[...] = jnp.zeros_like(carry_ref)

    log_f = _log_sigmoid(f_ref[...] + bias_ref[...])
    tril = (_iota2((ts, ts), 0) >= _iota2((ts, ts), 1)).astype(BF16)
    c = _mask_mm_f32(tril, log_f) + carry_ref[0:1, :]
    carry_ref[0:1, :] = c[ts - 1:ts, :]
    c = c * LOG2E
    lane_c = _iota2(c.shape, 1)
    csel = jnp.zeros(c.shape, BF16)
    rest = c
    for t in range(FOX_SPLIT):
        term = rest.astype(BF16)
        csel = jnp.where((lane_c >= t * N_HEADS) & (lane_c < (t + 1) * N_HEADS), term, csel)
        rest = rest - term.astype(F32)

    blk = c_ref[...]
    head_mask = _head_mask()

    def rms(t, gain):
        ms = _head_sum(t * t, head_mask) * (1.0 / HEAD_DIM)
        return t * lax.rsqrt(ms + NORM_EPS) * gain

    qn = rms(blk[:, 0:MIX_W], qg_ref[...]) * (LOG2E * HEAD_DIM ** -0.5)
    kn = rms(blk[:, MIX_W:2 * MIX_W], kg_ref[...])
    q_stack = jnp.concatenate([qn.astype(BF16), csel], axis=1)
    k_stack = jnp.concatenate([kn.astype(BF16), csel], axis=1)
    vb = blk[:, 2 * MIX_W:3 * MIX_W].astype(BF16)

    lane = _iota2((1, 2 * LANES), 1) & (LANES - 1)
    ones_q = jnp.where((lane >= FOX_AUG + FOX_SPLIT) & (lane < FOX_AUG + 2 * FOX_SPLIT), 1.0, 0.0)
    ones_k = jnp.where((lane >= FOX_AUG) & (lane < FOX_AUG + FOX_SPLIT), 1.0, 0.0)
    ones_row = _iota2((FOX_VROWS, ts), 0) == FOX_AUG

    for hp in range(N_HEADS // 2):
        q2 = (jnp.dot(q_stack, pq_ref[hp], preferred_element_type=F32) + ones_q).astype(BF16)
        k2 = (jnp.dot(k_stack, pk_ref[hp], preferred_element_type=F32) + ones_k).astype(BF16)
        for sub in range(2):
            q_ref[2 * hp + sub] = q2[:, sub * LANES:(sub + 1) * LANES]
            k_ref[2 * hp + sub] = k2[:, sub * LANES:(sub + 1) * LANES]
    for h in range(N_HEADS):
        v_t = jnp.where(ones_row, 1.0, _mm_nt(pvt_ref[h], vb))
        for kb in range(ts // tk):
            vt_ref[h, kb] = v_t[:, kb * tk:(kb + 1) * tk].astype(BF16)


def _fox_prep(fox_c, f_c, f_bias, q_gain, k_gain, ts=512):
    B, S, W = fox_c.shape
    tk = FOX_TK
    tables = _fox_place_tables()
    head_spec = pl.BlockSpec((None, N_HEADS, ts, LANES), lambda b, i: (b, 0, i, 0))
    shp = jax.ShapeDtypeStruct((B, N_HEADS, S, LANES), BF16)
    return pl.pallas_call(
        functools.partial(_fox_prep_kernel, ts=ts, tk=tk),
        out_shape=(shp, shp, jax.ShapeDtypeStruct((B, N_HEADS, S // tk, FOX_VROWS, tk), BF16)),
        grid=(B, S // ts),
        in_specs=[pl.BlockSpec((None, ts, W), lambda b, i: (b, i, 0)),
                  pl.BlockSpec((None, ts, LANES), lambda b, i: (b, i, 0)),
                  _resident((1, LANES)), _resident((1, MIX_W)), _resident((1, MIX_W))]
                 + [_resident(t.shape) for t in tables],
        out_specs=(head_spec, head_spec,
                   pl.BlockSpec((None, N_HEADS, ts // tk, FOX_VROWS, tk),
                                lambda b, i: (b, 0, i, 0, 0))),
        scratch_shapes=[pltpu.VMEM((8, LANES), F32)],
        compiler_params=_cparams(("parallel", "arbitrary")),
        name="fox_prep",
    )(fox_c, f_c, f_bias, q_gain, k_gain, *tables)


def _fox_attn_kernel(q_ref, k_ref, vt_ref, o_ref, s0_ref, s1_ref, p0_ref, p1_ref, m_ref, alpha_ref,
                     acc_ref, *, tq, tk):
    qi = pl.program_id(2)
    m_ref[...] = jnp.full(m_ref.shape, -jnp.inf, F32)
    acc_ref[...] = jnp.zeros_like(acc_ref)
    p1_ref[...] = jnp.zeros_like(p1_ref)

    def qk(j, s_ref):
        start = pl.multiple_of(j * tk, tk)
        s_ref[...] = _mm_nt(k_ref[pl.ds(start, tk), :], q_ref[...])

    def pv(j, p_ref):
        return jnp.dot(vt_ref[j], p_ref[...], preferred_element_type=F32)

    def softmax(s_ref, p_ref, key_offset):
        for c in range(0, tq, LANES):
            cols = slice(c, c + LANES)
            s = s_ref[:, cols]
            if key_offset is not None and key_offset + tk - 1 > c:
                keep = _iota2((tk, LANES), 0) + key_offset <= _iota2((tk, LANES), 1) + c
                s = jnp.where(keep, s, -jnp.inf)
            m_old = m_ref[:, cols]
            m_new = jnp.maximum(m_old, jnp.max(s, axis=0, keepdims=True))
            p_ref[:, cols] = jnp.exp2(s - m_new).astype(BF16)
            m_ref[:, cols] = m_new
            alpha_ref[:, cols] = jnp.exp2(m_old - m_new)

    def advance(pv_prev):
        acc_ref[...] = (acc_ref[...] + pv_prev) * alpha_ref[...]

    qk(0, s0_ref)

    def pair(i):
        j = 2 * i
        qk(j + 1, s1_ref)
        pv_prev = pv(jnp.maximum(j - 1, 0), p1_ref)
        softmax(s0_ref, p0_ref, None)
        advance(pv_prev)
        qk(j + 2, s0_ref)
        pv_prev = pv(j, p0_ref)
        softmax(s1_ref, p1_ref, None)
        advance(pv_prev)

    def unrolled(i, carry):
        for u in range(FOX_UNROLL):
            pair(FOX_UNROLL * i + u)
        return carry

    def single(i, carry):
        pair(i)
        return carry

    n_main = qi // FOX_UNROLL
    lax.fori_loop(0, n_main, unrolled, 0)
    lax.fori_loop(n_main * FOX_UNROLL, qi, single, 0)

    j = 2 * qi
    qk(j + 1, s1_ref)
    pv_prev = pv(jnp.maximum(j - 1, 0), p1_ref)
    softmax(s0_ref, p0_ref, 0)
    advance(pv_prev)
    pv_prev = pv(j, p0_ref)
    softmax(s1_ref, p1_ref, tk)
    advance(pv_prev)
    acc = acc_ref[...] + pv(j + 1, p1_ref)
    o_ref[...] = (acc[0:HEAD_DIM] / acc[FOX_AUG:FOX_AUG + 1]).astype(o_ref.dtype)


def _fox_attn(q, k, vt):
    B, H, S, W = q.shape
    tq, tk = FOX_TQ, FOX_TK
    assert tq == 2 * tk and S % tq == 0
    return pl.pallas_call(
        functools.partial(_fox_attn_kernel, tq=tq, tk=tk),
        out_shape=jax.ShapeDtypeStruct((B, H, HEAD_DIM, S), BF16),
        grid=(B, H, S // tq),
        in_specs=[pl.BlockSpec((None, None, tq, W), lambda b, h, i: (b, h, i, 0)),
                  pl.BlockSpec((None, None, S, W), lambda b, h, i: (b, h, 0, 0)),
                  pl.BlockSpec((None, None, S // tk, FOX_VROWS, tk),
                               lambda b, h, i: (b, h, 0, 0, 0))],
        out_specs=pl.BlockSpec((None, None, HEAD_DIM, tq), lambda b, h, i: (b, h, 0, i)),
        scratch_shapes=[pltpu.VMEM((tk, tq), F32), pltpu.VMEM((tk, tq), F32),
                        pltpu.VMEM((tk, tq), BF16), pltpu.VMEM((tk, tq), BF16),
                        pltpu.VMEM((1, tq), F32), pltpu.VMEM((1, tq), F32),
                        pltpu.VMEM((FOX_VROWS, tq), F32)],
        compiler_params=_cparams(("parallel", "parallel", "arbitrary")),
        name="fox_attn",
    )(q, k, vt)


def _unit_lower_inverse(a_list, head_mask):
    C, W = a_list[0].shape
    bdiag = lambda t: _head_blockdiag(t, head_mask)
    ii = _iota2((C, W), 0)
    jj = _iota2((C, W), 1) & (C - 1)
    shift = int(math.log2(RWKV_SUB))
    same_blk = (ii >> shift) == (jj >> shift)
    eye = (ii == jj).astype(F32)
    d = [jnp.where(same_blk, a, 0.0) for a in a_list]
    low = [a - di for a, di in zip(a_list, d)]
    x = [eye + di for di in d]
    dp = d
    bd = [bdiag(t) for t in dp]
    for _ in range(shift - 1):
        dp = [_bmm(t, b) for t, b in zip(dp, bd)]
        yield
        bd = [bdiag(t) for t in dp]
        x = [xi + _bmm(xi, b) for xi, b in zip(x, bd)]
        yield
    n = [_bmm(xi, bdiag(lo)) for xi, lo in zip(x, low)]
    yield
    y = [eye + ni for ni in n]
    npow = n
    for _ in range(int(math.log2(C // RWKV_SUB)) - 1):
        bd = [bdiag(t) for t in npow]
        npow = [_bmm(t, b) for t, b in zip(npow, bd)]
        yield
        y = [yi + _bmm(yi, bdiag(t)) for yi, t in zip(y, npow)]
        yield
    return [_bmm(yi, bdiag(xi)) for yi, xi in zip(y, x)]


RWKV_STASH = (
    ("ar", lambda C, tb: (2 * C, MIX_W), BF16, True),
    ("vc", lambda C, tb: (C, MIX_W), BF16, True),
    ("bd_v", lambda C, tb: (MIX_W, MIX_W), BF16, True),
    ("a_r", lambda C, tb: (C, 2 * MIX_W), BF16, True),
    ("bk_end", lambda C, tb: (2 * C, MIX_W), BF16, True),
    ("t_inv", lambda C, tb: (C, MIX_W), BF16, True),
    ("akv", lambda C, tb: (C, MIX_W), F32, True),
    ("p_end", lambda C, tb: (1, MIX_W), F32, True),
    ("g", lambda C, tb: (tb, MIX_W), F32, False),
    ("bonus", lambda C, tb: (tb, MIX_W), F32, False),
)


def _interleave(*generators):
    alive = list(generators)
    while alive:
        for gen in list(alive):
            try:
                next(gen)
            except StopIteration:
                alive.remove(gen)


def _rwkv_kernel(c_ref, mu_ref, w0_ref, w2_ref, a0_ref, a2_ref, g2_ref, kk_ref, ka_ref, rk_ref,
                 gn_ref, o_ref, state_ref, carry_ref, y_ref, *stash_refs, tb):
    C = RWKV_CHUNK
    W = MIX_W
    n_batch = c_ref.shape[0]
    n_chunks = tb // C
    stash = {name: ref for (name, _, _, _), ref in zip(RWKV_STASH, stash_refs)}
    step = pl.program_id(0)
    n_entries = n_batch * n_chunks
    wr_chunk = (step % 2) * n_entries
    rd_chunk = ((step + 1) % 2) * n_entries
    wr_batch = (step % 2) * n_batch
    rd_batch = ((step + 1) % 2) * n_batch

    @pl.when(pl.program_id(0) == 0)
    def _():
        state_ref[...] = jnp.zeros_like(state_ref)
        carry_ref[...] = jnp.zeros_like(carry_ref)
        for ref in stash_refs:
            ref[...] = jnp.zeros_like(ref)

    head_mask = _head_mask()
    bdiag = lambda t: _head_blockdiag(t, head_mask)
    ii = _iota2((C, W), 0)
    jj = _iota2((C, W), 1) & (C - 1)
    strict = ii > jj
    incl = ii >= jj
    tril = (_iota2((C, C), 0) >= _iota2((C, C), 1)).astype(BF16)
    lane_head = _iota2((HEAD_DIM, W), 1) >> int(math.log2(HEAD_DIM))
    prepared = {}

    def prepare():
        post = []
        pre = []
        for b in range(n_batch):
            cur = c_ref[b]
            prev = jnp.where(_iota2(cur.shape, 0) == 0, carry_ref[b, 0:1, :],
                             pltpu.roll(cur, 1, 0))
            carry_ref[b, 0:1, :] = cur[tb - 1:tb, :]
            mixed = cur + (prev - cur) * mu_ref[...]
            r = mixed[:, 0:W]
            k = mixed[:, W:2 * W]
            v = mixed[:, 2 * W:3 * W]
            xw = mixed[:, 3 * W:3 * W + DECAY_LORA]
            xa = mixed[:, 3 * W + DECAY_LORA:3 * W + DECAY_LORA + AAA_LORA]
            xg = mixed[:, 3 * W + DECAY_LORA + AAA_LORA:]

            log_w = -jnp.exp(_log_sigmoid(w0_ref[...] + _mm(jnp.tanh(xw), w2_ref[...])) - 0.5)
            a = _sigmoid(a0_ref[...] + _mm(xa, a2_ref[...]))
            g = _mm(_sigmoid(xg), g2_ref[...])
            kk = k * kk_ref[...]
            k2 = k * (1.0 + (a - 1.0) * ka_ref[...])
            kk = kk / jnp.maximum(jnp.sqrt(_head_sum(kk * kk, head_mask)), 1e-12)
            a_vec = -kk
            b_vec = kk * a
            post.append((g, _head_sum(r * k2 * rk_ref[...], head_mask) * v))
            yield

            for c in range(n_chunks):
                rows = slice(c * C, (c + 1) * C)
                lw = log_w[rows]
                cum = _mask_mm_f32(tril, lw)
                e_pos = jnp.exp(cum)
                e_neg = jnp.exp(-cum)
                p_end = e_pos[C - 1:C, :]
                at = a_vec[rows] * jnp.exp(cum - lw)
                rt = r[rows] * e_pos
                bt = b_vec[rows] * e_neg
                kt = k2[rows] * e_neg
                ar = jnp.concatenate([at, rt], axis=0)
                gb = _bmm_nt(ar, bdiag(bt))
                gk = _bmm_nt(ar, bdiag(kt))
                vc = v[rows]
                pre.append(dict(
                    ar=ar, p_end=p_end, vc=vc, bd_v=bdiag(vc),
                    a_ab=jnp.where(strict, gb[0:C], 0.0), a_ak=jnp.where(strict, gk[0:C], 0.0),
                    a_r=jnp.concatenate([jnp.where(incl, gb[C:], 0.0),
                                         jnp.where(incl, gk[C:], 0.0)], axis=1),
                    bk_end=jnp.concatenate([bt * p_end, kt * p_end], axis=0)))
                yield
        t_inv = yield from _unit_lower_inverse([p["a_ab"] for p in pre], head_mask)
        for p, t in zip(pre, t_inv):
            p["t_inv"] = t
            p["akv"] = _bmm(p["a_ak"], p["bd_v"])
        prepared["pre"] = pre
        prepared["post"] = post

    def recur():
        s_all = [state_ref[b] for b in range(n_batch)]
        batches = range(n_batch)
        for c in range(n_chunks):
            ent = [rd_chunk + b * n_chunks + c for b in batches]
            ars = [_bmm_nt(stash["ar"][e], bdiag(s_all[b])) for b, e in zip(batches, ent)]
            yield
            u = [_bmm(stash["t_inv"][e], bdiag(ars[b][0:C] + stash["akv"][e]))
                 for b, e in zip(batches, ent)]
            yield
            for b, e in zip(batches, ent):
                uv = jnp.concatenate([u[b].astype(BF16), stash["vc"][e]], axis=0)
                y_ref[b, c * C:(c + 1) * C, :] = ars[b][C:] + jnp.dot(
                    stash["a_r"][e], jnp.concatenate([bdiag(u[b]), stash["bd_v"][e]], axis=0),
                    preferred_element_type=F32)
                full = _mm_tn(uv, stash["bk_end"][e])
                upd = full[0:HEAD_DIM]
                for h in range(1, N_HEADS):
                    upd = jnp.where(lane_head == h, full[h * HEAD_DIM:(h + 1) * HEAD_DIM], upd)
                s_all[b] = s_all[b] * stash["p_end"][e] + upd
            yield
        for b in range(n_batch):
            state_ref[b] = s_all[b]
            y = y_ref[b]
            yc = y - _head_sum(y, head_mask) * (1.0 / HEAD_DIM)
            var = _head_sum(yc * yc, head_mask) * (1.0 / HEAD_DIM)
            yn = yc * lax.rsqrt(var + RWKV_GN_EPS) * gn_ref[...]
            o_ref[b] = ((yn + stash["bonus"][rd_batch + b])
                        * stash["g"][rd_batch + b]).astype(o_ref.dtype)
            yield

    _interleave(prepare(), recur())

    for e, p in enumerate(prepared["pre"]):
        for name, _, dtype, per_chunk in RWKV_STASH:
            if per_chunk:
                stash[name][wr_chunk + e] = p[name].astype(dtype)
    for b, (g, bonus) in enumerate(prepared["post"]):
        stash["g"][wr_batch + b] = g
        stash["bonus"][wr_batch + b] = bonus


def _rwkv(rwkv_c, mu, w0, w2, a0, a2, g2, k_k, k_a, r_k, gn, tb=256):
    B, S, W = rwkv_c.shape
    n_blocks = S // tb
    n_entries = B * (tb // RWKV_CHUNK)
    vec = lambda t: _resident((1, t.shape[-1]))
    mat = lambda t: _resident(t.shape)
    stash_shapes = [pltpu.VMEM((2 * (n_entries if per_chunk else B),) + shape(RWKV_CHUNK, tb), dtype)
                    for _, shape, dtype, per_chunk in RWKV_STASH]
    return pl.pallas_call(
        functools.partial(_rwkv_kernel, tb=tb),
        out_shape=jax.ShapeDtypeStruct((B, S, MIX_W), BF16),
        grid=(n_blocks + 1,),
        in_specs=[pl.BlockSpec((B, tb, W), lambda i: (0, jnp.minimum(i, n_blocks - 1), 0)),
                  vec(mu), vec(w0), mat(w2), vec(a0), mat(a2), mat(g2), vec(k_k), vec(k_a),
                  vec(r_k), vec(gn)],
        out_specs=pl.BlockSpec((B, tb, MIX_W), lambda i: (0, jnp.maximum(i - 1, 0), 0)),
        scratch_shapes=[pltpu.VMEM((B, HEAD_DIM, MIX_W), F32),
                        pltpu.VMEM((B, 8, W), F32),
                        pltpu.VMEM((B, tb, MIX_W), F32)] + stash_shapes,
        compiler_params=_cparams(("arbitrary",)),
        name="rwkv7",
    )(rwkv_c, mu, w0, w2, a0, a2, g2, k_k, k_a, r_k, gn)


def _merge_ffn_kernel(x_ref, gate_ref, yret_ref, yfox_ref, yrwkv_ref, pret_ref, pfox_ref, prwkv_ref,
                      wout_ref, g_ref, wgu_ref, wd_ref, o_ref, *, d_model, d_ff, ck):
    D = d_model
    dot = lambda a, b: jnp.dot(a, b, preferred_element_type=F32)
    gates = gate_ref[...]
    fox = lax.dot_general(yfox_ref[...], pfox_ref[...], (((0,), (0,)), ((), ())),
                          preferred_element_type=F32)
    merged = (_sigmoid(gates[:, 0:D]) * dot(yret_ref[...], pret_ref[...])
              + _sigmoid(gates[:, D:2 * D]) * fox
              + _sigmoid(gates[:, 2 * D:3 * D]) * dot(yrwkv_ref[...], prwkv_ref[...]))
    x = x_ref[...] + dot(merged.astype(BF16), wout_ref[...])

    h = (x * lax.rsqrt(jnp.mean(x * x, axis=-1, keepdims=True) + NORM_EPS) * g_ref[...]).astype(BF16)
    acc = x
    for c in range(d_ff // ck):
        lo = c * ck
        gate = dot(h, wgu_ref[:, lo:lo + ck])
        up = dot(h, wgu_ref[:, d_ff + lo:d_ff + lo + ck])
        act = (gate * _sigmoid(gate) * up).astype(BF16)
        acc = acc + dot(act, wd_ref[lo:lo + ck, :])
    o_ref[...] = acc


def _merge_ffn(x, gates, y_ret, y_fox, y_rwkv, p_ret, p_fox, p_rwkv, w_out, gain, w_gate_up, w_down,
               layer, tm=512, ck=256):
    B, S, D = x.shape
    d_ff = w_down.shape[1]
    row = lambda w: pl.BlockSpec((None, tm, w), lambda b, i: (b, i, 0))
    stacked = lambda w: _resident(w.shape[1:], layer)
    weights = (p_ret, p_fox, p_rwkv, w_out, gain, w_gate_up, w_down)
    weight_specs = [stacked(p_ret), stacked(p_fox), stacked(p_rwkv), stacked(w_out),
                    _resident(gain.shape), stacked(w_gate_up), stacked(w_down)]
    return pl.pallas_call(
        functools.partial(_merge_ffn_kernel, d_model=D, d_ff=d_ff, ck=ck),
        out_shape=jax.ShapeDtypeStruct((B, S, D), F32),
        grid=(B, S // tm),
        in_specs=[row(D), row(3 * D), row(MIX_W),
                  pl.BlockSpec((None, MIX_W, tm), lambda b, i: (b, 0, i)),
                  row(MIX_W)] + weight_specs,
        out_specs=row(D),
        compiler_params=_cparams(("parallel", "parallel")),
        name="merge_ffn",
    )(x, gates, y_ret, y_fox, y_rwkv, *weights)


def kernel(x, mix_norm, w_in, ret_gn, fox_q_norm, fox_k_norm, fox_f_bias, rwkv_mu, rwkv_w0, rwkv_w2,
           rwkv_a0, rwkv_a2, rwkv_g2, rwkv_k_k, rwkv_k_a, rwkv_r_k, rwkv_gn, p_ret, p_fox, p_rwkv,
           w_out, ffn_norm, w_gate_up, w_down):
    B, S, D = x.shape
    depth = w_in.shape[0]
    cos, sin = _rope_tables(S)
    o0, o1, o2, o3 = (int(v) for v in IN_OFF)
    f_lo = o1 + FOXQKV_COLS
    row = lambda t: t.reshape(1, -1)
    p_ret, p_fox, p_rwkv, w_out, w_gate_up, w_down = (
        t.astype(BF16) for t in (p_ret, p_fox, p_rwkv, w_out, w_gate_up, w_down))
    for l in range(depth):
        w_cat = _w_in_relayout(w_in[l], f_lo, N_HEADS)
        ret_c, fox_c, rwkv_c, gate_c, f_c = _in_proj(x, row(mix_norm[l]), w_cat)

        y_ret = _retention(ret_c, cos, sin, row(ret_gn[l]))

        f_bias = jnp.pad(jnp.tile(fox_f_bias[l], FOX_SPLIT),
                         (0, LANES - FOX_SPLIT * N_HEADS)).reshape(1, LANES)
        fq, fk, fv = _fox_prep(fox_c, f_c, f_bias, jnp.tile(row(fox_q_norm[l]), (1, N_HEADS)),
                               jnp.tile(row(fox_k_norm[l]), (1, N_HEADS)))
        y_fox = _fox_attn(fq, fk, fv).reshape(B, MIX_W, S)

        y_rwkv = _rwkv(rwkv_c, row(rwkv_mu[l]), row(rwkv_w0[l]), rwkv_w2[l].astype(BF16),
                       row(rwkv_a0[l]), rwkv_a2[l].astype(BF16), rwkv_g2[l].astype(BF16),
                       row(rwkv_k_k[l]), row(rwkv_k_a[l]), row(rwkv_r_k[l]), row(rwkv_gn[l]))

        x = _merge_ffn(x, gate_c, y_ret, y_fox, y_rwkv, p_ret, p_fox, p_rwkv, w_out,
                       row(ffn_norm[l]), w_gate_up, w_down, l)
    return x
```

```python
import functools
import math

import jax
import jax.numpy as jnp
import numpy as np
from jax import lax
from jax.experimental import pallas as pl
from jax.experimental.pallas import tpu as pltpu

F32 = jnp.float32
BF16 = jnp.bfloat16

HEAD_DIM = 64
N_HEADS = 4
MIX_W = N_HEADS * HEAD_DIM
RET_CHUNK = 128
ROPE_BASE = 10000.0
DECAY_LORA = 64
AAA_LORA = 64
GATE_LORA = 128
NORM_EPS = 1e-6
RET_GN_EPS = 1e-5
RWKV_GN_EPS = 64e-5
RWKV_CHUNK = 64
RWKV_SUB = 16
LANES = 128
FOX_SPLIT = 3
VMEM_LIMIT = 52 * 1024 * 1024


def _cparams(sem):
    return pltpu.CompilerParams(dimension_semantics=sem, vmem_limit_bytes=VMEM_LIMIT)


def _resident(shape, layer=None):
    nd = len(shape)
    if layer is None:
        return pl.BlockSpec(shape, lambda *_: (0,) * nd, pipeline_mode=pl.Buffered(1))
    return pl.BlockSpec((None,) + tuple(shape), lambda *_: (layer,) + (0,) * nd,
                        pipeline_mode=pl.Buffered(1))


def _mm(a, b):
    return jnp.dot(a.astype(BF16), b.astype(BF16), preferred_element_type=F32)


def _mm_nt(a, b):
    return lax.dot_general(a.astype(BF16), b.astype(BF16), (((1,), (1,)), ((), ())),
                           preferred_element_type=F32)


def _mm_tn(a, b):
    return lax.dot_general(a.astype(BF16), b.astype(BF16), (((0,), (0,)), ((), ())),
                           preferred_element_type=F32)


def _mask_mm_f32(mask_bf16, x):
    h1 = x.astype(BF16)
    r1 = x - h1.astype(F32)
    h2 = r1.astype(BF16)
    h3 = (r1 - h2.astype(F32)).astype(BF16)
    dot = lambda h: jnp.dot(mask_bf16, h, preferred_element_type=F32)
    return dot(h1) + dot(h2) + dot(h3)


def _sigmoid(x):
    return 0.5 * jnp.tanh(0.5 * x) + 0.5


def _log_sigmoid(x):
    return jnp.minimum(x, 0.0) - jnp.log1p(jnp.exp(-jnp.abs(x)))


def _iota2(shape, dim):
    return lax.broadcasted_iota(jnp.int32, shape, dim)


def _head_mask():
    same = (_iota2((MIX_W, MIX_W), 0) >> 6) == (_iota2((MIX_W, MIX_W), 1) >> 6)
    return jnp.where(same, 1.0, 0.0).astype(BF16)


def _head_blockdiag(x, head_mask):
    xb = x.astype(BF16)
    return jnp.concatenate([xb] * N_HEADS, axis=0) * head_mask


def _head_sum(x, head_mask):
    hi = x.astype(BF16)
    lo = (x - hi.astype(F32)).astype(BF16)
    dot = lambda t: jnp.dot(t, head_mask, preferred_element_type=F32)
    return dot(hi) + dot(lo)


def _bmm(a, bd):
    return jnp.dot(a.astype(BF16), bd, preferred_element_type=F32)


def _bmm_nt(a, bd):
    return lax.dot_general(a.astype(BF16), bd, (((1,), (1,)), ((), ())), preferred_element_type=F32)


RET_COLS = 4 * MIX_W
FOXQKV_COLS = 3 * MIX_W
RWKV_COLS = 3 * MIX_W + DECAY_LORA + AAA_LORA + GATE_LORA
IN_OFF = np.cumsum([0, RET_COLS, FOXQKV_COLS, RWKV_COLS])


def _w_in_relayout_kernel(w_ref, o_ref, *, f_lo, n_f):
    ncol = w_ref.shape[1]
    rest = ncol - (f_lo + n_f)
    o_ref[:, 0:f_lo] = w_ref[:, 0:f_lo].astype(BF16)
    o_ref[:, f_lo:f_lo + rest] = w_ref[:, f_lo + n_f:ncol].astype(BF16)
    win = w_ref[:, f_lo:f_lo + LANES]
    lane = _iota2(win.shape, 1)
    reps = jnp.where(lane < n_f, win, 0.0)
    for t in range(1, FOX_SPLIT):
        reps = jnp.where((lane >= t * n_f) & (lane < (t + 1) * n_f), pltpu.roll(win, t * n_f, 1), reps)
    o_ref[:, f_lo + rest:] = reps.astype(BF16)


def _w_in_relayout(w, f_lo, n_f, tr=128):
    D, ncol = w.shape
    out_cols = ncol - n_f + LANES
    return pl.pallas_call(
        functools.partial(_w_in_relayout_kernel, f_lo=f_lo, n_f=n_f),
        out_shape=jax.ShapeDtypeStruct((D, out_cols), BF16),
        grid=(D // tr,),
        in_specs=[pl.BlockSpec((tr, ncol), lambda i: (i, 0))],
        out_specs=pl.BlockSpec((tr, out_cols), lambda i: (i, 0)),
        compiler_params=_cparams(("parallel",)),
        name="w_in_relayout",
    )(w)


def _in_proj_kernel(x_ref, g_ref, w_ref, ret_ref, fox_ref, rwkv_ref, gate_ref, f_ref, *, d_model):
    x = x_ref[...]
    h = x * lax.rsqrt(jnp.mean(x * x, axis=-1, keepdims=True) + NORM_EPS) * g_ref[...]
    hb = h.astype(BF16)
    o0, o1, o2, o3 = (int(v) for v in IN_OFF)
    o4 = o3 + 3 * d_model
    dot = lambda lo, hi: jnp.dot(hb, w_ref[:, lo:hi], preferred_element_type=F32)
    ret_ref[...] = dot(o0, o1)
    fox_ref[...] = dot(o1, o2)
    rwkv_ref[...] = dot(o2, o3)
    gate_ref[...] = dot(o3, o4)
    f_ref[...] = dot(o4, o4 + LANES)


def _in_proj(x, gain, w_cat, tm=512):
    B, S, D = x.shape
    ncol = w_cat.shape[1]
    row = lambda w: pl.BlockSpec((None, tm, w), lambda b, i: (b, i, 0))
    widths = (RET_COLS, FOXQKV_COLS, RWKV_COLS, 3 * D, LANES)
    return pl.pallas_call(
        functools.partial(_in_proj_kernel, d_model=D),
        out_shape=tuple(jax.ShapeDtypeStruct((B, S, w), F32) for w in widths),
        grid=(B, S // tm),
        in_specs=[row(D), _resident((1, D)), _resident((D, ncol))],
        out_specs=tuple(row(w) for w in widths),
        compiler_params=_cparams(("parallel", "parallel")),
        name="in_proj",
    )(x, gain, w_cat)


def _retention_kernel(c_ref, cos_ref, sin_ref, gn_ref, dmat_ref, qdec_ref, kdec_ref, cdec_ref, o_ref,
                      state_ref, *, tb):
    C = RET_CHUNK
    n_batch = c_ref.shape[0]

    @pl.when(pl.program_id(0) == 0)
    def _():
        state_ref[...] = jnp.zeros_like(state_ref)

    head_mask = _head_mask()
    first_half = (_iota2((tb, MIX_W), 1) & (HEAD_DIM - 1)) < HEAD_DIM // 2
    kv_mask = jnp.where((_iota2((N_HEADS * C, MIX_W), 0) >> int(math.log2(C)))
                        == (_iota2((N_HEADS * C, MIX_W), 1) >> int(math.log2(HEAD_DIM))),
                        1.0, 0.0).astype(BF16)
    lane_head = _iota2((HEAD_DIM, MIX_W), 1) >> int(math.log2(HEAD_DIM))
    cos = jnp.concatenate([cos_ref[...]] * (MIX_W // LANES), axis=1)
    sin = jnp.concatenate([sin_ref[...]] * (MIX_W // LANES), axis=1)

    def rope(t):
        partner = jnp.where(first_half, pltpu.roll(t, MIX_W - HEAD_DIM // 2, 1),
                            pltpu.roll(t, HEAD_DIM // 2, 1))
        return t * cos + partner * sin

    def stack_heads(t):
        return jnp.concatenate([t.astype(BF16)] * N_HEADS, axis=0) * kv_mask

    for b in range(n_batch):
        blk = c_ref[b]
        q = rope(blk[:, 0:MIX_W])
        k = rope(blk[:, MIX_W:2 * MIX_W]) * (HEAD_DIM ** -0.5)
        v = blk[:, 2 * MIX_W:3 * MIX_W]
        g = blk[:, 3 * MIX_W:4 * MIX_W]
        state = state_ref[b]
        ys = []
        for c in range(tb // C):
            rows = slice(c * C, (c + 1) * C)
            qc, kc, vc = q[rows], k[rows], v[rows]
            scores = _bmm_nt(qc, stack_heads(kc)) * dmat_ref[...]
            inner = _bmm(scores, stack_heads(vc))
            cross = _bmm(qc * qdec_ref[...], _head_blockdiag(state, head_mask))
            ys.append(inner + cross)
            full = _mm_tn(kc * kdec_ref[...], vc)
            upd = full[0:HEAD_DIM]
            for h in range(1, N_HEADS):
                upd = jnp.where(lane_head == h, full[h * HEAD_DIM:(h + 1) * HEAD_DIM], upd)
            state = state * cdec_ref[...] + upd
        state_ref[b] = state
        y = jnp.concatenate(ys, axis=0)
        yc = y - _head_sum(y, head_mask) * (1.0 / HEAD_DIM)
        var = _head_sum(yc * yc, head_mask) * (1.0 / HEAD_DIM)
        yn = yc * lax.rsqrt(var + RET_GN_EPS)
        o_ref[b] = (g * _sigmoid(g) * (yn * gn_ref[...])).astype(o_ref.dtype)


def _retention_tables():
    C = RET_CHUNK
    log_gamma = jnp.log1p(-jnp.exp2(-5.0 - jnp.arange(N_HEADS, dtype=F32)))
    idx = jnp.arange(C, dtype=F32)
    dist = idx[:, None] - idx[None, :]
    dmat = jnp.where(dist >= 0, jnp.exp(log_gamma[:, None, None] * jnp.maximum(dist, 0.0)), 0.0)
    dmat = dmat.transpose(1, 0, 2).reshape(C, N_HEADS * C)
    lanes = lambda t: jnp.repeat(t, HEAD_DIM, axis=-1)
    q_dec = lanes(jnp.exp(log_gamma[None, :] * (idx[:, None] + 1.0)))
    k_dec = lanes(jnp.exp(log_gamma[None, :] * (C - 1.0 - idx[:, None])))
    c_dec = lanes(jnp.exp(log_gamma * C)[None, :])
    return dmat, q_dec, k_dec, c_dec


def _retention(ret_c, cos, sin, gn, tb=256):
    B, S, W = ret_c.shape
    tables = _retention_tables()
    return pl.pallas_call(
        functools.partial(_retention_kernel, tb=tb),
        out_shape=jax.ShapeDtypeStruct((B, S, MIX_W), BF16),
        grid=(S // tb,),
        in_specs=[pl.BlockSpec((B, tb, W), lambda i: (0, i, 0)),
                  pl.BlockSpec((tb, LANES), lambda i: (i, 0)),
                  pl.BlockSpec((tb, LANES), lambda i: (i, 0)),
                  _resident((1, MIX_W))] + [_resident(t.shape) for t in tables],
        out_specs=pl.BlockSpec((B, tb, MIX_W), lambda i: (0, i, 0)),
        scratch_shapes=[pltpu.VMEM((B, HEAD_DIM, MIX_W), F32)],
        compiler_params=_cparams(("arbitrary",)),
        name="retention",
    )(ret_c, cos, sin, gn, *tables)


def _rope_tables(S):
    half = HEAD_DIM // 2
    inv_freq = ROPE_BASE ** (-jnp.arange(half, dtype=F32) / half)
    ang = jnp.arange(S, dtype=F32)[:, None] * inv_freq[None, :]
    cos, sin = jnp.cos(ang), jnp.sin(ang)
    cos_h = jnp.concatenate([cos, cos], axis=-1)
    sin_h = jnp.concatenate([-sin, sin], axis=-1)
    reps = LANES // HEAD_DIM
    return jnp.tile(cos_h, (1, reps)), jnp.tile(sin_h, (1, reps))


LOG2E = math.log2(math.e)
FOX_AUG = HEAD_DIM
FOX_TQ = 512
FOX_TK = 256
FOX_VROWS = 80
FOX_UNROLL = 8


def _fox_place_tables():
    H, Dh, L = N_HEADS, HEAD_DIM, LANES
    place_q = np.zeros((H // 2, MIX_W + L, 2 * L), np.float32)
    place_k = np.zeros((H // 2, MIX_W + L, 2 * L), np.float32)
    place_vt = np.zeros((H, FOX_VROWS, MIX_W), np.float32)
    for h in range(H):
        out = (h % 2) * L
        for d in range(Dh):
            place_q[h // 2, h * Dh + d, out + d] = 1.0
            place_k[h // 2, h * Dh + d, out + d] = 1.0
            place_vt[h, d, h * Dh + d] = 1.0
        for t in range(FOX_SPLIT):
            place_q[h // 2, MIX_W + t * H + h, out + FOX_AUG + t] = 1.0
            place_k[h // 2, MIX_W + t * H + h, out + FOX_AUG + FOX_SPLIT + t] = -1.0
    return tuple(jnp.asarray(t, BF16) for t in (place_q, place_k, place_vt))


def _fox_prep_kernel(c_ref, f_ref, bias_ref, qg_ref, kg_ref, pq_ref, pk_ref, pvt_ref, q_ref, k_ref,
                     vt_ref, carry_ref, *, ts, tk):
    @pl.when(pl.program_id(1) == 0)
    def _():
        carry_ref[...] = jnp.zeros_like(carry_ref)

    log_f = _log_sigmoid(f_ref[...] + bias_ref[...])
    tril = (_iota2((ts, ts), 0) >= _iota2((ts, ts), 1)).astype(BF16)
    c = _mask_mm_f32(tril, log_f) + carry_ref[0:1, :]
    carry_ref[0:1, :] = c[ts - 1:ts, :]
    c = c * LOG2E
    lane_c = _iota2(c.shape, 1)
    csel = jnp.zeros(c.shape, BF16)
    rest = c
    for t in range(FOX_SPLIT):
        term = rest.astype(BF16)
        csel = jnp.where((lane_c >= t * N_HEADS) & (lane_c < (t + 1) * N_HEADS), term, csel)
        rest = rest - term.astype(F32)

    blk = c_ref[...]
    head_mask = _head_mask()

    def rms(t, gain):
        ms = _head_sum(t * t, head_mask) * (1.0 / HEAD_DIM)
        return t * lax.rsqrt(ms + NORM_EPS) * gain

    qn = rms(blk[:, 0:MIX_W], qg_ref[...]) * (LOG2E * HEAD_DIM ** -0.5)
    kn = rms(blk[:, MIX_W:2 * MIX_W], kg_ref[...])
    q_stack = jnp.concatenate([qn.astype(BF16), csel], axis=1)
    k_stack = jnp.concatenate([kn.astype(BF16), csel], axis=1)
    vb = blk[:, 2 * MIX_W:3 * MIX_W].astype(BF16)

    lane = _iota2((1, 2 * LANES), 1) & (LANES - 1)
    ones_q = jnp.where((lane >= FOX_AUG + FOX_SPLIT) & (lane < FOX_AUG + 2 * FOX_SPLIT), 1.0, 0.0)
    ones_k = jnp.where((lane >= FOX_AUG) & (lane < FOX_AUG + FOX_SPLIT), 1.0, 0.0)
    ones_row = _iota2((FOX_VROWS, ts), 0) == FOX_AUG

    for hp in range(N_HEADS // 2):
        q2 = (jnp.dot(q_stack, pq_ref[hp], preferred_element_type=F32) + ones_q).astype(BF16)
        k2 = (jnp.dot(k_stack, pk_ref[hp], preferred_element_type=F32) + ones_k).astype(BF16)
        for sub in range(2):
            q_ref[2 * hp + sub] = q2[:, sub * LANES:(sub + 1) * LANES]
            k_ref[2 * hp + sub] = k2[:, sub * LANES:(sub + 1) * LANES]
    for h in range(N_HEADS):
        v_t = jnp.where(ones_row, 1.0, _mm_nt(pvt_ref[h], vb))
        for kb in range(ts // tk):
            vt_ref[h, kb] = v_t[:, kb * tk:(kb + 1) * tk].astype(BF16)


def _fox_prep(fox_c, f_c, f_bias, q_gain, k_gain, ts=512):
    B, S, W = fox_c.shape
    tk = FOX_TK
    tables = _fox_place_tables()
    head_spec = pl.BlockSpec((None, N_HEADS, ts, LANES), lambda b, i: (b, 0, i, 0))
    shp = jax.ShapeDtypeStruct((B, N_HEADS, S, LANES), BF16)
    return pl.pallas_call(
        functools.partial(_fox_prep_kernel, ts=ts, tk=tk),
        out_shape=(shp, shp, jax.ShapeDtypeStruct((B, N_HEADS, S // tk, FOX_VROWS, tk), BF16)),
        grid=(B, S // ts),
        in_specs=[pl.BlockSpec((None, ts, W), lambda b, i: (b, i, 0)),
                  pl.BlockSpec((None, ts, LANES), lambda b, i: (b, i, 0)),
                  _resident((1, LANES)), _resident((1, MIX_W)), _resident((1, MIX_W))]
                 + [_resident(t.shape) for t in tables],
        out_specs=(head_spec, head_spec,
                   pl.BlockSpec((None, N_HEADS, ts // tk, FOX_VROWS, tk),
                                lambda b, i: (b, 0, i, 0, 0))),
        scratch_shapes=[pltpu.VMEM((8, LANES), F32)],
        compiler_params=_cparams(("parallel", "arbitrary")),
        name="fox_prep",
    )(fox_c, f_c, f_bias, q_gain, k_gain, *tables)


def _fox_attn_kernel(q_ref, k_ref, vt_ref, o_ref, s0_ref, s1_ref, p0_ref, p1_ref, m_ref, alpha_ref,
                     acc_ref, *, tq, tk):
    qi = pl.program_id(2)
    m_ref[...] = jnp.full(m_ref.shape, -jnp.inf, F32)
    acc_ref[...] = jnp.zeros_like(acc_ref)
    p1_ref[...] = jnp.zeros_like(p1_ref)

    def qk(j, s_ref):
        start = pl.multiple_of(j * tk, tk)
        s_ref[...] = _mm_nt(k_ref[pl.ds(start, tk), :], q_ref[...])

    def pv(j, p_ref):
        return jnp.dot(vt_ref[j], p_ref[...], preferred_element_type=F32)

    def softmax(s_ref, p_ref, key_offset):
        for c in range(0, tq, LANES):
            cols = slice(c, c + LANES)
            s = s_ref[:, cols]
            if key_offset is not None and key_offset + tk - 1 > c:
                keep = _iota2((tk, LANES), 0) + key_offset <= _iota2((tk, LANES), 1) + c
                s = jnp.where(keep, s, -jnp.inf)
            m_old = m_ref[:, cols]
            m_new = jnp.maximum(m_old, jnp.max(s, axis=0, keepdims=True))
            p_ref[:, cols] = jnp.exp2(s - m_new).astype(BF16)
            m_ref[:, cols] = m_new
            alpha_ref[:, cols] = jnp.exp2(m_old - m_new)

    def advance(pv_prev):
        acc_ref[...] = (acc_ref[...] + pv_prev) * alpha_ref[...]

    qk(0, s0_ref)

    def pair(i):
        j = 2 * i
        qk(j + 1, s1_ref)
        pv_prev = pv(jnp.maximum(j - 1, 0), p1_ref)
        softmax(s0_ref, p0_ref, None)
        advance(pv_prev)
        qk(j + 2, s0_ref)
        pv_prev = pv(j, p0_ref)
        softmax(s1_ref, p1_ref, None)
        advance(pv_prev)

    done = 0
    unroll = FOX_UNROLL
    while unroll >= 1:
        def group(i, carry, unroll=unroll):
            for u in range(unroll):
                pair(i + u)
            return carry

        n_groups = (qi - done) // unroll
        lax.fori_loop(0, n_groups, lambda g, c, done=done, unroll=unroll: group(done + g * unroll, c),
                      0)
        done = done + n_groups * unroll
        unroll //= 2

    j = 2 * qi
    qk(j + 1, s1_ref)
    pv_prev = pv(jnp.maximum(j - 1, 0), p1_ref)
    softmax(s0_ref, p0_ref, 0)
    advance(pv_prev)
    pv_prev = pv(j, p0_ref)
    softmax(s1_ref, p1_ref, tk)
    advance(pv_prev)
    acc = acc_ref[...] + pv(j + 1, p1_ref)
    o_ref[...] = (acc[0:HEAD_DIM] / acc[FOX_AUG:FOX_AUG + 1]).astype(o_ref.dtype)


def _fox_attn(q, k, vt):
    B, H, S, W = q.shape
    tq, tk = FOX_TQ, FOX_TK
    assert tq == 2 * tk and S % tq == 0
    return pl.pallas_call(
        functools.partial(_fox_attn_kernel, tq=tq, tk=tk),
        out_shape=jax.ShapeDtypeStruct((B, H, HEAD_DIM, S), BF16),
        grid=(B, H, S // tq),
        in_specs=[pl.BlockSpec((None, None, tq, W), lambda b, h, i: (b, h, i, 0)),
                  pl.BlockSpec((None, None, S, W), lambda b, h, i: (b, h, 0, 0)),
                  pl.BlockSpec((None, None, S // tk, FOX_VROWS, tk),
                               lambda b, h, i: (b, h, 0, 0, 0))],
        out_specs=pl.BlockSpec((None, None, HEAD_DIM, tq), lambda b, h, i: (b, h, 0, i)),
        scratch_shapes=[pltpu.VMEM((tk, tq), F32), pltpu.VMEM((tk, tq), F32),
                        pltpu.VMEM((tk, tq), BF16), pltpu.VMEM((tk, tq), BF16),
                        pltpu.VMEM((1, tq), F32), pltpu.VMEM((1, tq), F32),
                        pltpu.VMEM((FOX_VROWS, tq), F32)],
        compiler_params=_cparams(("parallel", "parallel", "arbitrary")),
        name="fox_attn",
    )(q, k, vt)


def _unit_lower_inverse(a_list, head_mask):
    C, W = a_list[0].shape
    bdiag = lambda t: _head_blockdiag(t, head_mask)
    ii = _iota2((C, W), 0)
    jj = _iota2((C, W), 1) & (C - 1)
    shift = int(math.log2(RWKV_SUB))
    same_blk = (ii >> shift) == (jj >> shift)
    eye = (ii == jj).astype(F32)
    d = [jnp.where(same_blk, a, 0.0) for a in a_list]
    low = [a - di for a, di in zip(a_list, d)]
    x = [eye + di for di in d]
    dp = d
    bd = [bdiag(t) for t in dp]
    for _ in range(shift - 1):
        dp = [_bmm(t, b) for t, b in zip(dp, bd)]
        yield
        bd = [bdiag(t) for t in dp]
        x = [xi + _bmm(xi, b) for xi, b in zip(x, bd)]
        yield
    n = [_bmm(xi, bdiag(lo)) for xi, lo in zip(x, low)]
    yield
    y = [eye + ni for ni in n]
    npow = n
    for _ in range(int(math.log2(C // RWKV_SUB)) - 1):
        bd = [bdiag(t) for t in npow]
        npow = [_bmm(t, b) for t, b in zip(npow, bd)]
        yield
        y = [yi + _bmm(yi, bdiag(t)) for yi, t in zip(y, npow)]
        yield
    return [_bmm(yi, bdiag(xi)) for yi, xi in zip(y, x)]


RWKV_STASH = (
    ("ar", lambda C, tb: (2 * C, MIX_W), BF16, True),
    ("vc", lambda C, tb: (C, MIX_W), BF16, True),
    ("bd_v", lambda C, tb: (MIX_W, MIX_W), BF16, True),
    ("a_r", lambda C, tb: (C, 2 * MIX_W), BF16, True),
    ("bk_end", lambda C, tb: (2 * C, MIX_W), BF16, True),
    ("t_inv", lambda C, tb: (C, MIX_W), BF16, True),
    ("akv", lambda C, tb: (C, MIX_W), F32, True),
    ("p_end", lambda C, tb: (1, MIX_W), F32, True),
    ("g", lambda C, tb: (tb, MIX_W), F32, False),
    ("bonus", lambda C, tb: (tb, MIX_W), F32, False),
)


def _interleave(*generators):
    alive = list(generators)
    while alive:
        for gen in list(alive):
            try:
                next(gen)
            except StopIteration:
                alive.remove(gen)


def _rwkv_kernel(c_ref, mu_ref, w0_ref, w2_ref, a0_ref, a2_ref, g2_ref, kk_ref, ka_ref, rk_ref,
                 gn_ref, o_ref, state_ref, carry_ref, y_ref, *stash_refs, tb):
    C = RWKV_CHUNK
    W = MIX_W
    n_batch = c_ref.shape[0]
    n_chunks = tb // C
    stash = {name: ref for (name, _, _, _), ref in zip(RWKV_STASH, stash_refs)}
    step = pl.program_id(0)
    n_entries = n_batch * n_chunks
    wr_chunk = (step % 2) * n_entries
    rd_chunk = ((step + 1) % 2) * n_entries
    wr_batch = (step % 2) * n_batch
    rd_batch = ((step + 1) % 2) * n_batch

    @pl.when(pl.program_id(0) == 0)
    def _():
        state_ref[...] = jnp.zeros_like(state_ref)
        carry_ref[...] = jnp.zeros_like(carry_ref)
        for ref in stash_refs:
            ref[...] = jnp.zeros_like(ref)

    head_mask = _head_mask()
    bdiag = lambda t: _head_blockdiag(t, head_mask)
    ii = _iota2((C, W), 0)
    jj = _iota2((C, W), 1) & (C - 1)
    strict = ii > jj
    incl = ii >= jj
    tril = (_iota2((C, C), 0) >= _iota2((C, C), 1)).astype(BF16)
    lane_head = _iota2((HEAD_DIM, W), 1) >> int(math.log2(HEAD_DIM))
    prepared = {}

    def prepare():
        post = []
        pre = []
        for b in range(n_batch):
            cur = c_ref[b]
            prev = jnp.where(_iota2(cur.shape, 0) == 0, carry_ref[b, 0:1, :],
                             pltpu.roll(cur, 1, 0))
            carry_ref[b, 0:1, :] = cur[tb - 1:tb, :]
            mixed = cur + (prev - cur) * mu_ref[...]
            r = mixed[:, 0:W]
            k = mixed[:, W:2 * W]
            v = mixed[:, 2 * W:3 * W]
            xw = mixed[:, 3 * W:3 * W + DECAY_LORA]
            xa = mixed[:, 3 * W + DECAY_LORA:3 * W + DECAY_LORA + AAA_LORA]
            xg = mixed[:, 3 * W + DECAY_LORA + AAA_LORA:]

            log_w = -math.exp(-0.5) * _sigmoid(w0_ref[...] + _mm(jnp.tanh(xw), w2_ref[...]))
            a = _sigmoid(a0_ref[...] + _mm(xa, a2_ref[...]))
            g = _mm(_sigmoid(xg), g2_ref[...])
            kk = k * kk_ref[...]
            k2 = k * (1.0 + (a - 1.0) * ka_ref[...])
            kk = kk / jnp.maximum(jnp.sqrt(_head_sum(kk * kk, head_mask)), 1e-12)
            a_vec = -kk
            b_vec = kk * a
            post.append((g, _head_sum(r * k2 * rk_ref[...], head_mask) * v))
            yield

            for c in range(n_chunks):
                rows = slice(c * C, (c + 1) * C)
                lw = log_w[rows]
                cum = _mask_mm_f32(tril, lw)
                e_pos = jnp.exp(cum)
                e_neg = jnp.exp(-cum)
                p_end = e_pos[C - 1:C, :]
                at = a_vec[rows] * jnp.exp(cum - lw)
                rt = r[rows] * e_pos
                bt = b_vec[rows] * e_neg
                kt = k2[rows] * e_neg
                ar = jnp.concatenate([at, rt], axis=0)
                gb = _bmm_nt(ar, bdiag(bt))
                gk = _bmm_nt(ar, bdiag(kt))
                vc = v[rows]
                pre.append(dict(
                    ar=ar, p_end=p_end, vc=vc, bd_v=bdiag(vc),
                    a_ab=jnp.where(strict, gb[0:C], 0.0), a_ak=jnp.where(strict, gk[0:C], 0.0),
                    a_r=jnp.concatenate([jnp.where(incl, gb[C:], 0.0),
                                         jnp.where(incl, gk[C:], 0.0)], axis=1),
                    bk_end=jnp.concatenate([bt * p_end, kt * p_end], axis=0)))
                yield
        t_inv = yield from _unit_lower_inverse([p["a_ab"] for p in pre], head_mask)
        for p, t in zip(pre, t_inv):
            p["t_inv"] = t
            p["akv"] = _bmm(p["a_ak"], p["bd_v"])
        prepared["pre"] = pre
        prepared["post"] = post

    def recur():
        s_all = [state_ref[b] for b in range(n_batch)]
        batches = range(n_batch)
        for c in range(n_chunks):
            ent = [rd_chunk + b * n_chunks + c for b in batches]
            ars = [_bmm_nt(stash["ar"][e], bdiag(s_all[b])) for b, e in zip(batches, ent)]
            yield
            u = [_bmm(stash["t_inv"][e], bdiag(ars[b][0:C] + stash["akv"][e]))
                 for b, e in zip(batches, ent)]
            yield
            for b, e in zip(batches, ent):
                uv = jnp.concatenate([u[b].astype(BF16), stash["vc"][e]], axis=0)
                y_ref[b, c * C:(c + 1) * C, :] = ars[b][C:] + jnp.dot(
                    stash["a_r"][e], jnp.concatenate([bdiag(u[b]), stash["bd_v"][e]], axis=0),
                    preferred_element_type=F32)
                full = _mm_tn(uv, stash["bk_end"][e])
                upd = full[0:HEAD_DIM]
                for h in range(1, N_HEADS):
                    upd = jnp.where(lane_head == h, full[h * HEAD_DIM:(h + 1) * HEAD_DIM], upd)
                s_all[b] = s_all[b] * stash["p_end"][e] + upd
            yield
        for b in range(n_batch):
            state_ref[b] = s_all[b]
            y = y_ref[b]
            yc = y - _head_sum(y, head_mask) * (1.0 / HEAD_DIM)
            var = _head_sum(yc * yc, head_mask) * (1.0 / HEAD_DIM)
            yn = yc * lax.rsqrt(var + RWKV_GN_EPS) * gn_ref[...]
            o_ref[b] = ((yn + stash["bonus"][rd_batch + b])
                        * stash["g"][rd_batch + b]).astype(o_ref.dtype)
            yield

    _interleave(prepare(), recur())

    for e, p in enumerate(prepared["pre"]):
        for name, _, dtype, per_chunk in RWKV_STASH:
            if per_chunk:
                stash[name][wr_chunk + e] = p[name].astype(dtype)
    for b, (g, bonus) in enumerate(prepared["post"]):
        stash["g"][wr_batch + b] = g
        stash["bonus"][wr_batch + b] = bonus


def _rwkv(rwkv_c, mu, w0, w2, a0, a2, g2, k_k, k_a, r_k, gn, tb=256):
    B, S, W = rwkv_c.shape
    n_blocks = S // tb
    n_entries = B * (tb // RWKV_CHUNK)
    vec = lambda t: _resident((1, t.shape[-1]))
    mat = lambda t: _resident(t.shape)
    stash_shapes = [pltpu.VMEM((2 * (n_entries if per_chunk else B),) + shape(RWKV_CHUNK, tb), dtype)
                    for _, shape, dtype, per_chunk in RWKV_STASH]
    return pl.pallas_call(
        functools.partial(_rwkv_kernel, tb=tb),
        out_shape=jax.ShapeDtypeStruct((B, S, MIX_W), BF16),
        grid=(n_blocks + 1,),
        in_specs=[pl.BlockSpec((B, tb, W), lambda i: (0, jnp.minimum(i, n_blocks - 1), 0)),
                  vec(mu), vec(w0), mat(w2), vec(a0), mat(a2), mat(g2), vec(k_k), vec(k_a),
                  vec(r_k), vec(gn)],
        out_specs=pl.BlockSpec((B, tb, MIX_W), lambda i: (0, jnp.maximum(i - 1, 0), 0)),
        scratch_shapes=[pltpu.VMEM((B, HEAD_DIM, MIX_W), F32),
                        pltpu.VMEM((B, 8, W), F32),
                        pltpu.VMEM((B, tb, MIX_W), F32)] + stash_shapes,
        compiler_params=_cparams(("arbitrary",)),
        name="rwkv7",
    )(rwkv_c, mu, w0, w2, a0, a2, g2, k_k, k_a, r_k, gn)


def _merge_ffn_kernel(x_ref, gate_ref, yret_ref, yfox_ref, yrwkv_ref, pret_ref, pfox_ref, prwkv_ref,
                      wout_ref, g_ref, wgu_ref, wd_ref, o_ref, *, d_model, d_ff, ck):
    D = d_model
    dot = lambda a, b: jnp.dot(a, b, preferred_element_type=F32)
    gates = gate_ref[...]
    fox = lax.dot_general(yfox_ref[...], pfox_ref[...], (((0,), (0,)), ((), ())),
                          preferred_element_type=F32)
    merged = (_sigmoid(gates[:, 0:D]) * dot(yret_ref[...], pret_ref[...])
              + _sigmoid(gates[:, D:2 * D]) * fox
              + _sigmoid(gates[:, 2 * D:3 * D]) * dot(yrwkv_ref[...], prwkv_ref[...]))
    x = x_ref[...] + dot(merged.astype(BF16), wout_ref[...])

    h = (x * lax.rsqrt(jnp.mean(x * x, axis=-1, keepdims=True) + NORM_EPS) * g_ref[...]).astype(BF16)
    acc = x
    for c in range(d_ff // ck):
        lo = c * ck
        gate = dot(h, wgu_ref[:, lo:lo + ck])
        up = dot(h, wgu_ref[:, d_ff + lo:d_ff + lo + ck])
        act = (gate * _sigmoid(gate) * up).astype(BF16)
        acc = acc + dot(act, wd_ref[lo:lo + ck, :])
    o_ref[...] = acc


def _merge_ffn(x, gates, y_ret, y_fox, y_rwkv, p_ret, p_fox, p_rwkv, w_out, gain, w_gate_up, w_down,
               layer, tm=512, ck=256):
    B, S, D = x.shape
    d_ff = w_down.shape[1]
    row = lambda w: pl.BlockSpec((None, tm, w), lambda b, i: (b, i, 0))
    stacked = lambda w: _resident(w.shape[1:], layer)
    weights = (p_ret, p_fox, p_rwkv, w_out, gain, w_gate_up, w_down)
    weight_specs = [stacked(p_ret), stacked(p_fox), stacked(p_rwkv), stacked(w_out),
                    _resident(gain.shape), stacked(w_gate_up), stacked(w_down)]
    return pl.pallas_call(
        functools.partial(_merge_ffn_kernel, d_model=D, d_ff=d_ff, ck=ck),
        out_shape=jax.ShapeDtypeStruct((B, S, D), F32),
        grid=(B, S // tm),
        in_specs=[row(D), row(3 * D), row(MIX_W),
                  pl.BlockSpec((None, MIX_W, tm), lambda b, i: (b, 0, i)),
                  row(MIX_W)] + weight_specs,
        out_specs=row(D),
        compiler_params=_cparams(("parallel", "parallel")),
        name="merge_ffn",
    )(x, gates, y_ret, y_fox, y_rwkv, *weights)


def kernel(x, mix_norm, w_in, ret_gn, fox_q_norm, fox_k_norm, fox_f_bias, rwkv_mu, rwkv_w0, rwkv_w2,
           rwkv_a0, rwkv_a2, rwkv_g2, rwkv_k_k, rwkv_k_a, rwkv_r_k, rwkv_gn, p_ret, p_fox, p_rwkv,
           w_out, ffn_norm, w_gate_up, w_down):
    B, S, D = x.shape
    depth = w_in.shape[0]
    cos, sin = _rope_tables(S)
    o0, o1, o2, o3 = (int(v) for v in IN_OFF)
    f_lo = o1 + FOXQKV_COLS
    row = lambda t: t.reshape(1, -1)
    p_ret, p_fox, p_rwkv, w_out, w_gate_up, w_down = (
        t.astype(BF16) for t in (p_ret, p_fox, p_rwkv, w_out, w_gate_up, w_down))
    for l in range(depth):
        w_cat = _w_in_relayout(w_in[l], f_lo, N_HEADS)
        ret_c, fox_c, rwkv_c, gate_c, f_c = _in_proj(x, row(mix_norm[l]), w_cat)

        y_ret = _retention(ret_c, cos, sin, row(ret_gn[l]))

        f_bias = jnp.pad(jnp.tile(fox_f_bias[l], FOX_SPLIT),
                         (0, LANES - FOX_SPLIT * N_HEADS)).reshape(1, LANES)
        fq, fk, fv = _fox_prep(fox_c, f_c, f_bias, jnp.tile(row(fox_q_norm[l]), (1, N_HEADS)),
                               jnp.tile(row(fox_k_norm[l]), (1, N_HEADS)))
        y_fox = _fox_attn(fq, fk, fv).reshape(B, MIX_W, S)

        y_rwkv = _rwkv(rwkv_c, row(rwkv_mu[l]), row(rwkv_w0[l]), rwkv_w2[l].astype(BF16),
                       row(rwkv_a0[l]), rwkv_a2[l].astype(BF16), rwkv_g2[l].astype(BF16),
                       row(rwkv_k_k[l]), row(rwkv_k_a[l]), row(rwkv_r_k[l]), row(rwkv_gn[l]))

        x = _merge_ffn(x, gate_c, y_ret, y_fox, y_rwkv, p_ret, p_fox, p_rwkv, w_out,
                       row(ffn_norm[l]), w_gate_up, w_down, l)
    return x
```

```python
import functools
import math

import jax
import jax.numpy as jnp
import numpy as np
from jax import lax
from jax.experimental import pallas as pl
from jax.experimental.pallas import tpu as pltpu

F32 = jnp.float32
BF16 = jnp.bfloat16

HEAD_DIM = 64
N_HEADS = 4
MIX_W = N_HEADS * HEAD_DIM
RET_CHUNK = 128
ROPE_BASE = 10000.0
DECAY_LORA = 64
AAA_LORA = 64
GATE_LORA = 128
NORM_EPS = 1e-6
RET_GN_EPS = 1e-5
RWKV_GN_EPS = 64e-5
RWKV_CHUNK = 64
RWKV_SUB = 16
LANES = 128
FOX_SPLIT = 3
VMEM_LIMIT = 52 * 1024 * 1024


def _cparams(sem):
    return pltpu.CompilerParams(dimension_semantics=sem, vmem_limit_bytes=VMEM_LIMIT)


def _resident(shape, layer=None):
    nd = len(shape)
    if layer is None:
        return pl.BlockSpec(shape, lambda *_: (0,) * nd, pipeline_mode=pl.Buffered(1))
    return pl.BlockSpec((None,) + tuple(shape), lambda *_: (layer,) + (0,) * nd,
                        pipeline_mode=pl.Buffered(1))


def _mm(a, b):
    return jnp.dot(a.astype(BF16), b.astype(BF16), preferred_element_type=F32)


def _mm_nt(a, b):
    return lax.dot_general(a.astype(BF16), b.astype(BF16), (((1,), (1,)), ((), ())),
                           preferred_element_type=F32)


def _mm_tn(a, b):
    return lax.dot_general(a.astype(BF16), b.astype(BF16), (((0,), (0,)), ((), ())),
                           preferred_element_type=F32)


def _mask_mm_f32(mask_bf16, x):
    h1 = x.astype(BF16)
    r1 = x - h1.astype(F32)
    h2 = r1.astype(BF16)
    h3 = (r1 - h2.astype(F32)).astype(BF16)
    dot = lambda h: jnp.dot(mask_bf16, h, preferred_element_type=F32)
    return dot(h1) + dot(h2) + dot(h3)


def _sigmoid(x):
    return 0.5 * jnp.tanh(0.5 * x) + 0.5


def _log_sigmoid(x):
    return jnp.minimum(x, 0.0) - jnp.log1p(jnp.exp(-jnp.abs(x)))


def _iota2(shape, dim):
    return lax.broadcasted_iota(jnp.int32, shape, dim)


def _head_mask():
    same = (_iota2((MIX_W, MIX_W), 0) >> 6) == (_iota2((MIX_W, MIX_W), 1) >> 6)
    return jnp.where(same, 1.0, 0.0).astype(BF16)


def _head_blockdiag(x, head_mask):
    xb = x.astype(BF16)
    return jnp.concatenate([xb] * N_HEADS, axis=0) * head_mask


def _head_sum(x, head_mask):
    hi = x.astype(BF16)
    lo = (x - hi.astype(F32)).astype(BF16)
    dot = lambda t: jnp.dot(t, head_mask, preferred_element_type=F32)
    return dot(hi) + dot(lo)


def _bmm(a, bd):
    return jnp.dot(a.astype(BF16), bd, preferred_element_type=F32)


def _bmm_nt(a, bd):
    return lax.dot_general(a.astype(BF16), bd, (((1,), (1,)), ((), ())), preferred_element_type=F32)


RET_COLS = 4 * MIX_W
FOXQKV_COLS = 3 * MIX_W
RWKV_COLS = 3 * MIX_W + DECAY_LORA + AAA_LORA + GATE_LORA
IN_OFF = np.cumsum([0, RET_COLS, FOXQKV_COLS, RWKV_COLS])


def _w_in_relayout_kernel(w_ref, o_ref, *, f_lo, n_f):
    ncol = w_ref.shape[1]
    rest = ncol - (f_lo + n_f)
    o_ref[:, 0:f_lo] = w_ref[:, 0:f_lo].astype(BF16)
    o_ref[:, f_lo:f_lo + rest] = w_ref[:, f_lo + n_f:ncol].astype(BF16)
    win = w_ref[:, f_lo:f_lo + LANES]
    lane = _iota2(win.shape, 1)
    reps = jnp.where(lane < n_f, win, 0.0)
    for t in range(1, FOX_SPLIT):
        reps = jnp.where((lane >= t * n_f) & (lane < (t + 1) * n_f), pltpu.roll(win, t * n_f, 1), reps)
    o_ref[:, f_lo + rest:] = reps.astype(BF16)


def _w_in_relayout(w, f_lo, n_f, tr=128):
    D, ncol = w.shape
    out_cols = ncol - n_f + LANES
    return pl.pallas_call(
        functools.partial(_w_in_relayout_kernel, f_lo=f_lo, n_f=n_f),
        out_shape=jax.ShapeDtypeStruct((D, out_cols), BF16),
        grid=(D // tr,),
        in_specs=[pl.BlockSpec((tr, ncol), lambda i: (i, 0))],
        out_specs=pl.BlockSpec((tr, out_cols), lambda i: (i, 0)),
        compiler_params=_cparams(("parallel",)),
        name="w_in_relayout",
    )(w)


def _in_proj_kernel(x_ref, g_ref, w_ref, ret_ref, fox_ref, rwkv_ref, gate_ref, f_ref, *, d_model):
    x = x_ref[...]
    h = x * lax.rsqrt(jnp.mean(x * x, axis=-1, keepdims=True) + NORM_EPS) * g_ref[...]
    hb = h.astype(BF16)
    o0, o1, o2, o3 = (int(v) for v in IN_OFF)
    o4 = o3 + 3 * d_model
    dot = lambda lo, hi: jnp.dot(hb, w_ref[:, lo:hi], preferred_element_type=F32)
    ret_ref[...] = dot(o0, o1)
    fox_ref[...] = dot(o1, o2)
    rwkv_ref[...] = dot(o2, o3)
    gate_ref[...] = dot(o3, o4)
    f_ref[...] = dot(o4, o4 + LANES)


def _in_proj(x, gain, w_cat, tm=512):
    B, S, D = x.shape
    ncol = w_cat.shape[1]
    row = lambda w: pl.BlockSpec((None, tm, w), lambda b, i: (b, i, 0))
    widths = (RET_COLS, FOXQKV_COLS, RWKV_COLS, 3 * D, LANES)
    return pl.pallas_call(
        functools.partial(_in_proj_kernel, d_model=D),
        out_shape=tuple(jax.ShapeDtypeStruct((B, S, w), F32) for w in widths),
        grid=(B, S // tm),
        in_specs=[row(D), _resident((1, D)), _resident((D, ncol))],
        out_specs=tuple(row(w) for w in widths),
        compiler_params=_cparams(("parallel", "parallel")),
        name="in_proj",
    )(x, gain, w_cat)


def _retention_kernel(c_ref, cos_ref, sin_ref, gn_ref, dmat_ref, qdec_ref, kdec_ref, cdec_ref, o_ref,
                      state_ref, *, tb):
    C = RET_CHUNK
    n_batch = c_ref.shape[0]

    @pl.when(pl.program_id(0) == 0)
    def _():
        state_ref[...] = jnp.zeros_like(state_ref)

    head_mask = _head_mask()
    first_half = (_iota2((tb, MIX_W), 1) & (HEAD_DIM - 1)) < HEAD_DIM // 2
    kv_mask = jnp.where((_iota2((N_HEADS * C, MIX_W), 0) >> int(math.log2(C)))
                        == (_iota2((N_HEADS * C, MIX_W), 1) >> int(math.log2(HEAD_DIM))),
                        1.0, 0.0).astype(BF16)
    lane_head = _iota2((HEAD_DIM, MIX_W), 1) >> int(math.log2(HEAD_DIM))
    cos = jnp.concatenate([cos_ref[...]] * (MIX_W // LANES), axis=1)
    sin = jnp.concatenate([sin_ref[...]] * (MIX_W // LANES), axis=1)

    def rope(t):
        partner = jnp.where(first_half, pltpu.roll(t, MIX_W - HEAD_DIM // 2, 1),
                            pltpu.roll(t, HEAD_DIM // 2, 1))
        return t * cos + partner * sin

    def stack_heads(t):
        return jnp.concatenate([t.astype(BF16)] * N_HEADS, axis=0) * kv_mask

    for b in range(n_batch):
        blk = c_ref[b]
        q = rope(blk[:, 0:MIX_W])
        k = rope(blk[:, MIX_W:2 * MIX_W]) * (HEAD_DIM ** -0.5)
        v = blk[:, 2 * MIX_W:3 * MIX_W]
        g = blk[:, 3 * MIX_W:4 * MIX_W]
        state = state_ref[b]
        ys = []
        for c in range(tb // C):
            rows = slice(c * C, (c + 1) * C)
            qc, kc, vc = q[rows], k[rows], v[rows]
            scores = _bmm_nt(qc, stack_heads(kc)) * dmat_ref[...]
            inner = _bmm(scores, stack_heads(vc))
            cross = _bmm(qc * qdec_ref[...], _head_blockdiag(state, head_mask))
            ys.append(inner + cross)
            full = _mm_tn(kc * kdec_ref[...], vc)
            upd = full[0:HEAD_DIM]
            for h in range(1, N_HEADS):
                upd = jnp.where(lane_head == h, full[h * HEAD_DIM:(h + 1) * HEAD_DIM], upd)
            state = state * cdec_ref[...] + upd
        state_ref[b] = state
        y = jnp.concatenate(ys, axis=0)
        yc = y - _head_sum(y, head_mask) * (1.0 / HEAD_DIM)
        var = _head_sum(yc * yc, head_mask) * (1.0 / HEAD_DIM)
        yn = yc * lax.rsqrt(var + RET_GN_EPS)
        o_ref[b] = (g * _sigmoid(g) * (yn * gn_ref[...])).astype(o_ref.dtype)


def _retention_tables():
    C = RET_CHUNK
    log_gamma = jnp.log1p(-jnp.exp2(-5.0 - jnp.arange(N_HEADS, dtype=F32)))
    idx = jnp.arange(C, dtype=F32)
    dist = idx[:, None] - idx[None, :]
    dmat = jnp.where(dist >= 0, jnp.exp(log_gamma[:, None, None] * jnp.maximum(dist, 0.0)), 0.0)
    dmat = dmat.transpose(1, 0, 2).reshape(C, N_HEADS * C)
    lanes = lambda t: jnp.repeat(t, HEAD_DIM, axis=-1)
    q_dec = lanes(jnp.exp(log_gamma[None, :] * (idx[:, None] + 1.0)))
    k_dec = lanes(jnp.exp(log_gamma[None, :] * (C - 1.0 - idx[:, None])))
    c_dec = lanes(jnp.exp(log_gamma * C)[None, :])
    return dmat, q_dec, k_dec, c_dec


def _retention(ret_c, cos, sin, gn, tb=256):
    B, S, W = ret_c.shape
    tables = _retention_tables()
    return pl.pallas_call(
        functools.partial(_retention_kernel, tb=tb),
        out_shape=jax.ShapeDtypeStruct((B, S, MIX_W), BF16),
        grid=(S // tb,),
        in_specs=[pl.BlockSpec((B, tb, W), lambda i: (0, i, 0)),
                  pl.BlockSpec((tb, LANES), lambda i: (i, 0)),
                  pl.BlockSpec((tb, LANES), lambda i: (i, 0)),
                  _resident((1, MIX_W))] + [_resident(t.shape) for t in tables],
        out_specs=pl.BlockSpec((B, tb, MIX_W), lambda i: (0, i, 0)),
        scratch_shapes=[pltpu.VMEM((B, HEAD_DIM, MIX_W), F32)],
        compiler_params=_cparams(("arbitrary",)),
        name="retention",
    )(ret_c, cos, sin, gn, *tables)


def _rope_tables(S):
    half = HEAD_DIM // 2
    inv_freq = ROPE_BASE ** (-jnp.arange(half, dtype=F32) / half)
    ang = jnp.arange(S, dtype=F32)[:, None] * inv_freq[None, :]
    cos, sin = jnp.cos(ang), jnp.sin(ang)
    cos_h = jnp.concatenate([cos, cos], axis=-1)
    sin_h = jnp.concatenate([-sin, sin], axis=-1)
    reps = LANES // HEAD_DIM
    return jnp.tile(cos_h, (1, reps)), jnp.tile(sin_h, (1, reps))


LOG2E = math.log2(math.e)
FOX_AUG = HEAD_DIM
FOX_TQ = 512
FOX_TK = 256
FOX_VROWS = 80
FOX_UNROLL = 8


def _fox_place_tables():
    H, Dh, L = N_HEADS, HEAD_DIM, LANES
    place_q = np.zeros((H // 2, MIX_W + L, 2 * L), np.float32)
    place_k = np.zeros((H // 2, MIX_W + L, 2 * L), np.float32)
    place_vt = np.zeros((H, FOX_VROWS, MIX_W), np.float32)
    for h in range(H):
        out = (h % 2) * L
        for d in range(Dh):
            place_q[h // 2, h * Dh + d, out + d] = 1.0
            place_k[h // 2, h * Dh + d, out + d] = 1.0
            place_vt[h, d, h * Dh + d] = 1.0
        for t in range(FOX_SPLIT):
            place_q[h // 2, MIX_W + t * H + h, out + FOX_AUG + t] = 1.0
            place_k[h // 2, MIX_W + t * H + h, out + FOX_AUG + FOX_SPLIT + t] = -1.0
    return tuple(jnp.asarray(t, BF16) for t in (place_q, place_k, place_vt))


def _fox_prep_kernel(c_ref, f_ref, bias_ref, qg_ref, kg_ref, pq_ref, pk_ref, pvt_ref, q_ref, k_ref,
                     vt_ref, carry_ref, *, ts, tk):
    @pl.when(pl.program_id(1) == 0)
    def _():
        carry_ref[...] = jnp.zeros_like(carry_ref)

    log_f = _log_sigmoid(f_ref[...] + bias_ref[...])
    tril = (_iota2((LANES, LANES), 0) >= _iota2((LANES, LANES), 1)).astype(BF16)
    carry = carry_ref[0:1, :]
    parts = []
    for r in range(0, ts, LANES):
        part = _mask_mm_f32(tril, log_f[r:r + LANES]) + carry
        carry = part[LANES - 1:LANES, :]
        parts.append(part)
    carry_ref[0:1, :] = carry
    c = jnp.concatenate(parts, axis=0) * LOG2E
    lane_c = _iota2(c.shape, 1)
    csel = jnp.zeros(c.shape, BF16)
    rest = c
    for t in range(FOX_SPLIT):
        term = rest.astype(BF16)
        csel = jnp.where((lane_c >= t * N_HEADS) & (lane_c < (t + 1) * N_HEADS), term, csel)
        rest = rest - term.astype(F32)

    blk = c_ref[...]
    head_mask = _head_mask()

    def rms(t, gain):
        ms = _head_sum(t * t, head_mask) * (1.0 / HEAD_DIM)
        return t * lax.rsqrt(ms + NORM_EPS) * gain

    qn = rms(blk[:, 0:MIX_W], qg_ref[...]) * (LOG2E * HEAD_DIM ** -0.5)
    kn = rms(blk[:, MIX_W:2 * MIX_W], kg_ref[...])
    q_stack = jnp.concatenate([qn.astype(BF16), csel], axis=1)
    k_stack = jnp.concatenate([kn.astype(BF16), csel], axis=1)
    vb = blk[:, 2 * MIX_W:3 * MIX_W].astype(BF16)

    lane = _iota2((1, 2 * LANES), 1) & (LANES - 1)
    ones_q = jnp.where((lane >= FOX_AUG + FOX_SPLIT) & (lane < FOX_AUG + 2 * FOX_SPLIT), 1.0, 0.0)
    ones_k = jnp.where((lane >= FOX_AUG) & (lane < FOX_AUG + FOX_SPLIT), 1.0, 0.0)
    ones_row = _iota2((FOX_VROWS, ts), 0) == FOX_AUG

    for hp in range(N_HEADS // 2):
        q2 = (jnp.dot(q_stack, pq_ref[hp], preferred_element_type=F32) + ones_q).astype(BF16)
        k2 = (jnp.dot(k_stack, pk_ref[hp], preferred_element_type=F32) + ones_k).astype(BF16)
        for sub in range(2):
            q_ref[2 * hp + sub] = q2[:, sub * LANES:(sub + 1) * LANES]
            k_ref[2 * hp + sub] = k2[:, sub * LANES:(sub + 1) * LANES]
    for h in range(N_HEADS):
        v_t = jnp.where(ones_row, 1.0, _mm_nt(pvt_ref[h], vb))
        for kb in range(ts // tk):
            vt_ref[h, kb] = v_t[:, kb * tk:(kb + 1) * tk].astype(BF16)


def _fox_prep(fox_c, f_c, f_bias, q_gain, k_gain, ts=512):
    B, S, W = fox_c.shape
    tk = FOX_TK
    tables = _fox_place_tables()
    head_spec = pl.BlockSpec((None, N_HEADS, ts, LANES), lambda b, i: (b, 0, i, 0))
    shp = jax.ShapeDtypeStruct((B, N_HEADS, S, LANES), BF16)
    return pl.pallas_call(
        functools.partial(_fox_prep_kernel, ts=ts, tk=tk),
        out_shape=(shp, shp, jax.ShapeDtypeStruct((B, N_HEADS, S // tk, FOX_VROWS, tk), BF16)),
        grid=(B, S // ts),
        in_specs=[pl.BlockSpec((None, ts, W), lambda b, i: (b, i, 0)),
                  pl.BlockSpec((None, ts, LANES), lambda b, i: (b, i, 0)),
                  _resident((1, LANES)), _resident((1, MIX_W)), _resident((1, MIX_W))]
                 + [_resident(t.shape) for t in tables],
        out_specs=(head_spec, head_spec,
                   pl.BlockSpec((None, N_HEADS, ts // tk, FOX_VROWS, tk),
                                lambda b, i: (b, 0, i, 0, 0))),
        scratch_shapes=[pltpu.VMEM((8, LANES), F32)],
        compiler_params=_cparams(("parallel", "arbitrary")),
        name="fox_prep",
    )(fox_c, f_c, f_bias, q_gain, k_gain, *tables)


def _fox_attn_kernel(q_ref, k_ref, vt_ref, o_ref, s0_ref, s1_ref, p0_ref, p1_ref, m_ref, alpha_ref,
                     acc_ref, *, tq, tk):
    qi = pl.program_id(2)
    m_ref[...] = jnp.full(m_ref.shape, -jnp.inf, F32)
    acc_ref[...] = jnp.zeros_like(acc_ref)
    p1_ref[...] = jnp.zeros_like(p1_ref)

    def qk(j, s_ref):
        start = pl.multiple_of(j * tk, tk)
        s_ref[...] = _mm_nt(k_ref[pl.ds(start, tk), :], q_ref[...])

    def pv(j, p_ref):
        return jnp.dot(vt_ref[j], p_ref[...], preferred_element_type=F32)

    def softmax(s_ref, p_ref, key_offset):
        for c in range(0, tq, LANES):
            cols = slice(c, c + LANES)
            s = s_ref[:, cols]
            if key_offset is not None and key_offset + tk - 1 > c:
                keep = _iota2((tk, LANES), 0) + key_offset <= _iota2((tk, LANES), 1) + c
                s = jnp.where(keep, s, -jnp.inf)
            m_old = m_ref[:, cols]
            m_new = jnp.maximum(m_old, jnp.max(s, axis=0, keepdims=True))
            p_ref[:, cols] = jnp.exp2(s - m_new).astype(BF16)
            m_ref[:, cols] = m_new
            alpha_ref[:, cols] = jnp.exp2(m_old - m_new)

    def advance(pv_prev):
        acc_ref[...] = (acc_ref[...] + pv_prev) * alpha_ref[...]

    qk(0, s0_ref)

    def pair(i):
        j = 2 * i
        qk(j + 1, s1_ref)
        pv_prev = pv(jnp.maximum(j - 1, 0), p1_ref)
        softmax(s0_ref, p0_ref, None)
        advance(pv_prev)
        qk(j + 2, s0_ref)
        pv_prev = pv(j, p0_ref)
        softmax(s1_ref, p1_ref, None)
        advance(pv_prev)

    done = 0
    unroll = FOX_UNROLL
    while unroll >= 1:
        def group(i, carry, unroll=unroll):
            for u in range(unroll):
                pair(i + u)
            return carry

        n_groups = (qi - done) // unroll
        lax.fori_loop(0, n_groups, lambda g, c, done=done, unroll=unroll: group(done + g * unroll, c),
                      0)
        done = done + n_groups * unroll
        unroll //= 2

    j = 2 * qi
    qk(j + 1, s1_ref)
    pv_prev = pv(jnp.maximum(j - 1, 0), p1_ref)
    softmax(s0_ref, p0_ref, 0)
    advance(pv_prev)
    pv_prev = pv(j, p0_ref)
    softmax(s1_ref, p1_ref, tk)
    advance(pv_prev)
    acc = acc_ref[...] + pv(j + 1, p1_ref)
    o_ref[...] = (acc[0:HEAD_DIM] / acc[FOX_AUG:FOX_AUG + 1]).astype(o_ref.dtype)


def _fox_attn(q, k, vt):
    B, H, S, W = q.shape
    tq, tk = FOX_TQ, FOX_TK
    assert tq == 2 * tk and S % tq == 0
    return pl.pallas_call(
        functools.partial(_fox_attn_kernel, tq=tq, tk=tk),
        out_shape=jax.ShapeDtypeStruct((B, H, HEAD_DIM, S), BF16),
        grid=(B, H, S // tq),
        in_specs=[pl.BlockSpec((None, None, tq, W), lambda b, h, i: (b, h, i, 0)),
                  pl.BlockSpec((None, None, S, W), lambda b, h, i: (b, h, 0, 0)),
                  pl.BlockSpec((None, None, S // tk, FOX_VROWS, tk),
                               lambda b, h, i: (b, h, 0, 0, 0))],
        out_specs=pl.BlockSpec((None, None, HEAD_DIM, tq), lambda b, h, i: (b, h, 0, i)),
        scratch_shapes=[pltpu.VMEM((tk, tq), F32), pltpu.VMEM((tk, tq), F32),
                        pltpu.VMEM((tk, tq), BF16), pltpu.VMEM((tk, tq), BF16),
                        pltpu.VMEM((1, tq), F32), pltpu.VMEM((1, tq), F32),
                        pltpu.VMEM((FOX_VROWS, tq), F32)],
        compiler_params=_cparams(("parallel", "parallel", "arbitrary")),
        name="fox_attn",
    )(q, k, vt)


def _unit_lower_inverse(a_list, head_mask):
    C, W = a_list[0].shape
    bdiag = lambda t: _head_blockdiag(t, head_mask)
    ii = _iota2((C, W), 0)
    jj = _iota2((C, W), 1) & (C - 1)
    shift = int(math.log2(RWKV_SUB))
    same_blk = (ii >> shift) == (jj >> shift)
    eye = (ii == jj).astype(F32)
    d = [jnp.where(same_blk, a, 0.0) for a in a_list]
    low = [a - di for a, di in zip(a_list, d)]
    x = [eye + di for di in d]
    dp = [_bmm(di, bdiag(di)) for di in d]
    yield
    for _ in range(shift - 2):
        bd = [bdiag(t) for t in dp]
        both = [_bmm(jnp.concatenate([xi, t], axis=0), b) for xi, t, b in zip(x, dp, bd)]
        x = [xi + r[0:C] for xi, r in zip(x, both)]
        dp = [r[C:] for r in both]
        yield
    x = [xi + _bmm(xi, bdiag(t)) for xi, t in zip(x, dp)]
    yield
    n = [_bmm(xi, bdiag(lo)) for xi, lo in zip(x, low)]
    yield
    y = [eye + ni for ni in n]
    npow = n
    for _ in range(int(math.log2(C // RWKV_SUB)) - 1):
        bd = [bdiag(t) for t in npow]
        npow = [_bmm(t, b) for t, b in zip(npow, bd)]
        yield
        y = [yi + _bmm(yi, bdiag(t)) for yi, t in zip(y, npow)]
        yield
    return [_bmm(yi, bdiag(xi)) for yi, xi in zip(y, x)]


RWKV_STASH = (
    ("ar", lambda C, tb: (2 * C, MIX_W), BF16, True),
    ("vc", lambda C, tb: (C, MIX_W), BF16, True),
    ("bd_v", lambda C, tb: (MIX_W, MIX_W), BF16, True),
    ("a_r", lambda C, tb: (C, 2 * MIX_W), BF16, True),
    ("bk_end", lambda C, tb: (2 * C, MIX_W), BF16, True),
    ("t_inv", lambda C, tb: (C, MIX_W), BF16, True),
    ("akv", lambda C, tb: (C, MIX_W), F32, True),
    ("p_end", lambda C, tb: (1, MIX_W), F32, True),
    ("g", lambda C, tb: (tb, MIX_W), F32, False),
    ("bonus", lambda C, tb: (tb, MIX_W), F32, False),
)


def _interleave(*generators):
    alive = list(generators)
    while alive:
        for gen in list(alive):
            try:
                next(gen)
            except StopIteration:
                alive.remove(gen)


def _rwkv_kernel(c_ref, mu_ref, w0_ref, w2_ref, a0_ref, a2_ref, g2_ref, kk_ref, ka_ref, rk_ref,
                 gn_ref, o_ref, state_ref, carry_ref, y_ref, *stash_refs, tb):
    C = RWKV_CHUNK
    W = MIX_W
    n_batch = c_ref.shape[0]
    n_chunks = tb // C
    stash = {name: ref for (name, _, _, _), ref in zip(RWKV_STASH, stash_refs)}
    step = pl.program_id(0)
    n_entries = n_batch * n_chunks
    wr_chunk = (step % 2) * n_entries
    rd_chunk = ((step + 1) % 2) * n_entries
    wr_batch = (step % 2) * n_batch
    rd_batch = ((step + 1) % 2) * n_batch

    @pl.when(pl.program_id(0) == 0)
    def _():
        state_ref[...] = jnp.zeros_like(state_ref)
        carry_ref[...] = jnp.zeros_like(carry_ref)
        for ref in stash_refs:
            ref[...] = jnp.zeros_like(ref)

    head_mask = _head_mask()
    bdiag = lambda t: _head_blockdiag(t, head_mask)
    ii = _iota2((C, W), 0)
    jj = _iota2((C, W), 1) & (C - 1)
    strict = ii > jj
    incl = ii >= jj
    tril = (_iota2((C, C), 0) >= _iota2((C, C), 1)).astype(BF16)
    lane_head = _iota2((HEAD_DIM, W), 1) >> int(math.log2(HEAD_DIM))
    prepared = {}

    def prepare():
        post = []
        pre = []
        for b in range(n_batch):
            cur = c_ref[b]
            prev = jnp.where(_iota2(cur.shape, 0) == 0, carry_ref[b, 0:1, :],
                             pltpu.roll(cur, 1, 0))
            carry_ref[b, 0:1, :] = cur[tb - 1:tb, :]
            mixed = cur + (prev - cur) * mu_ref[...]
            r = mixed[:, 0:W]
            k = mixed[:, W:2 * W]
            v = mixed[:, 2 * W:3 * W]
            xw = mixed[:, 3 * W:3 * W + DECAY_LORA]
            xa = mixed[:, 3 * W + DECAY_LORA:3 * W + DECAY_LORA + AAA_LORA]
            xg = mixed[:, 3 * W + DECAY_LORA + AAA_LORA:]

            log_w = -math.exp(-0.5) * _sigmoid(w0_ref[...] + _mm(jnp.tanh(xw), w2_ref[...]))
            a = _sigmoid(a0_ref[...] + _mm(xa, a2_ref[...]))
            g = _mm(_sigmoid(xg), g2_ref[...])
            kk = k * kk_ref[...]
            k2 = k * (1.0 + (a - 1.0) * ka_ref[...])
            kk = kk / jnp.maximum(jnp.sqrt(_head_sum(kk * kk, head_mask)), 1e-12)
            a_vec = -kk
            b_vec = kk * a
            post.append((g, _head_sum(r * k2 * rk_ref[...], head_mask) * v))
            yield

            for c in range(n_chunks):
                rows = slice(c * C, (c + 1) * C)
                lw = log_w[rows]
                cum = _mask_mm_f32(tril, lw)
                e_pos = jnp.exp(cum)
                e_neg = jnp.exp(-cum)
                p_end = e_pos[C - 1:C, :]
                at = a_vec[rows] * jnp.exp(cum - lw)
                rt = r[rows] * e_pos
                bt = b_vec[rows] * e_neg
                kt = k2[rows] * e_neg
                ar = jnp.concatenate([at, rt], axis=0)
                gb = _bmm_nt(ar, bdiag(bt))
                gk = _bmm_nt(ar, bdiag(kt))
                vc = v[rows]
                pre.append(dict(
                    ar=ar, p_end=p_end, vc=vc, bd_v=bdiag(vc),
                    a_ab=jnp.where(strict, gb[0:C], 0.0), a_ak=jnp.where(strict, gk[0:C], 0.0),
                    a_r=jnp.concatenate([jnp.where(incl, gb[C:], 0.0),
                                         jnp.where(incl, gk[C:], 0.0)], axis=1),
                    bk_end=jnp.concatenate([bt * p_end, kt * p_end], axis=0)))
                yield
        t_inv = yield from _unit_lower_inverse([p["a_ab"] for p in pre], head_mask)
        for p, t in zip(pre, t_inv):
            p["t_inv"] = t
            p["akv"] = _bmm(p["a_ak"], p["bd_v"])
        prepared["pre"] = pre
        prepared["post"] = post

    def recur():
        s_all = [state_ref[b] for b in range(n_batch)]
        batches = range(n_batch)
        for c in range(n_chunks):
            ent = [rd_chunk + b * n_chunks + c for b in batches]
            ars = [_bmm_nt(stash["ar"][e], bdiag(s_all[b])) for b, e in zip(batches, ent)]
            yield
            u = [_bmm(stash["t_inv"][e], bdiag(ars[b][0:C] + stash["akv"][e]))
                 for b, e in zip(batches, ent)]
            yield
            for b, e in zip(batches, ent):
                uv = jnp.concatenate([u[b].astype(BF16), stash["vc"][e]], axis=0)
                y_ref[b, c * C:(c + 1) * C, :] = ars[b][C:] + jnp.dot(
                    stash["a_r"][e], jnp.concatenate([bdiag(u[b]), stash["bd_v"][e]], axis=0),
                    preferred_element_type=F32)
                full = _mm_tn(uv, stash["bk_end"][e])
                upd = full[0:HEAD_DIM]
                for h in range(1, N_HEADS):
                    upd = jnp.where(lane_head == h, full[h * HEAD_DIM:(h + 1) * HEAD_DIM], upd)
                s_all[b] = s_all[b] * stash["p_end"][e] + upd
            yield
        for b in range(n_batch):
            state_ref[b] = s_all[b]
            y = y_ref[b]
            yc = y - _head_sum(y, head_mask) * (1.0 / HEAD_DIM)
            var = _head_sum(yc * yc, head_mask) * (1.0 / HEAD_DIM)
            yn = yc * lax.rsqrt(var + RWKV_GN_EPS) * gn_ref[...]
            o_ref[b] = ((yn + stash["bonus"][rd_batch + b])
                        * stash["g"][rd_batch + b]).astype(o_ref.dtype)
            yield

    _interleave(prepare(), recur())

    for e, p in enumerate(prepared["pre"]):
        for name, _, dtype, per_chunk in RWKV_STASH:
            if per_chunk:
                stash[name][wr_chunk + e] = p[name].astype(dtype)
    for b, (g, bonus) in enumerate(prepared["post"]):
        stash["g"][wr_batch + b] = g
        stash["bonus"][wr_batch + b] = bonus


def _rwkv(rwkv_c, mu, w0, w2, a0, a2, g2, k_k, k_a, r_k, gn, tb=256):
    B, S, W = rwkv_c.shape
    n_blocks = S // tb
    n_entries = B * (tb // RWKV_CHUNK)
    vec = lambda t: _resident((1, t.shape[-1]))
    mat = lambda t: _resident(t.shape)
    stash_shapes = [pltpu.VMEM((2 * (n_entries if per_chunk else B),) + shape(RWKV_CHUNK, tb), dtype)
                    for _, shape, dtype, per_chunk in RWKV_STASH]
    return pl.pallas_call(
        functools.partial(_rwkv_kernel, tb=tb),
        out_shape=jax.ShapeDtypeStruct((B, S, MIX_W), BF16),
        grid=(n_blocks + 1,),
        in_specs=[pl.BlockSpec((B, tb, W), lambda i: (0, jnp.minimum(i, n_blocks - 1), 0)),
                  vec(mu), vec(w0), mat(w2), vec(a0), mat(a2), mat(g2), vec(k_k), vec(k_a),
                  vec(r_k), vec(gn)],
        out_specs=pl.BlockSpec((B, tb, MIX_W), lambda i: (0, jnp.maximum(i - 1, 0), 0)),
        scratch_shapes=[pltpu.VMEM((B, HEAD_DIM, MIX_W), F32),
                        pltpu.VMEM((B, 8, W), F32),
                        pltpu.VMEM((B, tb, MIX_W), F32)] + stash_shapes,
        compiler_params=_cparams(("arbitrary",)),
        name="rwkv7",
    )(rwkv_c, mu, w0, w2, a0, a2, g2, k_k, k_a, r_k, gn)


def _merge_ffn_kernel(x_ref, gate_ref, yret_ref, yfox_ref, yrwkv_ref, pret_ref, pfox_ref, prwkv_ref,
                      wout_ref, g_ref, wgu_ref, wd_ref, o_ref, *, d_model, d_ff, ck):
    D = d_model
    dot = lambda a, b: jnp.dot(a, b, preferred_element_type=F32)
    gates = gate_ref[...]
    fox = lax.dot_general(yfox_ref[...], pfox_ref[...], (((0,), (0,)), ((), ())),
                          preferred_element_type=F32)
    merged = (_sigmoid(gates[:, 0:D]) * dot(yret_ref[...], pret_ref[...])
              + _sigmoid(gates[:, D:2 * D]) * fox
              + _sigmoid(gates[:, 2 * D:3 * D]) * dot(yrwkv_ref[...], prwkv_ref[...]))
    x = x_ref[...] + dot(merged.astype(BF16), wout_ref[...])

    h = (x * lax.rsqrt(jnp.mean(x * x, axis=-1, keepdims=True) + NORM_EPS) * g_ref[...]).astype(BF16)
    acc = x
    for c in range(d_ff // ck):
        lo = c * ck
        gate = dot(h, wgu_ref[:, lo:lo + ck])
        up = dot(h, wgu_ref[:, d_ff + lo:d_ff + lo + ck])
        act = (gate * _sigmoid(gate) * up).astype(BF16)
        acc = acc + dot(act, wd_ref[lo:lo + ck, :])
    o_ref[...] = acc


def _merge_ffn(x, gates, y_ret, y_fox, y_rwkv, p_ret, p_fox, p_rwkv, w_out, gain, w_gate_up, w_down,
               layer, tm=512, ck=256):
    B, S, D = x.shape
    d_ff = w_down.shape[1]
    row = lambda w: pl.BlockSpec((None, tm, w), lambda b, i: (b, i, 0))
    stacked = lambda w: _resident(w.shape[1:], layer)
    weights = (p_ret, p_fox, p_rwkv, w_out, gain, w_gate_up, w_down)
    weight_specs = [stacked(p_ret), stacked(p_fox), stacked(p_rwkv), stacked(w_out),
                    _resident(gain.shape), stacked(w_gate_up), stacked(w_down)]
    return pl.pallas_call(
        functools.partial(_merge_ffn_kernel, d_model=D, d_ff=d_ff, ck=ck),
        out_shape=jax.ShapeDtypeStruct((B, S, D), F32),
        grid=(B, S // tm),
        in_specs=[row(D), row(3 * D), row(MIX_W),
                  pl.BlockSpec((None, MIX_W, tm), lambda b, i: (b, 0, i)),
                  row(MIX_W)] + weight_specs,
        out_specs=row(D),
        compiler_params=_cparams(("parallel", "parallel")),
        name="merge_ffn",
    )(x, gates, y_ret, y_fox, y_rwkv, *weights)


def kernel(x, mix_norm, w_in, ret_gn, fox_q_norm, fox_k_norm, fox_f_bias, rwkv_mu, rwkv_w0, rwkv_w2,
           rwkv_a0, rwkv_a2, rwkv_g2, rwkv_k_k, rwkv_k_a, rwkv_r_k, rwkv_gn, p_ret, p_fox, p_rwkv,
           w_out, ffn_norm, w_gate_up, w_down):
    B, S, D = x.shape
    depth = w_in.shape[0]
    cos, sin = _rope_tables(S)
    o0, o1, o2, o3 = (int(v) for v in IN_OFF)
    f_lo = o1 + FOXQKV_COLS
    row = lambda t: t.reshape(1, -1)
    p_ret, p_fox, p_rwkv, w_out, w_gate_up, w_down = (
        t.astype(BF16) for t in (p_ret, p_fox, p_rwkv, w_out, w_gate_up, w_down))
    for l in range(depth):
        w_cat = _w_in_relayout(w_in[l], f_lo, N_HEADS)
        ret_c, fox_c, rwkv_c, gate_c, f_c = _in_proj(x, row(mix_norm[l]), w_cat)

        y_ret = _retention(ret_c, cos, sin, row(ret_gn[l]))

        f_bias = jnp.pad(jnp.tile(fox_f_bias[l], FOX_SPLIT),
                         (0, LANES - FOX_SPLIT * N_HEADS)).reshape(1, LANES)
        fq, fk, fv = _fox_prep(fox_c, f_c, f_bias, jnp.tile(row(fox_q_norm[l]), (1, N_HEADS)),
                               jnp.tile(row(fox_k_norm[l]), (1, N_HEADS)))
        y_fox = _fox_attn(fq, fk, fv).reshape(B, MIX_W, S)

        y_rwkv = _rwkv(rwkv_c, row(rwkv_mu[l]), row(rwkv_w0[l]), rwkv_w2[l].astype(BF16),
                       row(rwkv_a0[l]), rwkv_a2[l].astype(BF16), rwkv_g2[l].astype(BF16),
                       row(rwkv_k_k[l]), row(rwkv_k_a[l]), row(rwkv_r_k[l]), row(rwkv_gn[l]))

        x = _merge_ffn(x, gate_c, y_ret, y_fox, y_rwkv, p_ret, p_fox, p_rwkv, w_out,
                       row(ffn_norm[l]), w_gate_up, w_down, l)
    return x
```

```python
import functools
import math

import jax
import jax.numpy as jnp
import numpy as np
from jax import lax
from jax.experimental import pallas as pl
from jax.experimental.pallas import tpu as pltpu

F32 = jnp.float32
BF16 = jnp.bfloat16

HEAD_DIM = 64
N_HEADS = 4
MIX_W = N_HEADS * HEAD_DIM
RET_CHUNK = 128
ROPE_BASE = 10000.0
DECAY_LORA = 64
AAA_LORA = 64
GATE_LORA = 128
NORM_EPS = 1e-6
RET_GN_EPS = 1e-5
RWKV_GN_EPS = 64e-5
RWKV_CHUNK = 64
RWKV_SUB = 16
LANES = 128
FOX_SPLIT = 3
VMEM_LIMIT = 52 * 1024 * 1024


def _cparams(sem):
    return pltpu.CompilerParams(dimension_semantics=sem, vmem_limit_bytes=VMEM_LIMIT)


def _resident(shape, layer=None):
    nd = len(shape)
    if layer is None:
        return pl.BlockSpec(shape, lambda *_: (0,) * nd, pipeline_mode=pl.Buffered(1))
    return pl.BlockSpec((None,) + tuple(shape), lambda *_: (layer,) + (0,) * nd,
                        pipeline_mode=pl.Buffered(1))


def _mm(a, b):
    return jnp.dot(a.astype(BF16), b.astype(BF16), preferred_element_type=F32)


def _mm_nt(a, b):
    return lax.dot_general(a.astype(BF16), b.astype(BF16), (((1,), (1,)), ((), ())),
                           preferred_element_type=F32)


def _mm_tn(a, b):
    return lax.dot_general(a.astype(BF16), b.astype(BF16), (((0,), (0,)), ((), ())),
                           preferred_element_type=F32)


def _mask_mm_f32(mask_bf16, x):
    h1 = x.astype(BF16)
    r1 = x - h1.astype(F32)
    h2 = r1.astype(BF16)
    h3 = (r1 - h2.astype(F32)).astype(BF16)
    dot = lambda h: jnp.dot(mask_bf16, h, preferred_element_type=F32)
    return dot(h1) + dot(h2) + dot(h3)


def _sigmoid(x):
    return 0.5 * jnp.tanh(0.5 * x) + 0.5


def _log_sigmoid(x):
    return jnp.minimum(x, 0.0) - jnp.log1p(jnp.exp(-jnp.abs(x)))


def _iota2(shape, dim):
    return lax.broadcasted_iota(jnp.int32, shape, dim)


def _head_mask():
    shift = int(math.log2(HEAD_DIM))
    same = (_iota2((MIX_W, MIX_W), 0) >> shift) == (_iota2((MIX_W, MIX_W), 1) >> shift)
    return jnp.where(same, 1.0, 0.0).astype(BF16)


def _head_blockdiag(x, head_mask):
    xb = x.astype(BF16)
    return jnp.concatenate([xb] * N_HEADS, axis=0) * head_mask


def _head_sum(x, head_mask):
    hi = x.astype(BF16)
    lo = (x - hi.astype(F32)).astype(BF16)
    dot = lambda t: jnp.dot(t, head_mask, preferred_element_type=F32)
    return dot(hi) + dot(lo)


def _bmm(a, bd):
    return jnp.dot(a.astype(BF16), bd, preferred_element_type=F32)


def _bmm_nt(a, bd):
    return lax.dot_general(a.astype(BF16), bd, (((1,), (1,)), ((), ())), preferred_element_type=F32)


RET_COLS = 4 * MIX_W
FOXQKV_COLS = 3 * MIX_W
RWKV_COLS = 3 * MIX_W + DECAY_LORA + AAA_LORA + GATE_LORA
IN_OFF = np.cumsum([0, RET_COLS, FOXQKV_COLS, RWKV_COLS])


def _w_in_relayout_kernel(w_ref, o_ref, *, f_lo, n_f):
    ncol = w_ref.shape[1]
    rest = ncol - (f_lo + n_f)
    o_ref[:, 0:f_lo] = w_ref[:, 0:f_lo].astype(BF16)
    o_ref[:, f_lo:f_lo + rest] = w_ref[:, f_lo + n_f:ncol].astype(BF16)
    win = w_ref[:, f_lo:f_lo + LANES]
    lane = _iota2(win.shape, 1)
    reps = jnp.where(lane < n_f, win, 0.0)
    for t in range(1, FOX_SPLIT):
        reps = jnp.where((lane >= t * n_f) & (lane < (t + 1) * n_f), pltpu.roll(win, t * n_f, 1), reps)
    o_ref[:, f_lo + rest:] = reps.astype(BF16)


def _w_in_relayout(w, f_lo, n_f, tr=128):
    D, ncol = w.shape
    out_cols = ncol - n_f + LANES
    return pl.pallas_call(
        functools.partial(_w_in_relayout_kernel, f_lo=f_lo, n_f=n_f),
        out_shape=jax.ShapeDtypeStruct((D, out_cols), BF16),
        grid=(D // tr,),
        in_specs=[pl.BlockSpec((tr, ncol), lambda i: (i, 0))],
        out_specs=pl.BlockSpec((tr, out_cols), lambda i: (i, 0)),
        compiler_params=_cparams(("parallel",)),
        name="w_in_relayout",
    )(w)


def _in_proj_kernel(x_ref, g_ref, w_ref, ret_ref, fox_ref, rwkv_ref, gate_ref, f_ref, *, d_model):
    x = x_ref[...]
    h = x * lax.rsqrt(jnp.mean(x * x, axis=-1, keepdims=True) + NORM_EPS) * g_ref[...]
    hb = h.astype(BF16)
    o0, o1, o2, o3 = (int(v) for v in IN_OFF)
    o4 = o3 + 3 * d_model
    dot = lambda lo, hi: jnp.dot(hb, w_ref[:, lo:hi], preferred_element_type=F32)
    ret_ref[...] = dot(o0, o1)
    fox_ref[...] = dot(o1, o2)
    rwkv_ref[...] = dot(o2, o3)
    gate_ref[...] = dot(o3, o4)
    f_ref[...] = dot(o4, o4 + LANES)


def _in_proj(x, gain, w_cat, tm=512):
    B, S, D = x.shape
    ncol = w_cat.shape[1]
    row = lambda w: pl.BlockSpec((None, tm, w), lambda b, i: (b, i, 0))
    widths = (RET_COLS, FOXQKV_COLS, RWKV_COLS, 3 * D, LANES)
    return pl.pallas_call(
        functools.partial(_in_proj_kernel, d_model=D),
        out_shape=tuple(jax.ShapeDtypeStruct((B, S, w), F32) for w in widths),
        grid=(B, S // tm),
        in_specs=[row(D), _resident((1, D)), _resident((D, ncol))],
        out_specs=tuple(row(w) for w in widths),
        compiler_params=_cparams(("parallel", "parallel")),
        name="in_proj",
    )(x, gain, w_cat)


def _retention_kernel(c_ref, cos_ref, sin_ref, gn_ref, dmat_ref, qdec_ref, kdec_ref, cdec_ref, o_ref,
                      state_ref, *, tb):
    C = RET_CHUNK
    n_batch = c_ref.shape[0]

    @pl.when(pl.program_id(0) == 0)
    def _():
        state_ref[...] = jnp.zeros_like(state_ref)

    head_mask = _head_mask()
    first_half = (_iota2((tb, MIX_W), 1) & (HEAD_DIM - 1)) < HEAD_DIM // 2
    kv_mask = jnp.where((_iota2((N_HEADS * C, MIX_W), 0) >> int(math.log2(C)))
                        == (_iota2((N_HEADS * C, MIX_W), 1) >> int(math.log2(HEAD_DIM))),
                        1.0, 0.0).astype(BF16)
    lane_head = _iota2((HEAD_DIM, MIX_W), 1) >> int(math.log2(HEAD_DIM))
    cos = jnp.concatenate([cos_ref[...]] * (MIX_W // LANES), axis=1)
    sin = jnp.concatenate([sin_ref[...]] * (MIX_W // LANES), axis=1)

    def rope(t):
        partner = jnp.where(first_half, pltpu.roll(t, MIX_W - HEAD_DIM // 2, 1),
                            pltpu.roll(t, HEAD_DIM // 2, 1))
        return t * cos + partner * sin

    def stack_heads(t):
        return jnp.concatenate([t.astype(BF16)] * N_HEADS, axis=0) * kv_mask

    for b in range(n_batch):
        blk = c_ref[b]
        q = rope(blk[:, 0:MIX_W])
        k = rope(blk[:, MIX_W:2 * MIX_W]) * (HEAD_DIM ** -0.5)
        v = blk[:, 2 * MIX_W:3 * MIX_W]
        g = blk[:, 3 * MIX_W:4 * MIX_W]
        state = state_ref[b]
        ys = []
        for c in range(tb // C):
            rows = slice(c * C, (c + 1) * C)
            qc, kc, vc = q[rows], k[rows], v[rows]
            scores = _bmm_nt(qc, stack_heads(kc)) * dmat_ref[...]
            inner = _bmm(scores, stack_heads(vc))
            cross = _bmm(qc * qdec_ref[...], _head_blockdiag(state, head_mask))
            ys.append(inner + cross)
            full = _mm_tn(kc * kdec_ref[...], vc)
            upd = full[0:HEAD_DIM]
            for h in range(1, N_HEADS):
                upd = jnp.where(lane_head == h, full[h * HEAD_DIM:(h + 1) * HEAD_DIM], upd)
            state = state * cdec_ref[...] + upd
        state_ref[b] = state
        y = jnp.concatenate(ys, axis=0)
        yc = y - _head_sum(y, head_mask) * (1.0 / HEAD_DIM)
        var = _head_sum(yc * yc, head_mask) * (1.0 / HEAD_DIM)
        yn = yc * lax.rsqrt(var + RET_GN_EPS)
        o_ref[b] = (g * _sigmoid(g) * (yn * gn_ref[...])).astype(o_ref.dtype)


def _retention_tables():
    C = RET_CHUNK
    log_gamma = jnp.log1p(-jnp.exp2(-5.0 - jnp.arange(N_HEADS, dtype=F32)))
    idx = jnp.arange(C, dtype=F32)
    dist = idx[:, None] - idx[None, :]
    dmat = jnp.where(dist >= 0, jnp.exp(log_gamma[:, None, None] * jnp.maximum(dist, 0.0)), 0.0)
    dmat = dmat.transpose(1, 0, 2).reshape(C, N_HEADS * C)
    lanes = lambda t: jnp.repeat(t, HEAD_DIM, axis=-1)
    q_dec = lanes(jnp.exp(log_gamma[None, :] * (idx[:, None] + 1.0)))
    k_dec = lanes(jnp.exp(log_gamma[None, :] * (C - 1.0 - idx[:, None])))
    c_dec = lanes(jnp.exp(log_gamma * C)[None, :])
    return dmat, q_dec, k_dec, c_dec


def _retention(ret_c, cos, sin, gn, tb=512):
    B, S, W = ret_c.shape
    tables = _retention_tables()
    return pl.pallas_call(
        functools.partial(_retention_kernel, tb=tb),
        out_shape=jax.ShapeDtypeStruct((B, S, MIX_W), BF16),
        grid=(S // tb,),
        in_specs=[pl.BlockSpec((B, tb, W), lambda i: (0, i, 0)),
                  pl.BlockSpec((tb, LANES), lambda i: (i, 0)),
                  pl.BlockSpec((tb, LANES), lambda i: (i, 0)),
                  _resident((1, MIX_W))] + [_resident(t.shape) for t in tables],
        out_specs=pl.BlockSpec((B, tb, MIX_W), lambda i: (0, i, 0)),
        scratch_shapes=[pltpu.VMEM((B, HEAD_DIM, MIX_W), F32)],
        compiler_params=_cparams(("arbitrary",)),
        name="retention",
    )(ret_c, cos, sin, gn, *tables)


def _rope_tables(S):
    half = HEAD_DIM // 2
    inv_freq = ROPE_BASE ** (-jnp.arange(half, dtype=F32) / half)
    ang = jnp.arange(S, dtype=F32)[:, None] * inv_freq[None, :]
    cos, sin = jnp.cos(ang), jnp.sin(ang)
    cos_h = jnp.concatenate([cos, cos], axis=-1)
    sin_h = jnp.concatenate([-sin, sin], axis=-1)
    reps = LANES // HEAD_DIM
    return jnp.tile(cos_h, (1, reps)), jnp.tile(sin_h, (1, reps))


LOG2E = math.log2(math.e)
FOX_AUG = HEAD_DIM
FOX_TQ = 512
FOX_TK = 256
FOX_VROWS = 80
FOX_UNROLL = 8


def _fox_place_tables():
    H, Dh, L = N_HEADS, HEAD_DIM, LANES
    place_q = np.zeros((H // 2, MIX_W + L, 2 * L), np.float32)
    place_k = np.zeros((H // 2, MIX_W + L, 2 * L), np.float32)
    place_vt = np.zeros((H, FOX_VROWS, MIX_W), np.float32)
    for h in range(H):
        out = (h % 2) * L
        for d in range(Dh):
            place_q[h // 2, h * Dh + d, out + d] = 1.0
            place_k[h // 2, h * Dh + d, out + d] = 1.0
            place_vt[h, d, h * Dh + d] = 1.0
        for t in range(FOX_SPLIT):
            place_q[h // 2, MIX_W + t * H + h, out + FOX_AUG + t] = 1.0
            place_k[h // 2, MIX_W + t * H + h, out + FOX_AUG + FOX_SPLIT + t] = -1.0
    return tuple(jnp.asarray(t, BF16) for t in (place_q, place_k, place_vt))


def _fox_prep_kernel(c_ref, f_ref, bias_ref, qg_ref, kg_ref, pq_ref, pk_ref, pvt_ref, q_ref, k_ref,
                     vt_ref, carry_ref, *, ts, tk):
    @pl.when(pl.program_id(1) == 0)
    def _():
        carry_ref[...] = jnp.zeros_like(carry_ref)

    log_f = _log_sigmoid(f_ref[...] + bias_ref[...])
    tril = (_iota2((LANES, LANES), 0) >= _iota2((LANES, LANES), 1)).astype(BF16)
    carry = carry_ref[0:1, :]
    parts = []
    for r in range(0, ts, LANES):
        part = _mask_mm_f32(tril, log_f[r:r + LANES]) + carry
        carry = part[LANES - 1:LANES, :]
        parts.append(part)
    carry_ref[0:1, :] = carry
    c = jnp.concatenate(parts, axis=0) * LOG2E
    lane_c = _iota2(c.shape, 1)
    csel = jnp.zeros(c.shape, BF16)
    rest = c
    for t in range(FOX_SPLIT):
        term = rest.astype(BF16)
        csel = jnp.where((lane_c >= t * N_HEADS) & (lane_c < (t + 1) * N_HEADS), term, csel)
        rest = rest - term.astype(F32)

    blk = c_ref[...]
    head_mask = _head_mask()

    def rms(t, gain):
        ms = _head_sum(t * t, head_mask) * (1.0 / HEAD_DIM)
        return t * lax.rsqrt(ms + NORM_EPS) * gain

    qn = rms(blk[:, 0:MIX_W], qg_ref[...]) * (LOG2E * HEAD_DIM ** -0.5)
    kn = rms(blk[:, MIX_W:2 * MIX_W], kg_ref[...])
    q_stack = jnp.concatenate([qn.astype(BF16), csel], axis=1)
    k_stack = jnp.concatenate([kn.astype(BF16), csel], axis=1)
    vb = blk[:, 2 * MIX_W:3 * MIX_W].astype(BF16)

    lane = _iota2((1, 2 * LANES), 1) & (LANES - 1)
    ones_q = jnp.where((lane >= FOX_AUG + FOX_SPLIT) & (lane < FOX_AUG + 2 * FOX_SPLIT), 1.0, 0.0)
    ones_k = jnp.where((lane >= FOX_AUG) & (lane < FOX_AUG + FOX_SPLIT), 1.0, 0.0)
    ones_row = _iota2((FOX_VROWS, ts), 0) == FOX_AUG

    for hp in range(N_HEADS // 2):
        q2 = (jnp.dot(q_stack, pq_ref[hp], preferred_element_type=F32) + ones_q).astype(BF16)
        k2 = (jnp.dot(k_stack, pk_ref[hp], preferred_element_type=F32) + ones_k).astype(BF16)
        for sub in range(2):
            q_ref[2 * hp + sub] = q2[:, sub * LANES:(sub + 1) * LANES]
            k_ref[2 * hp + sub] = k2[:, sub * LANES:(sub + 1) * LANES]
    for h in range(N_HEADS):
        v_t = jnp.where(ones_row, 1.0, _mm_nt(pvt_ref[h], vb))
        for kb in range(ts // tk):
            vt_ref[h, kb] = v_t[:, kb * tk:(kb + 1) * tk].astype(BF16)


def _fox_prep(fox_c, f_c, f_bias, q_gain, k_gain, ts=1024):
    B, S, W = fox_c.shape
    tk = FOX_TK
    tables = _fox_place_tables()
    head_spec = pl.BlockSpec((None, N_HEADS, ts, LANES), lambda b, i: (b, 0, i, 0))
    shp = jax.ShapeDtypeStruct((B, N_HEADS, S, LANES), BF16)
    return pl.pallas_call(
        functools.partial(_fox_prep_kernel, ts=ts, tk=tk),
        out_shape=(shp, shp, jax.ShapeDtypeStruct((B, N_HEADS, S // tk, FOX_VROWS, tk), BF16)),
        grid=(B, S // ts),
        in_specs=[pl.BlockSpec((None, ts, W), lambda b, i: (b, i, 0)),
                  pl.BlockSpec((None, ts, LANES), lambda b, i: (b, i, 0)),
                  _resident((1, LANES)), _resident((1, MIX_W)), _resident((1, MIX_W))]
                 + [_resident(t.shape) for t in tables],
        out_specs=(head_spec, head_spec,
                   pl.BlockSpec((None, N_HEADS, ts // tk, FOX_VROWS, tk),
                                lambda b, i: (b, 0, i, 0, 0))),
        scratch_shapes=[pltpu.VMEM((8, LANES), F32)],
        compiler_params=_cparams(("parallel", "arbitrary")),
        name="fox_prep",
    )(fox_c, f_c, f_bias, q_gain, k_gain, *tables)


def _fox_attn_kernel(q_ref, k_ref, vt_ref, o_ref, s0_ref, s1_ref, p0_ref, p1_ref, m_ref, alpha_ref,
                     acc_ref, *, tq, tk):
    qi = pl.program_id(2)
    m_ref[...] = jnp.full(m_ref.shape, -jnp.inf, F32)
    acc_ref[...] = jnp.zeros_like(acc_ref)
    p1_ref[...] = jnp.zeros_like(p1_ref)

    def qk(j, s_ref):
        start = pl.multiple_of(j * tk, tk)
        s_ref[...] = _mm_nt(k_ref[pl.ds(start, tk), :], q_ref[...])

    def pv(j, p_ref):
        return jnp.dot(vt_ref[j], p_ref[...], preferred_element_type=F32)

    def softmax(s_ref, p_ref, key_offset):
        for c in range(0, tq, LANES):
            cols = slice(c, c + LANES)
            s = s_ref[:, cols]
            if key_offset is not None and key_offset + tk - 1 > c:
                keep = _iota2((tk, LANES), 0) + key_offset <= _iota2((tk, LANES), 1) + c
                s = jnp.where(keep, s, -jnp.inf)
            m_old = m_ref[:, cols]
            m_new = jnp.maximum(m_old, jnp.max(s, axis=0, keepdims=True))
            p_ref[:, cols] = jnp.exp2(s - m_new).astype(BF16)
            m_ref[:, cols] = m_new
            alpha_ref[:, cols] = jnp.exp2(m_old - m_new)

    def advance(pv_prev):
        acc_ref[...] = (acc_ref[...] + pv_prev) * alpha_ref[...]

    qk(0, s0_ref)

    def pair(i):
        j = 2 * i
        qk(j + 1, s1_ref)
        pv_prev = pv(jnp.maximum(j - 1, 0), p1_ref)
        softmax(s0_ref, p0_ref, None)
        advance(pv_prev)
        qk(j + 2, s0_ref)
        pv_prev = pv(j, p0_ref)
        softmax(s1_ref, p1_ref, None)
        advance(pv_prev)

    done = 0
    unroll = FOX_UNROLL
    while unroll >= 1:
        def group(i, carry, unroll=unroll):
            for u in range(unroll):
                pair(i + u)
            return carry

        n_groups = (qi - done) // unroll
        lax.fori_loop(0, n_groups, lambda g, c, done=done, unroll=unroll: group(done + g * unroll, c),
                      0)
        done = done + n_groups * unroll
        unroll //= 2

    j = 2 * qi
    qk(j + 1, s1_ref)
    pv_prev = pv(jnp.maximum(j - 1, 0), p1_ref)
    softmax(s0_ref, p0_ref, 0)
    advance(pv_prev)
    pv_prev = pv(j, p0_ref)
    softmax(s1_ref, p1_ref, tk)
    advance(pv_prev)
    acc = acc_ref[...] + pv(j + 1, p1_ref)
    o_ref[...] = (acc[0:HEAD_DIM] / acc[FOX_AUG:FOX_AUG + 1]).astype(o_ref.dtype)


def _fox_attn(q, k, vt):
    B, H, S, W = q.shape
    tq, tk = FOX_TQ, FOX_TK
    assert tq == 2 * tk and S % tq == 0
    return pl.pallas_call(
        functools.partial(_fox_attn_kernel, tq=tq, tk=tk),
        out_shape=jax.ShapeDtypeStruct((B, H, HEAD_DIM, S), BF16),
        grid=(B, H, S // tq),
        in_specs=[pl.BlockSpec((None, None, tq, W), lambda b, h, i: (b, h, i, 0)),
                  pl.BlockSpec((None, None, S, W), lambda b, h, i: (b, h, 0, 0)),
                  pl.BlockSpec((None, None, S // tk, FOX_VROWS, tk),
                               lambda b, h, i: (b, h, 0, 0, 0))],
        out_specs=pl.BlockSpec((None, None, HEAD_DIM, tq), lambda b, h, i: (b, h, 0, i)),
        scratch_shapes=[pltpu.VMEM((tk, tq), F32), pltpu.VMEM((tk, tq), F32),
                        pltpu.VMEM((tk, tq), BF16), pltpu.VMEM((tk, tq), BF16),
                        pltpu.VMEM((1, tq), F32), pltpu.VMEM((1, tq), F32),
                        pltpu.VMEM((FOX_VROWS, tq), F32)],
        compiler_params=_cparams(("parallel", "parallel", "arbitrary")),
        name="fox_attn",
    )(q, k, vt)


def _unit_lower_inverse(a_list, head_mask):
    C, W = a_list[0].shape
    bdiag = lambda t: _head_blockdiag(t, head_mask)
    ii = _iota2((C, W), 0)
    jj = _iota2((C, W), 1) & (C - 1)
    shift = int(math.log2(RWKV_SUB))
    same_blk = (ii >> shift) == (jj >> shift)
    eye = (ii == jj).astype(F32)
    d = [jnp.where(same_blk, a, 0.0) for a in a_list]
    low = [a - di for a, di in zip(a_list, d)]
    x = [eye + di for di in d]
    dp = [_bmm(di, bdiag(di)) for di in d]
    yield
    for _ in range(shift - 2):
        bd = [bdiag(t) for t in dp]
        both = [_bmm(jnp.concatenate([xi, t], axis=0), b) for xi, t, b in zip(x, dp, bd)]
        x = [xi + r[0:C] for xi, r in zip(x, both)]
        dp = [r[C:] for r in both]
        yield
    x = [xi + _bmm(xi, bdiag(t)) for xi, t in zip(x, dp)]
    yield
    n = [_bmm(xi, bdiag(lo)) for xi, lo in zip(x, low)]
    yield
    y = [eye + ni for ni in n]
    npow = n
    for _ in range(int(math.log2(C // RWKV_SUB)) - 1):
        bd = [bdiag(t) for t in npow]
        npow = [_bmm(t, b) for t, b in zip(npow, bd)]
        yield
        y = [yi + _bmm(yi, bdiag(t)) for yi, t in zip(y, npow)]
        yield
    return [_bmm(yi, bdiag(xi)) for yi, xi in zip(y, x)]


RWKV_STASH = (
    ("ar", lambda C, tb: (2 * C, MIX_W), BF16, True),
    ("vc", lambda C, tb: (C, MIX_W), BF16, True),
    ("bd_v", lambda C, tb: (MIX_W, MIX_W), BF16, True),
    ("a_r", lambda C, tb: (C, 2 * MIX_W), BF16, True),
    ("bk_end", lambda C, tb: (2 * C, MIX_W), BF16, True),
    ("t_inv", lambda C, tb: (C, MIX_W), BF16, True),
    ("akv", lambda C, tb: (C, MIX_W), F32, True),
    ("p_end", lambda C, tb: (1, MIX_W), F32, True),
    ("g", lambda C, tb: (tb, MIX_W), F32, False),
    ("bonus", lambda C, tb: (tb, MIX_W), F32, False),
)


def _interleave(*generators):
    alive = list(generators)
    while alive:
        for gen in list(alive):
            try:
                next(gen)
            except StopIteration:
                alive.remove(gen)


def _rwkv_kernel(c_ref, mu_ref, w0_ref, w2_ref, a0_ref, a2_ref, g2_ref, kk_ref, ka_ref, rk_ref,
                 gn_ref, o_ref, state_ref, carry_ref, y_ref, *stash_refs, tb):
    C = RWKV_CHUNK
    W = MIX_W
    n_batch = c_ref.shape[0]
    n_chunks = tb // C
    stash = {name: ref for (name, _, _, _), ref in zip(RWKV_STASH, stash_refs)}
    step = pl.program_id(0)
    n_entries = n_batch * n_chunks
    wr_chunk = (step % 2) * n_entries
    rd_chunk = ((step + 1) % 2) * n_entries
    wr_batch = (step % 2) * n_batch
    rd_batch = ((step + 1) % 2) * n_batch

    @pl.when(pl.program_id(0) == 0)
    def _():
        state_ref[...] = jnp.zeros_like(state_ref)
        carry_ref[...] = jnp.zeros_like(carry_ref)
        for ref in stash_refs:
            ref[...] = jnp.zeros_like(ref)

    head_mask = _head_mask()
    bdiag = lambda t: _head_blockdiag(t, head_mask)
    ii = _iota2((C, W), 0)
    jj = _iota2((C, W), 1) & (C - 1)
    strict = ii > jj
    incl = ii >= jj
    tril = (_iota2((C, C), 0) >= _iota2((C, C), 1)).astype(BF16)
    lane_head = _iota2((HEAD_DIM, W), 1) >> int(math.log2(HEAD_DIM))
    prepared = {}

    def prepare():
        post = []
        pre = []
        for b in range(n_batch):
            cur = c_ref[b]
            prev = jnp.where(_iota2(cur.shape, 0) == 0, carry_ref[b, 0:1, :],
                             pltpu.roll(cur, 1, 0))
            carry_ref[b, 0:1, :] = cur[tb - 1:tb, :]
            mixed = cur + (prev - cur) * mu_ref[...]
            r = mixed[:, 0:W]
            k = mixed[:, W:2 * W]
            v = mixed[:, 2 * W:3 * W]
            xw = mixed[:, 3 * W:3 * W + DECAY_LORA]
            xa = mixed[:, 3 * W + DECAY_LORA:3 * W + DECAY_LORA + AAA_LORA]
            xg = mixed[:, 3 * W + DECAY_LORA + AAA_LORA:]

            log_w = -math.exp(-0.5) * _sigmoid(w0_ref[...] + _mm(jnp.tanh(xw), w2_ref[...]))
            a = _sigmoid(a0_ref[...] + _mm(xa, a2_ref[...]))
            g = _mm(_sigmoid(xg), g2_ref[...])
            kk = k * kk_ref[...]
            k2 = k * (1.0 + (a - 1.0) * ka_ref[...])
            kk = kk / jnp.maximum(jnp.sqrt(_head_sum(kk * kk, head_mask)), 1e-12)
            a_vec = -kk
            b_vec = kk * a
            post.append((g, _head_sum(r * k2 * rk_ref[...], head_mask) * v))
            yield

            for c in range(n_chunks):
                rows = slice(c * C, (c + 1) * C)
                lw = log_w[rows]
                cum = _mask_mm_f32(tril, lw)
                e_pos = jnp.exp(cum)
                e_neg = jnp.exp(-cum)
                p_end = e_pos[C - 1:C, :]
                at = a_vec[rows] * jnp.exp(cum - lw)
                rt = r[rows] * e_pos
                bt = b_vec[rows] * e_neg
                kt = k2[rows] * e_neg
                ar = jnp.concatenate([at, rt], axis=0)
                gb = _bmm_nt(ar, bdiag(bt))
                gk = _bmm_nt(ar, bdiag(kt))
                vc = v[rows]
                pre.append(dict(
                    ar=ar, p_end=p_end, vc=vc, bd_v=bdiag(vc),
                    a_ab=jnp.where(strict, gb[0:C], 0.0), a_ak=jnp.where(strict, gk[0:C], 0.0),
                    a_r=jnp.concatenate([jnp.where(incl, gb[C:], 0.0),
                                         jnp.where(incl, gk[C:], 0.0)], axis=1),
                    bk_end=jnp.concatenate([bt * p_end, kt * p_end], axis=0)))
                yield
        t_inv = yield from _unit_lower_inverse([p["a_ab"] for p in pre], head_mask)
        for p, t in zip(pre, t_inv):
            p["t_inv"] = t
            p["akv"] = _bmm(p["a_ak"], p["bd_v"])
        prepared["pre"] = pre
        prepared["post"] = post

    def recur():
        s_all = [state_ref[b] for b in range(n_batch)]
        batches = range(n_batch)
        for c in range(n_chunks):
            ent = [rd_chunk + b * n_chunks + c for b in batches]
            ars = [_bmm_nt(stash["ar"][e], bdiag(s_all[b])) for b, e in zip(batches, ent)]
            yield
            u = [_bmm(stash["t_inv"][e], bdiag(ars[b][0:C] + stash["akv"][e]))
                 for b, e in zip(batches, ent)]
            yield
            for b, e in zip(batches, ent):
                uv = jnp.concatenate([u[b].astype(BF16), stash["vc"][e]], axis=0)
                y_ref[b, c * C:(c + 1) * C, :] = ars[b][C:] + jnp.dot(
                    stash["a_r"][e], jnp.concatenate([bdiag(u[b]), stash["bd_v"][e]], axis=0),
                    preferred_element_type=F32)
                full = _mm_tn(uv, stash["bk_end"][e])
                upd = full[0:HEAD_DIM]
                for h in range(1, N_HEADS):
                    upd = jnp.where(lane_head == h, full[h * HEAD_DIM:(h + 1) * HEAD_DIM], upd)
                s_all[b] = s_all[b] * stash["p_end"][e] + upd
            yield
        for b in range(n_batch):
            state_ref[b] = s_all[b]
            y = y_ref[b]
            yc = y - _head_sum(y, head_mask) * (1.0 / HEAD_DIM)
            var = _head_sum(yc * yc, head_mask) * (1.0 / HEAD_DIM)
            yn = yc * lax.rsqrt(var + RWKV_GN_EPS) * gn_ref[...]
            o_ref[b] = ((yn + stash["bonus"][rd_batch + b])
                        * stash["g"][rd_batch + b]).astype(o_ref.dtype)
            yield

    _interleave(prepare(), recur())

    for e, p in enumerate(prepared["pre"]):
        for name, _, dtype, per_chunk in RWKV_STASH:
            if per_chunk:
                stash[name][wr_chunk + e] = p[name].astype(dtype)
    for b, (g, bonus) in enumerate(prepared["post"]):
        stash["g"][wr_batch + b] = g
        stash["bonus"][wr_batch + b] = bonus


def _rwkv(rwkv_c, mu, w0, w2, a0, a2, g2, k_k, k_a, r_k, gn, tb=256):
    B, S, W = rwkv_c.shape
    n_blocks = S // tb
    n_entries = B * (tb // RWKV_CHUNK)
    vec = lambda t: _resident((1, t.shape[-1]))
    mat = lambda t: _resident(t.shape)
    stash_shapes = [pltpu.VMEM((2 * (n_entries if per_chunk else B),) + shape(RWKV_CHUNK, tb), dtype)
                    for _, shape, dtype, per_chunk in RWKV_STASH]
    return pl.pallas_call(
        functools.partial(_rwkv_kernel, tb=tb),
        out_shape=jax.ShapeDtypeStruct((B, S, MIX_W), BF16),
        grid=(n_blocks + 1,),
        in_specs=[pl.BlockSpec((B, tb, W), lambda i: (0, jnp.minimum(i, n_blocks - 1), 0)),
                  vec(mu), vec(w0), mat(w2), vec(a0), mat(a2), mat(g2), vec(k_k), vec(k_a),
                  vec(r_k), vec(gn)],
        out_specs=pl.BlockSpec((B, tb, MIX_W), lambda i: (0, jnp.maximum(i - 1, 0), 0)),
        scratch_shapes=[pltpu.VMEM((B, HEAD_DIM, MIX_W), F32),
                        pltpu.VMEM((B, 8, W), F32),
                        pltpu.VMEM((B, tb, MIX_W), F32)] + stash_shapes,
        compiler_params=_cparams(("arbitrary",)),
        name="rwkv7",
    )(rwkv_c, mu, w0, w2, a0, a2, g2, k_k, k_a, r_k, gn)


def _merge_ffn_kernel(x_ref, gate_ref, yret_ref, yfox_ref, yrwkv_ref, pret_ref, pfox_ref, prwkv_ref,
                      wout_ref, g_ref, wgu_ref, wd_ref, o_ref, *, d_model, d_ff, ck):
    D = d_model
    dot = lambda a, b: jnp.dot(a, b, preferred_element_type=F32)
    gates = gate_ref[...]
    fox = lax.dot_general(yfox_ref[...], pfox_ref[...], (((0,), (0,)), ((), ())),
                          preferred_element_type=F32)
    merged = (_sigmoid(gates[:, 0:D]) * dot(yret_ref[...], pret_ref[...])
              + _sigmoid(gates[:, D:2 * D]) * fox
              + _sigmoid(gates[:, 2 * D:3 * D]) * dot(yrwkv_ref[...], prwkv_ref[...]))
    x = x_ref[...] + dot(merged.astype(BF16), wout_ref[...])

    h = (x * lax.rsqrt(jnp.mean(x * x, axis=-1, keepdims=True) + NORM_EPS) * g_ref[...]).astype(BF16)
    acc = x
    for c in range(d_ff // ck):
        lo = c * ck
        gate = dot(h, wgu_ref[:, lo:lo + ck])
        up = dot(h, wgu_ref[:, d_ff + lo:d_ff + lo + ck])
        act = (gate * _sigmoid(gate) * up).astype(BF16)
        acc = acc + dot(act, wd_ref[lo:lo + ck, :])
    o_ref[...] = acc


def _merge_ffn(x, gates, y_ret, y_fox, y_rwkv, p_ret, p_fox, p_rwkv, w_out, gain, w_gate_up, w_down,
               layer, tm=512, ck=256):
    B, S, D = x.shape
    d_ff = w_down.shape[1]
    row = lambda w: pl.BlockSpec((None, tm, w), lambda b, i: (b, i, 0))
    stacked = lambda w: _resident(w.shape[1:], layer)
    weights = (p_ret, p_fox, p_rwkv, w_out, gain, w_gate_up, w_down)
    weight_specs = [stacked(p_ret), stacked(p_fox), stacked(p_rwkv), stacked(w_out),
                    _resident(gain.shape), stacked(w_gate_up), stacked(w_down)]
    return pl.pallas_call(
        functools.partial(_merge_ffn_kernel, d_model=D, d_ff=d_ff, ck=ck),
        out_shape=jax.ShapeDtypeStruct((B, S, D), F32),
        grid=(B, S // tm),
        in_specs=[row(D), row(3 * D), row(MIX_W),
                  pl.BlockSpec((None, MIX_W, tm), lambda b, i: (b, 0, i)),
                  row(MIX_W)] + weight_specs,
        out_specs=row(D),
        compiler_params=_cparams(("parallel", "parallel")),
        name="merge_ffn",
    )(x, gates, y_ret, y_fox, y_rwkv, *weights)


def kernel(x, mix_norm, w_in, ret_gn, fox_q_norm, fox_k_norm, fox_f_bias, rwkv_mu, rwkv_w0, rwkv_w2,
           rwkv_a0, rwkv_a2, rwkv_g2, rwkv_k_k, rwkv_k_a, rwkv_r_k, rwkv_gn, p_ret, p_fox, p_rwkv,
           w_out, ffn_norm, w_gate_up, w_down):
    B, S, D = x.shape
    depth = w_in.shape[0]
    cos, sin = _rope_tables(S)
    o0, o1, o2, o3 = (int(v) for v in IN_OFF)
    f_lo = o1 + FOXQKV_COLS
    row = lambda t: t.reshape(1, -1)
    p_ret, p_fox, p_rwkv, w_out, w_gate_up, w_down = (
        t.astype(BF16) for t in (p_ret, p_fox, p_rwkv, w_out, w_gate_up, w_down))
    for l in range(depth):
        w_cat = _w_in_relayout(w_in[l], f_lo, N_HEADS)
        ret_c, fox_c, rwkv_c, gate_c, f_c = _in_proj(x, row(mix_norm[l]), w_cat)

        y_ret = _retention(ret_c, cos, sin, row(ret_gn[l]))

        f_bias = jnp.pad(jnp.tile(fox_f_bias[l], FOX_SPLIT),
                         (0, LANES - FOX_SPLIT * N_HEADS)).reshape(1, LANES)
        fq, fk, fv = _fox_prep(fox_c, f_c, f_bias, jnp.tile(row(fox_q_norm[l]), (1, N_HEADS)),
                               jnp.tile(row(fox_k_norm[l]), (1, N_HEADS)))
        y_fox = _fox_attn(fq, fk, fv).reshape(B, MIX_W, S)

        y_rwkv = _rwkv(rwkv_c, row(rwkv_mu[l]), row(rwkv_w0[l]), rwkv_w2[l].astype(BF16),
                       row(rwkv_a0[l]), rwkv_a2[l].astype(BF16), rwkv_g2[l].astype(BF16),
                       row(rwkv_k_k[l]), row(rwkv_k_a[l]), row(rwkv_r_k[l]), row(rwkv_gn[l]))

        x = _merge_ffn(x, gate_c, y_ret, y_fox, y_rwkv, p_ret, p_fox, p_rwkv, w_out,
                       row(ffn_norm[l]), w_gate_up, w_down, l)
    return x
```

```python
import functools
import math

import jax
import jax.numpy as jnp
import numpy as np
from jax import lax
from jax.experimental import pallas as pl
from jax.experimental.pallas import tpu as pltpu

F32 = jnp.float32
BF16 = jnp.bfloat16

HEAD_DIM = 64
N_HEADS = 4
MIX_W = N_HEADS * HEAD_DIM
RET_CHUNK = 128
ROPE_BASE = 10000.0
DECAY_LORA = 64
AAA_LORA = 64
GATE_LORA = 128
NORM_EPS = 1e-6
RET_GN_EPS = 1e-5
RWKV_GN_EPS = 64e-5
RWKV_CHUNK = 64
RWKV_SUB = 16
LANES = 128
FOX_SPLIT = 3
VMEM_LIMIT = 52 * 1024 * 1024


def _cparams(sem):
    return pltpu.CompilerParams(dimension_semantics=sem, vmem_limit_bytes=VMEM_LIMIT)


def _resident(shape, layer=None):
    nd = len(shape)
    if layer is None:
        return pl.BlockSpec(shape, lambda *_: (0,) * nd, pipeline_mode=pl.Buffered(1))
    return pl.BlockSpec((None,) + tuple(shape), lambda *_: (layer,) + (0,) * nd,
                        pipeline_mode=pl.Buffered(1))


def _mm(a, b):
    return jnp.dot(a.astype(BF16), b.astype(BF16), preferred_element_type=F32)


def _mm_nt(a, b):
    return lax.dot_general(a.astype(BF16), b.astype(BF16), (((1,), (1,)), ((), ())),
                           preferred_element_type=F32)


def _mm_tn(a, b):
    return lax.dot_general(a.astype(BF16), b.astype(BF16), (((0,), (0,)), ((), ())),
                           preferred_element_type=F32)


def _mask_mm_f32(mask_bf16, x):
    h1 = x.astype(BF16)
    r1 = x - h1.astype(F32)
    h2 = r1.astype(BF16)
    h3 = (r1 - h2.astype(F32)).astype(BF16)
    dot = lambda h: jnp.dot(mask_bf16, h, preferred_element_type=F32)
    return dot(h1) + dot(h2) + dot(h3)


def _sigmoid(x):
    return 0.5 * jnp.tanh(0.5 * x) + 0.5


def _log_sigmoid(x):
    return jnp.minimum(x, 0.0) - jnp.log1p(jnp.exp(-jnp.abs(x)))


def _iota2(shape, dim):
    return lax.broadcasted_iota(jnp.int32, shape, dim)


def _head_mask():
    shift = int(math.log2(HEAD_DIM))
    same = (_iota2((MIX_W, MIX_W), 0) >> shift) == (_iota2((MIX_W, MIX_W), 1) >> shift)
    return jnp.where(same, 1.0, 0.0).astype(BF16)


def _head_blockdiag(x, head_mask):
    xb = x.astype(BF16)
    return jnp.concatenate([xb] * N_HEADS, axis=0) * head_mask


def _head_sum(x, head_mask):
    hi = x.astype(BF16)
    lo = (x - hi.astype(F32)).astype(BF16)
    dot = lambda t: jnp.dot(t, head_mask, preferred_element_type=F32)
    return dot(hi) + dot(lo)


def _bmm(a, bd):
    return jnp.dot(a.astype(BF16), bd, preferred_element_type=F32)


def _bmm_nt(a, bd):
    return lax.dot_general(a.astype(BF16), bd, (((1,), (1,)), ((), ())), preferred_element_type=F32)


RET_COLS = 4 * MIX_W
FOXQKV_COLS = 3 * MIX_W
RWKV_COLS = 3 * MIX_W + DECAY_LORA + AAA_LORA + GATE_LORA
IN_OFF = np.cumsum([0, RET_COLS, FOXQKV_COLS, RWKV_COLS])


def _w_in_relayout_kernel(w_ref, o_ref, *, f_lo, n_f):
    ncol = w_ref.shape[1]
    rest = ncol - (f_lo + n_f)
    o_ref[:, 0:f_lo] = w_ref[:, 0:f_lo].astype(BF16)
    o_ref[:, f_lo:f_lo + rest] = w_ref[:, f_lo + n_f:ncol].astype(F32).astype(BF16)
    win = w_ref[:, f_lo:f_lo + LANES].astype(F32)
    lane = _iota2(win.shape, 1)
    reps = jnp.where(lane < n_f, win, 0.0)
    for t in range(1, FOX_SPLIT):
        reps = jnp.where((lane >= t * n_f) & (lane < (t + 1) * n_f), pltpu.roll(win, t * n_f, 1), reps)
    o_ref[:, f_lo + rest:] = reps.astype(BF16)


def _w_in_relayout(w, f_lo, n_f, tr=128):
    D, ncol = w.shape
    out_cols = ncol - n_f + LANES
    return pl.pallas_call(
        functools.partial(_w_in_relayout_kernel, f_lo=f_lo, n_f=n_f),
        out_shape=jax.ShapeDtypeStruct((D, out_cols), BF16),
        grid=(D // tr,),
        in_specs=[pl.BlockSpec((tr, ncol), lambda i: (i, 0))],
        out_specs=pl.BlockSpec((tr, out_cols), lambda i: (i, 0)),
        compiler_params=_cparams(("parallel",)),
        name="w_in_relayout",
    )(w)


def _in_proj_kernel(x_ref, g_ref, w_ref, ret_ref, fox_ref, rwkv_ref, gate_ref, f_ref, *, d_model):
    x = x_ref[...]
    h = x * lax.rsqrt(jnp.mean(x * x, axis=-1, keepdims=True) + NORM_EPS) * g_ref[...]
    hb = h.astype(BF16)
    o0, o1, o2, o3 = (int(v) for v in IN_OFF)
    o4 = o3 + 3 * d_model
    dot = lambda lo, hi: jnp.dot(hb, w_ref[:, lo:hi], preferred_element_type=F32)
    ret_ref[...] = dot(o0, o1)
    fox_ref[...] = dot(o1, o2)
    rwkv_ref[...] = dot(o2, o3)
    gate_ref[...] = dot(o3, o4)
    f_ref[...] = dot(o4, o4 + LANES)


def _in_proj(x, gain, w_cat, tm=512):
    B, S, D = x.shape
    ncol = w_cat.shape[1]
    row = lambda w: pl.BlockSpec((None, tm, w), lambda b, i: (b, i, 0))
    widths = (RET_COLS, FOXQKV_COLS, RWKV_COLS, 3 * D, LANES)
    return pl.pallas_call(
        functools.partial(_in_proj_kernel, d_model=D),
        out_shape=tuple(jax.ShapeDtypeStruct((B, S, w), F32) for w in widths),
        grid=(B, S // tm),
        in_specs=[row(D), _resident((1, D)), _resident((D, ncol))],
        out_specs=tuple(row(w) for w in widths),
        compiler_params=_cparams(("parallel", "parallel")),
        name="in_proj",
    )(x, gain, w_cat)


def _retention_kernel(c_ref, cos_ref, sin_ref, gn_ref, dmat_ref, qdec_ref, kdec_ref, cdec_ref, o_ref,
                      state_ref, *, tb):
    C = RET_CHUNK
    n_batch = c_ref.shape[0]

    @pl.when(pl.program_id(0) == 0)
    def _():
        state_ref[...] = jnp.zeros_like(state_ref)

    head_mask = _head_mask()
    first_half = (_iota2((tb, MIX_W), 1) & (HEAD_DIM - 1)) < HEAD_DIM // 2
    kv_mask = jnp.where((_iota2((N_HEADS * C, MIX_W), 0) >> int(math.log2(C)))
                        == (_iota2((N_HEADS * C, MIX_W), 1) >> int(math.log2(HEAD_DIM))),
                        1.0, 0.0).astype(BF16)
    lane_head = _iota2((HEAD_DIM, MIX_W), 1) >> int(math.log2(HEAD_DIM))
    cos = jnp.concatenate([cos_ref[...]] * (MIX_W // LANES), axis=1)
    sin = jnp.concatenate([sin_ref[...]] * (MIX_W // LANES), axis=1)

    def rope(t):
        partner = jnp.where(first_half, pltpu.roll(t, MIX_W - HEAD_DIM // 2, 1),
                            pltpu.roll(t, HEAD_DIM // 2, 1))
        return t * cos + partner * sin

    def stack_heads(t):
        return jnp.concatenate([t.astype(BF16)] * N_HEADS, axis=0) * kv_mask

    for b in range(n_batch):
        blk = c_ref[b]
        q = rope(blk[:, 0:MIX_W])
        k = rope(blk[:, MIX_W:2 * MIX_W]) * (HEAD_DIM ** -0.5)
        v = blk[:, 2 * MIX_W:3 * MIX_W]
        g = blk[:, 3 * MIX_W:4 * MIX_W]
        state = state_ref[b]
        ys = []
        for c in range(tb // C):
            rows = slice(c * C, (c + 1) * C)
            qc, kc, vc = q[rows], k[rows], v[rows]
            scores = _bmm_nt(qc, stack_heads(kc)) * dmat_ref[...]
            inner = _bmm(scores, stack_heads(vc))
            cross = _bmm(qc * qdec_ref[...], _head_blockdiag(state, head_mask))
            ys.append(inner + cross)
            full = _mm_tn(kc * kdec_ref[...], vc)
            upd = full[0:HEAD_DIM]
            for h in range(1, N_HEADS):
                upd = jnp.where(lane_head == h, full[h * HEAD_DIM:(h + 1) * HEAD_DIM], upd)
            state = state * cdec_ref[...] + upd
        state_ref[b] = state
        y = jnp.concatenate(ys, axis=0)
        yc = y - _head_sum(y, head_mask) * (1.0 / HEAD_DIM)
        var = _head_sum(yc * yc, head_mask) * (1.0 / HEAD_DIM)
        yn = yc * lax.rsqrt(var + RET_GN_EPS)
        o_ref[b] = (g * _sigmoid(g) * (yn * gn_ref[...])).astype(o_ref.dtype)


def _retention_tables():
    C = RET_CHUNK
    log_gamma = jnp.log1p(-jnp.exp2(-5.0 - jnp.arange(N_HEADS, dtype=F32)))
    idx = jnp.arange(C, dtype=F32)
    dist = idx[:, None] - idx[None, :]
    dmat = jnp.where(dist >= 0, jnp.exp(log_gamma[:, None, None] * jnp.maximum(dist, 0.0)), 0.0)
    dmat = dmat.transpose(1, 0, 2).reshape(C, N_HEADS * C)
    lanes = lambda t: jnp.repeat(t, HEAD_DIM, axis=-1)
    q_dec = lanes(jnp.exp(log_gamma[None, :] * (idx[:, None] + 1.0)))
    k_dec = lanes(jnp.exp(log_gamma[None, :] * (C - 1.0 - idx[:, None])))
    c_dec = lanes(jnp.exp(log_gamma * C)[None, :])
    return dmat, q_dec, k_dec, c_dec


def _retention(ret_c, cos, sin, gn, tb=512):
    B, S, W = ret_c.shape
    tables = _retention_tables()
    return pl.pallas_call(
        functools.partial(_retention_kernel, tb=tb),
        out_shape=jax.ShapeDtypeStruct((B, S, MIX_W), BF16),
        grid=(S // tb,),
        in_specs=[pl.BlockSpec((B, tb, W), lambda i: (0, i, 0)),
                  pl.BlockSpec((tb, LANES), lambda i: (i, 0)),
                  pl.BlockSpec((tb, LANES), lambda i: (i, 0)),
                  _resident((1, MIX_W))] + [_resident(t.shape) for t in tables],
        out_specs=pl.BlockSpec((B, tb, MIX_W), lambda i: (0, i, 0)),
        scratch_shapes=[pltpu.VMEM((B, HEAD_DIM, MIX_W), F32)],
        compiler_params=_cparams(("arbitrary",)),
        name="retention",
    )(ret_c, cos, sin, gn, *tables)


def _rope_tables(S):
    half = HEAD_DIM // 2
    inv_freq = ROPE_BASE ** (-jnp.arange(half, dtype=F32) / half)
    ang = jnp.arange(S, dtype=F32)[:, None] * inv_freq[None, :]
    cos, sin = jnp.cos(ang), jnp.sin(ang)
    cos_h = jnp.concatenate([cos, cos], axis=-1)
    sin_h = jnp.concatenate([-sin, sin], axis=-1)
    reps = LANES // HEAD_DIM
    return jnp.tile(cos_h, (1, reps)), jnp.tile(sin_h, (1, reps))


LOG2E = math.log2(math.e)
FOX_AUG = HEAD_DIM
FOX_TQ = 512
FOX_TK = 256
FOX_VROWS = 80
FOX_UNROLL = 8


def _fox_place_tables():
    H, Dh, L = N_HEADS, HEAD_DIM, LANES
    place_q = np.zeros((H // 2, MIX_W + L, 2 * L), np.float32)
    place_k = np.zeros((H // 2, MIX_W + L, 2 * L), np.float32)
    place_vt = np.zeros((H, FOX_VROWS, MIX_W), np.float32)
    for h in range(H):
        out = (h % 2) * L
        for d in range(Dh):
            place_q[h // 2, h * Dh + d, out + d] = 1.0
            place_k[h // 2, h * Dh + d, out + d] = 1.0
            place_vt[h, d, h * Dh + d] = 1.0
        for t in range(FOX_SPLIT):
            place_q[h // 2, MIX_W + t * H + h, out + FOX_AUG + t] = 1.0
            place_k[h // 2, MIX_W + t * H + h, out + FOX_AUG + FOX_SPLIT + t] = -1.0
    return tuple(jnp.asarray(t, BF16) for t in (place_q, place_k, place_vt))


def _fox_prep_kernel(c_ref, f_ref, bias_ref, qg_ref, kg_ref, pq_ref, pk_ref, pvt_ref, q_ref, k_ref,
                     vt_ref, carry_ref, *, ts, tk):
    @pl.when(pl.program_id(1) == 0)
    def _():
        carry_ref[...] = jnp.zeros_like(carry_ref)

    log_f = _log_sigmoid(f_ref[...] + bias_ref[...])
    tril = (_iota2((LANES, LANES), 0) >= _iota2((LANES, LANES), 1)).astype(BF16)
    carry = carry_ref[0:1, :]
    parts = []
    for r in range(0, ts, LANES):
        part = _mask_mm_f32(tril, log_f[r:r + LANES]) + carry
        carry = part[LANES - 1:LANES, :]
        parts.append(part)
    carry_ref[0:1, :] = carry
    c = jnp.concatenate(parts, axis=0) * LOG2E
    lane_c = _iota2(c.shape, 1)
    csel = jnp.zeros(c.shape, BF16)
    rest = c
    for t in range(FOX_SPLIT):
        term = rest.astype(BF16)
        csel = jnp.where((lane_c >= t * N_HEADS) & (lane_c < (t + 1) * N_HEADS), term, csel)
        rest = rest - term.astype(F32)

    blk = c_ref[...]
    head_mask = _head_mask()

    def rms(t, gain):
        ms = _head_sum(t * t, head_mask) * (1.0 / HEAD_DIM)
        return t * lax.rsqrt(ms + NORM_EPS) * gain

    qn = rms(blk[:, 0:MIX_W], qg_ref[...]) * (LOG2E * HEAD_DIM ** -0.5)
    kn = rms(blk[:, MIX_W:2 * MIX_W], kg_ref[...])
    q_stack = jnp.concatenate([qn.astype(BF16), csel], axis=1)
    k_stack = jnp.concatenate([kn.astype(BF16), csel], axis=1)
    vb = blk[:, 2 * MIX_W:3 * MIX_W].astype(BF16)

    lane = _iota2((1, 2 * LANES), 1) & (LANES - 1)
    ones_q = jnp.where((lane >= FOX_AUG + FOX_SPLIT) & (lane < FOX_AUG + 2 * FOX_SPLIT), 1.0, 0.0)
    ones_k = jnp.where((lane >= FOX_AUG) & (lane < FOX_AUG + FOX_SPLIT), 1.0, 0.0)
    ones_row = _iota2((FOX_VROWS, ts), 0) == FOX_AUG

    for hp in range(N_HEADS // 2):
        q2 = (jnp.dot(q_stack, pq_ref[hp], preferred_element_type=F32) + ones_q).astype(BF16)
        k2 = (jnp.dot(k_stack, pk_ref[hp], preferred_element_type=F32) + ones_k).astype(BF16)
        for sub in range(2):
            q_ref[2 * hp + sub] = q2[:, sub * LANES:(sub + 1) * LANES]
            k_ref[2 * hp + sub] = k2[:, sub * LANES:(sub + 1) * LANES]
    for h in range(N_HEADS):
        v_t = jnp.where(ones_row, 1.0, _mm_nt(pvt_ref[h], vb))
        for kb in range(ts // tk):
            vt_ref[h, kb] = v_t[:, kb * tk:(kb + 1) * tk].astype(BF16)


def _fox_prep(fox_c, f_c, f_bias, q_gain, k_gain, ts=1024):
    B, S, W = fox_c.shape
    tk = FOX_TK
    tables = _fox_place_tables()
    head_spec = pl.BlockSpec((None, N_HEADS, ts, LANES), lambda b, i: (b, 0, i, 0))
    shp = jax.ShapeDtypeStruct((B, N_HEADS, S, LANES), BF16)
    return pl.pallas_call(
        functools.partial(_fox_prep_kernel, ts=ts, tk=tk),
        out_shape=(shp, shp, jax.ShapeDtypeStruct((B, N_HEADS, S // tk, FOX_VROWS, tk), BF16)),
        grid=(B, S // ts),
        in_specs=[pl.BlockSpec((None, ts, W), lambda b, i: (b, i, 0)),
                  pl.BlockSpec((None, ts, LANES), lambda b, i: (b, i, 0)),
                  _resident((1, LANES)), _resident((1, MIX_W)), _resident((1, MIX_W))]
                 + [_resident(t.shape) for t in tables],
        out_specs=(head_spec, head_spec,
                   pl.BlockSpec((None, N_HEADS, ts // tk, FOX_VROWS, tk),
                                lambda b, i: (b, 0, i, 0, 0))),
        scratch_shapes=[pltpu.VMEM((8, LANES), F32)],
        compiler_params=_cparams(("parallel", "arbitrary")),
        name="fox_prep",
    )(fox_c, f_c, f_bias, q_gain, k_gain, *tables)


def _fox_attn_kernel(q_ref, k_ref, vt_ref, o_ref, s0_ref, s1_ref, p0_ref, p1_ref, m_ref, alpha_ref,
                     acc_ref, *, tq, tk):
    qi = pl.program_id(2)
    m_ref[...] = jnp.full(m_ref.shape, -jnp.inf, F32)
    acc_ref[...] = jnp.zeros_like(acc_ref)
    p1_ref[...] = jnp.zeros_like(p1_ref)

    def qk(j, s_ref):
        start = pl.multiple_of(j * tk, tk)
        s_ref[...] = _mm_nt(k_ref[pl.ds(start, tk), :], q_ref[...])

    def pv(j, p_ref):
        return jnp.dot(vt_ref[j], p_ref[...], preferred_element_type=F32)

    def softmax(s_ref, p_ref, key_offset):
        for c in range(0, tq, LANES):
            cols = slice(c, c + LANES)
            s = s_ref[:, cols]
            if key_offset is not None and key_offset + tk - 1 > c:
                keep = _iota2((tk, LANES), 0) + key_offset <= _iota2((tk, LANES), 1) + c
                s = jnp.where(keep, s, -jnp.inf)
            m_old = m_ref[:, cols]
            m_new = jnp.maximum(m_old, jnp.max(s, axis=0, keepdims=True))
            p_ref[:, cols] = jnp.exp2(s - m_new).astype(BF16)
            m_ref[:, cols] = m_new
            alpha_ref[:, cols] = jnp.exp2(m_old - m_new)

    def advance(pv_prev):
        acc_ref[...] = (acc_ref[...] + pv_prev) * alpha_ref[...]

    qk(0, s0_ref)

    def pair(i):
        j = 2 * i
        qk(j + 1, s1_ref)
        pv_prev = pv(jnp.maximum(j - 1, 0), p1_ref)
        softmax(s0_ref, p0_ref, None)
        advance(pv_prev)
        qk(j + 2, s0_ref)
        pv_prev = pv(j, p0_ref)
        softmax(s1_ref, p1_ref, None)
        advance(pv_prev)

    done = 0
    unroll = FOX_UNROLL
    while unroll >= 1:
        def group(i, carry, unroll=unroll):
            for u in range(unroll):
                pair(i + u)
            return carry

        n_groups = (qi - done) // unroll
        lax.fori_loop(0, n_groups, lambda g, c, done=done, unroll=unroll: group(done + g * unroll, c),
                      0)
        done = done + n_groups * unroll
        unroll //= 2

    j = 2 * qi
    qk(j + 1, s1_ref)
    pv_prev = pv(jnp.maximum(j - 1, 0), p1_ref)
    softmax(s0_ref, p0_ref, 0)
    advance(pv_prev)
    pv_prev = pv(j, p0_ref)
    softmax(s1_ref, p1_ref, tk)
    advance(pv_prev)
    acc = acc_ref[...] + pv(j + 1, p1_ref)
    o_ref[...] = (acc[0:HEAD_DIM] / acc[FOX_AUG:FOX_AUG + 1]).astype(o_ref.dtype)


def _fox_attn(q, k, vt):
    B, H, S, W = q.shape
    tq, tk = FOX_TQ, FOX_TK
    assert tq == 2 * tk and S % tq == 0
    return pl.pallas_call(
        functools.partial(_fox_attn_kernel, tq=tq, tk=tk),
        out_shape=jax.ShapeDtypeStruct((B, H, HEAD_DIM, S), BF16),
        grid=(B, H, S // tq),
        in_specs=[pl.BlockSpec((None, None, tq, W), lambda b, h, i: (b, h, i, 0)),
                  pl.BlockSpec((None, None, S, W), lambda b, h, i: (b, h, 0, 0)),
                  pl.BlockSpec((None, None, S // tk, FOX_VROWS, tk),
                               lambda b, h, i: (b, h, 0, 0, 0))],
        out_specs=pl.BlockSpec((None, None, HEAD_DIM, tq), lambda b, h, i: (b, h, 0, i)),
        scratch_shapes=[pltpu.VMEM((tk, tq), F32), pltpu.VMEM((tk, tq), F32),
                        pltpu.VMEM((tk, tq), BF16), pltpu.VMEM((tk, tq), BF16),
                        pltpu.VMEM((1, tq), F32), pltpu.VMEM((1, tq), F32),
                        pltpu.VMEM((FOX_VROWS, tq), F32)],
        compiler_params=_cparams(("parallel", "parallel", "arbitrary")),
        name="fox_attn",
    )(q, k, vt)


def _unit_lower_inverse(a_list, head_mask):
    C, W = a_list[0].shape
    bdiag = lambda t: _head_blockdiag(t, head_mask)
    ii = _iota2((C, W), 0)
    jj = _iota2((C, W), 1) & (C - 1)
    shift = int(math.log2(RWKV_SUB))
    same_blk = (ii >> shift) == (jj >> shift)
    eye = (ii == jj).astype(F32)
    d = [jnp.where(same_blk, a, 0.0) for a in a_list]
    low = [a - di for a, di in zip(a_list, d)]
    x = [eye + di for di in d]
    dp = [_bmm(di, bdiag(di)) for di in d]
    yield
    for _ in range(shift - 2):
        bd = [bdiag(t) for t in dp]
        both = [_bmm(jnp.concatenate([xi, t], axis=0), b) for xi, t, b in zip(x, dp, bd)]
        x = [xi + r[0:C] for xi, r in zip(x, both)]
        dp = [r[C:] for r in both]
        yield
    x = [xi + _bmm(xi, bdiag(t)) for xi, t in zip(x, dp)]
    yield
    n = [_bmm(xi, bdiag(lo)) for xi, lo in zip(x, low)]
    yield
    y = [eye + ni for ni in n]
    npow = n
    for _ in range(int(math.log2(C // RWKV_SUB)) - 1):
        bd = [bdiag(t) for t in npow]
        npow = [_bmm(t, b) for t, b in zip(npow, bd)]
        yield
        y = [yi + _bmm(yi, bdiag(t)) for yi, t in zip(y, npow)]
        yield
    return [_bmm(yi, bdiag(xi)) for yi, xi in zip(y, x)]


RWKV_STASH = (
    ("ar", lambda C, tb: (2 * C, MIX_W), BF16, True),
    ("vc", lambda C, tb: (C, MIX_W), BF16, True),
    ("bd_v", lambda C, tb: (MIX_W, MIX_W), BF16, True),
    ("a_r", lambda C, tb: (C, 2 * MIX_W), BF16, True),
    ("bk_end", lambda C, tb: (2 * C, MIX_W), BF16, True),
    ("t_inv", lambda C, tb: (C, MIX_W), BF16, True),
    ("akv", lambda C, tb: (C, MIX_W), F32, True),
    ("p_end", lambda C, tb: (1, MIX_W), F32, True),
    ("g", lambda C, tb: (tb, MIX_W), F32, False),
    ("bonus", lambda C, tb: (tb, MIX_W), F32, False),
)


def _interleave(*generators):
    alive = list(generators)
    while alive:
        for gen in list(alive):
            try:
                next(gen)
            except StopIteration:
                alive.remove(gen)


def _rwkv_kernel(c_ref, mu_ref, w0_ref, w2_ref, a0_ref, a2_ref, g2_ref, kk_ref, ka_ref, rk_ref,
                 gn_ref, o_ref, state_ref, carry_ref, y_ref, *stash_refs, tb):
    C = RWKV_CHUNK
    W = MIX_W
    n_batch = c_ref.shape[0]
    n_chunks = tb // C
    stash = {name: ref for (name, _, _, _), ref in zip(RWKV_STASH, stash_refs)}
    step = pl.program_id(0)
    n_entries = n_batch * n_chunks
    wr_chunk = (step % 2) * n_entries
    rd_chunk = ((step + 1) % 2) * n_entries
    wr_batch = (step % 2) * n_batch
    rd_batch = ((step + 1) % 2) * n_batch

    @pl.when(pl.program_id(0) == 0)
    def _():
        state_ref[...] = jnp.zeros_like(state_ref)
        carry_ref[...] = jnp.zeros_like(carry_ref)
        for ref in stash_refs:
            ref[...] = jnp.zeros_like(ref)

    head_mask = _head_mask()
    bdiag = lambda t: _head_blockdiag(t, head_mask)
    ii = _iota2((C, W), 0)
    jj = _iota2((C, W), 1) & (C - 1)
    strict = ii > jj
    incl = ii >= jj
    tril = (_iota2((C, C), 0) >= _iota2((C, C), 1)).astype(BF16)
    lane_head = _iota2((HEAD_DIM, W), 1) >> int(math.log2(HEAD_DIM))
    prepared = {}

    def prepare():
        post = []
        pre = []
        for b in range(n_batch):
            cur = c_ref[b]
            prev = jnp.where(_iota2(cur.shape, 0) == 0, carry_ref[b, 0:1, :],
                             pltpu.roll(cur, 1, 0))
            carry_ref[b, 0:1, :] = cur[tb - 1:tb, :]
            mixed = cur + (prev - cur) * mu_ref[...]
            r = mixed[:, 0:W]
            k = mixed[:, W:2 * W]
            v = mixed[:, 2 * W:3 * W]
            xw = mixed[:, 3 * W:3 * W + DECAY_LORA]
            xa = mixed[:, 3 * W + DECAY_LORA:3 * W + DECAY_LORA + AAA_LORA]
            xg = mixed[:, 3 * W + DECAY_LORA + AAA_LORA:]

            log_w = -math.exp(-0.5) * _sigmoid(w0_ref[...] + _mm(jnp.tanh(xw), w2_ref[...]))
            a = _sigmoid(a0_ref[...] + _mm(xa, a2_ref[...]))
            g = _mm(_sigmoid(xg), g2_ref[...])
            kk = k * kk_ref[...]
            k2 = k * (1.0 + (a - 1.0) * ka_ref[...])
            kk = kk / jnp.maximum(jnp.sqrt(_head_sum(kk * kk, head_mask)), 1e-12)
            a_vec = -kk
            b_vec = kk * a
            post.append((g, _head_sum(r * k2 * rk_ref[...], head_mask) * v))
            yield

            for c in range(n_chunks):
                rows = slice(c * C, (c + 1) * C)
                lw = log_w[rows]
                cum = _mask_mm_f32(tril, lw)
                e_pos = jnp.exp(cum)
                e_neg = jnp.exp(-cum)
                p_end = e_pos[C - 1:C, :]
                at = a_vec[rows] * jnp.exp(cum - lw)
                rt = r[rows] * e_pos
                bt = b_vec[rows] * e_neg
                kt = k2[rows] * e_neg
                ar = jnp.concatenate([at, rt], axis=0)
                gb = _bmm_nt(ar, bdiag(bt))
                gk = _bmm_nt(ar, bdiag(kt))
                vc = v[rows]
                pre.append(dict(
                    ar=ar, p_end=p_end, vc=vc, bd_v=bdiag(vc),
                    a_ab=jnp.where(strict, gb[0:C], 0.0), a_ak=jnp.where(strict, gk[0:C], 0.0),
                    a_r=jnp.concatenate([jnp.where(incl, gb[C:], 0.0),
                                         jnp.where(incl, gk[C:], 0.0)], axis=1),
                    bk_end=jnp.concatenate([bt * p_end, kt * p_end], axis=0)))
                yield
        t_inv = yield from _unit_lower_inverse([p["a_ab"] for p in pre], head_mask)
        for p, t in zip(pre, t_inv):
            p["t_inv"] = t
            p["akv"] = _bmm(p["a_ak"], p["bd_v"])
        prepared["pre"] = pre
        prepared["post"] = post

    def recur():
        s_all = [state_ref[b] for b in range(n_batch)]
        batches = range(n_batch)
        for c in range(n_chunks):
            ent = [rd_chunk + b * n_chunks + c for b in batches]
            ars = [_bmm_nt(stash["ar"][e], bdiag(s_all[b])) for b, e in zip(batches, ent)]
            yield
            u = [_bmm(stash["t_inv"][e], bdiag(ars[b][0:C] + stash["akv"][e]))
                 for b, e in zip(batches, ent)]
            yield
            for b, e in zip(batches, ent):
                uv = jnp.concatenate([u[b].astype(BF16), stash["vc"][e]], axis=0)
                y_ref[b, c * C:(c + 1) * C, :] = ars[b][C:] + jnp.dot(
                    stash["a_r"][e], jnp.concatenate([bdiag(u[b]), stash["bd_v"][e]], axis=0),
                    preferred_element_type=F32)
                full = _mm_tn(uv, stash["bk_end"][e])
                upd = full[0:HEAD_DIM]
                for h in range(1, N_HEADS):
                    upd = jnp.where(lane_head == h, full[h * HEAD_DIM:(h + 1) * HEAD_DIM], upd)
                s_all[b] = s_all[b] * stash["p_end"][e] + upd
            yield
        for b in range(n_batch):
            state_ref[b] = s_all[b]
            y = y_ref[b]
            yc = y - _head_sum(y, head_mask) * (1.0 / HEAD_DIM)
            var = _head_sum(yc * yc, head_mask) * (1.0 / HEAD_DIM)
            yn = yc * lax.rsqrt(var + RWKV_GN_EPS) * gn_ref[...]
            o_ref[b] = ((yn + stash["bonus"][rd_batch + b])
                        * stash["g"][rd_batch + b]).astype(o_ref.dtype)
            yield

    _interleave(prepare(), recur())

    for e, p in enumerate(prepared["pre"]):
        for name, _, dtype, per_chunk in RWKV_STASH:
            if per_chunk:
                stash[name][wr_chunk + e] = p[name].astype(dtype)
    for b, (g, bonus) in enumerate(prepared["post"]):
        stash["g"][wr_batch + b] = g
        stash["bonus"][wr_batch + b] = bonus


def _rwkv(rwkv_c, mu, w0, w2, a0, a2, g2, k_k, k_a, r_k, gn, tb=256):
    B, S, W = rwkv_c.shape
    n_blocks = S // tb
    n_entries = B * (tb // RWKV_CHUNK)
    vec = lambda t: _resident((1, t.shape[-1]))
    mat = lambda t: _resident(t.shape)
    stash_shapes = [pltpu.VMEM((2 * (n_entries if per_chunk else B),) + shape(RWKV_CHUNK, tb), dtype)
                    for _, shape, dtype, per_chunk in RWKV_STASH]
    return pl.pallas_call(
        functools.partial(_rwkv_kernel, tb=tb),
        out_shape=jax.ShapeDtypeStruct((B, S, MIX_W), BF16),
        grid=(n_blocks + 1,),
        in_specs=[pl.BlockSpec((B, tb, W), lambda i: (0, jnp.minimum(i, n_blocks - 1), 0)),
                  vec(mu), vec(w0), mat(w2), vec(a0), mat(a2), mat(g2), vec(k_k), vec(k_a),
                  vec(r_k), vec(gn)],
        out_specs=pl.BlockSpec((B, tb, MIX_W), lambda i: (0, jnp.maximum(i - 1, 0), 0)),
        scratch_shapes=[pltpu.VMEM((B, HEAD_DIM, MIX_W), F32),
                        pltpu.VMEM((B, 8, W), F32),
                        pltpu.VMEM((B, tb, MIX_W), F32)] + stash_shapes,
        compiler_params=_cparams(("arbitrary",)),
        name="rwkv7",
    )(rwkv_c, mu, w0, w2, a0, a2, g2, k_k, k_a, r_k, gn)


def _merge_ffn_kernel(x_ref, gate_ref, yret_ref, yfox_ref, yrwkv_ref, pret_ref, pfox_ref, prwkv_ref,
                      wout_ref, g_ref, wgu_ref, wd_ref, o_ref, *, d_model, d_ff, ck):
    D = d_model
    dot = lambda a, b: jnp.dot(a, b, preferred_element_type=F32)
    gates = gate_ref[...]
    fox = lax.dot_general(yfox_ref[...], pfox_ref[...], (((0,), (0,)), ((), ())),
                          preferred_element_type=F32)
    merged = (_sigmoid(gates[:, 0:D]) * dot(yret_ref[...], pret_ref[...])
              + _sigmoid(gates[:, D:2 * D]) * fox
              + _sigmoid(gates[:, 2 * D:3 * D]) * dot(yrwkv_ref[...], prwkv_ref[...]))
    x = x_ref[...] + dot(merged.astype(BF16), wout_ref[...])

    h = (x * lax.rsqrt(jnp.mean(x * x, axis=-1, keepdims=True) + NORM_EPS) * g_ref[...]).astype(BF16)
    acc = x
    for c in range(d_ff // ck):
        lo = c * ck
        gate = dot(h, wgu_ref[:, lo:lo + ck])
        up = dot(h, wgu_ref[:, d_ff + lo:d_ff + lo + ck])
        act = (gate * _sigmoid(gate) * up).astype(BF16)
        acc = acc + dot(act, wd_ref[lo:lo + ck, :])
    o_ref[...] = acc


def _merge_ffn(x, gates, y_ret, y_fox, y_rwkv, p_ret, p_fox, p_rwkv, w_out, gain, w_gate_up, w_down,
               layer, tm=512, ck=256):
    B, S, D = x.shape
    d_ff = w_down.shape[1]
    row = lambda w: pl.BlockSpec((None, tm, w), lambda b, i: (b, i, 0))
    stacked = lambda w: _resident(w.shape[1:], layer)
    weights = (p_ret, p_fox, p_rwkv, w_out, gain, w_gate_up, w_down)
    weight_specs = [stacked(p_ret), stacked(p_fox), stacked(p_rwkv), stacked(w_out),
                    _resident(gain.shape), stacked(w_gate_up), stacked(w_down)]
    return pl.pallas_call(
        functools.partial(_merge_ffn_kernel, d_model=D, d_ff=d_ff, ck=ck),
        out_shape=jax.ShapeDtypeStruct((B, S, D), F32),
        grid=(B, S // tm),
        in_specs=[row(D), row(3 * D), row(MIX_W),
                  pl.BlockSpec((None, MIX_W, tm), lambda b, i: (b, 0, i)),
                  row(MIX_W)] + weight_specs,
        out_specs=row(D),
        compiler_params=_cparams(("parallel", "parallel")),
        name="merge_ffn",
    )(x, gates, y_ret, y_fox, y_rwkv, *weights)


def kernel(x, mix_norm, w_in, ret_gn, fox_q_norm, fox_k_norm, fox_f_bias, rwkv_mu, rwkv_w0, rwkv_w2,
           rwkv_a0, rwkv_a2, rwkv_g2, rwkv_k_k, rwkv_k_a, rwkv_r_k, rwkv_gn, p_ret, p_fox, p_rwkv,
           w_out, ffn_norm, w_gate_up, w_down):
    B, S, D = x.shape
    depth = w_in.shape[0]
    cos, sin = _rope_tables(S)
    o0, o1, o2, o3 = (int(v) for v in IN_OFF)
    f_lo = o1 + FOXQKV_COLS
    row = lambda t: t.reshape(1, -1)
    p_ret, p_fox, p_rwkv, w_out, w_gate_up, w_down = (
        t.astype(BF16) for t in (p_ret, p_fox, p_rwkv, w_out, w_gate_up, w_down))
    w_in_b = w_in.astype(BF16)
    for l in range(depth):
        w_cat = _w_in_relayout(w_in_b[l], f_lo, N_HEADS)
        ret_c, fox_c, rwkv_c, gate_c, f_c = _in_proj(x, row(mix_norm[l]), w_cat)

        y_ret = _retention(ret_c, cos, sin, row(ret_gn[l]))

        f_bias = jnp.pad(jnp.tile(fox_f_bias[l], FOX_SPLIT),
                         (0, LANES - FOX_SPLIT * N_HEADS)).reshape(1, LANES)
        fq, fk, fv = _fox_prep(fox_c, f_c, f_bias, jnp.tile(row(fox_q_norm[l]), (1, N_HEADS)),
                               jnp.tile(row(fox_k_norm[l]), (1, N_HEADS)))
        y_fox = _fox_attn(fq, fk, fv).reshape(B, MIX_W, S)

        y_rwkv = _rwkv(rwkv_c, row(rwkv_mu[l]), row(rwkv_w0[l]), rwkv_w2[l].astype(BF16),
                       row(rwkv_a0[l]), rwkv_a2[l].astype(BF16), rwkv_g2[l].astype(BF16),
                       row(rwkv_k_k[l]), row(rwkv_k_a[l]), row(rwkv_r_k[l]), row(rwkv_gn[l]))

        x = _merge_ffn(x, gate_c, y_ret, y_fox, y_rwkv, p_ret, p_fox, p_rwkv, w_out,
                       row(ffn_norm[l]), w_gate_up, w_down, l)
    return x
```
